```python
import math
import jax, jax.numpy as jnp
from jax import lax
import numpy as np

D_MODEL = 2048
BATCH = 4
SEQ = 8192
DEPTH = 1

CHUNK = 64
Q_BLOCK = 128
ATT_HEADS = 8
ATT_HEAD_DIM = 64
ATT_V_DIM = 2 * ATT_HEAD_DIM
ATT_WIDTH = ATT_HEADS * ATT_V_DIM
SSM_HEADS = 16
SSM_HEAD_DIM = 64
SSM_WIDTH = SSM_HEADS * SSM_HEAD_DIM
SSM_GROUPS = 2
SSM_STATE = 128
SSM_CONV = 4
MIX_WIDTH = ATT_WIDTH + SSM_WIDTH
N_Q = ATT_HEADS * 2 * ATT_HEAD_DIM
N_K = ATT_HEADS * 2 * ATT_HEAD_DIM
N_V = ATT_HEADS * ATT_V_DIM
N_Z = SSM_WIDTH
N_XBC = SSM_WIDTH + 2 * SSM_GROUPS * SSM_STATE
N_DT = SSM_HEADS
IN_COLS = N_Q + N_K + N_V + N_Z + N_XBC + N_DT
N_EXPERT_GROUPS = 4
EXPERTS_PER_GROUP = 8
N_EXPERTS = N_EXPERT_GROUPS * EXPERTS_PER_GROUP
TOP_K = 2
EXPERT_HIDDEN = D_MODEL // 2
MOE_BLOCK = 256
DN_ALPHA = (2 * DEPTH) ** 0.25
DN_BETA = (8 * DEPTH) ** -0.25
LN_EPS = 1e-5
RMS_EPS = 1e-6

kernel_name = "hybrid_diffattn_ssd_hmoe_deepnorm"


def layer_norm(x, g, b):
    xf = x.astype(jnp.float32)
    mu = jnp.mean(xf, -1, keepdims=True)
    var = jnp.mean(jnp.square(xf - mu), -1, keepdims=True)
    return ((xf - mu) * lax.rsqrt(var + LN_EPS) * g + b).astype(x.dtype)


def rms_norm(x, g):
    xf = x.astype(jnp.float32)
    return (xf * lax.rsqrt(jnp.mean(xf * xf, -1, keepdims=True) + RMS_EPS) * g).astype(x.dtype)


def alibi_slopes(n):
    return jnp.exp2(-8.0 * jnp.arange(1, n + 1, dtype=jnp.float32) / n)


def diff_attention(q, k, v, lam, lambda_init, norm_w):
    Bsz, S = q.shape[0], q.shape[1]
    scale = ATT_HEAD_DIM ** -0.5
    slopes = alibi_slopes(ATT_HEADS)
    outs = []
    for qs in range(0, S, Q_BLOCK):
        ke = qs + Q_BLOCK
        s = jnp.einsum('bqhmd,bkhmd->bhmqk', q[:, qs:ke], k[:, :ke],
                       preferred_element_type=jnp.float32) * scale
        t = jnp.arange(qs, ke)
        sp = jnp.arange(ke)
        dist = jnp.abs(t[:, None] - sp[None, :]).astype(jnp.float32)
        allowed = (sp[None, :] // CHUNK) <= (t[:, None] // CHUNK)
        bias = jnp.where(allowed[None], -slopes[:, None, None] * dist[None], -jnp.inf)
        p = jax.nn.softmax(s + bias[None, :, None], axis=-1)
        w = p[:, :, 0] - lam * p[:, :, 1]
        outs.append(jnp.einsum('bhqk,bkhv->bqhv', w.astype(v.dtype), v[:, :ke]))
    o = jnp.concatenate(outs, axis=1)
    o = rms_norm(o, norm_w) * (1.0 - lambda_init)
    return o.reshape(Bsz, S, ATT_WIDTH)


def ssd_mixer(z, xbc, dt_raw, conv_w, conv_b, dt_bias, a_log, d_skip, norm_w):
    Bsz, S, _ = xbc.shape
    f32 = jnp.float32
    xbc = lax.conv_general_dilated(xbc, conv_w[:, None, :].astype(xbc.dtype), window_strides=(1,),
                                   padding=[(SSM_CONV - 1, 0)],
                                   dimension_numbers=('NWC', 'WIO', 'NWC'),
                                   feature_group_count=N_XBC) + conv_b
    xbc = jax.nn.silu(xbc)
    xs, bm, cm = jnp.split(xbc, [SSM_WIDTH, SSM_WIDTH + SSM_GROUPS * SSM_STATE], axis=-1)
    nc = S // CHUNK
    hpg = SSM_HEADS // SSM_GROUPS
    xs = xs.astype(f32).reshape(Bsz, nc, CHUNK, SSM_GROUPS, hpg, SSM_HEAD_DIM)
    bm = bm.astype(f32).reshape(Bsz, nc, CHUNK, SSM_GROUPS, SSM_STATE)
    cm = cm.astype(f32).reshape(Bsz, nc, CHUNK, SSM_GROUPS, SSM_STATE)
    dt = jax.nn.softplus(dt_raw.astype(f32) + dt_bias.astype(f32)).reshape(Bsz, nc, CHUNK, SSM_GROUPS, hpg)
    a_head = -jnp.exp(a_log.astype(f32)).reshape(SSM_GROUPS, hpg)
    acs = jnp.cumsum(jnp.moveaxis(dt * a_head, 2, -1), axis=-1)
    xdt = xs * dt[..., None]
    tril = jnp.tril(jnp.ones((CHUNK, CHUNK), dtype=bool))
    seg = acs[..., :, None] - acs[..., None, :]
    decay_ls = jnp.exp(jnp.where(tril, seg, -jnp.inf))
    cb = jnp.einsum('bclgn,bcsgn->bcgls', cm, bm)
    m = cb[:, :, :, None] * decay_ls
    y_diag = jnp.einsum('bcgels,bcsgep->bclgep', m, xdt)
    decay_states = jnp.moveaxis(jnp.exp(acs[..., -1:] - acs), -1, 2)
    states = jnp.einsum('bclgn,bclgep->bcgepn', bm, xdt * decay_states[..., None])
    chunk_decay = jnp.exp(acs[..., -1])

    def step(h, inp):
        st, dec = inp
        return h * dec[..., None, None] + st, h

    h0 = jnp.zeros((Bsz, SSM_GROUPS, hpg, SSM_HEAD_DIM, SSM_STATE), f32)
    _, prev = lax.scan(step, h0, (jnp.moveaxis(states, 1, 0), jnp.moveaxis(chunk_decay, 1, 0)))
    prev = jnp.moveaxis(prev, 0, 1)
    out_decay = jnp.moveaxis(jnp.exp(acs), -1, 2)
    y_off = jnp.einsum('bclgn,bcgepn->bclgep', cm, prev) * out_decay[..., None]
    y = y_diag + y_off + xs * d_skip.astype(f32).reshape(SSM_GROUPS, hpg)[..., None]
    y = y.reshape(Bsz, S, SSM_WIDTH) * jax.nn.silu(z.astype(f32))
    gsz = SSM_WIDTH // SSM_GROUPS
    y = y.reshape(Bsz, S, SSM_GROUPS, gsz)
    y = y * lax.rsqrt(jnp.mean(y * y, -1, keepdims=True) + RMS_EPS)
    y = y.reshape(Bsz, S, SSM_WIDTH) * norm_w
    return y.astype(z.dtype)


def hierarchical_moe(h, w_rg, b_rg, w_re, b_re, w_gate, w_up, w_down):
    Bsz, S, D = h.shape
    T = Bsz * S
    f32 = jnp.float32
    hf = h.reshape(T, D)
    g_logits = jnp.dot(hf, w_rg, preferred_element_type=f32) + b_rg
    p_group = jax.nn.softmax(g_logits, axis=-1)
    g_prob, g_idx = lax.top_k(p_group, 1)
    g_prob, g_idx = g_prob[:, 0], g_idx[:, 0]
    e_logits = jnp.einsum('td,gde->tge', hf, w_re, preferred_element_type=f32) + b_re
    e_logits = e_logits[jnp.arange(T), g_idx]
    top_logit, top_local = lax.top_k(e_logits, TOP_K)
    top_w = jax.nn.softmax(top_logit, axis=-1) * g_prob[:, None]
    expert_id = g_idx[:, None] * EXPERTS_PER_GROUP + top_local
    n_assign = T * TOP_K
    a_exp = expert_id.reshape(-1).astype(jnp.int32)
    a_tok = jnp.repeat(jnp.arange(T, dtype=jnp.int32), TOP_K)
    a_w = top_w.reshape(-1)
    order = jnp.argsort(a_exp)
    s_exp, s_tok, s_w = a_exp[order], a_tok[order], a_w[order]
    counts = jnp.bincount(a_exp, length=N_EXPERTS)
    starts = jnp.cumsum(counts) - counts
    padded = (counts + MOE_BLOCK - 1) // MOE_BLOCK * MOE_BLOCK
    pad_ends = jnp.cumsum(padded)
    pad_starts = pad_ends - padded
    dest = pad_starts[s_exp] + jnp.arange(n_assign, dtype=jnp.int32) - starts[s_exp]
    n_blocks = -(-n_assign // MOE_BLOCK) + N_EXPERTS
    cap = n_blocks * MOE_BLOCK
    buf_tok = jnp.zeros((cap,), jnp.int32).at[dest].set(s_tok)
    buf_w = jnp.zeros((cap,), f32).at[dest].set(s_w)
    block_exp = jnp.minimum(
        jnp.searchsorted(pad_ends, jnp.arange(n_blocks) * MOE_BLOCK, side='right'), N_EXPERTS - 1)

    def expert_block(args):
        tok, e = args
        xb = hf[tok]
        hid = jax.nn.silu(xb @ w_gate[e]) * (xb @ w_up[e])
        return hid @ w_down[e]

    ys = lax.map(expert_block, (buf_tok.reshape(n_blocks, MOE_BLOCK), block_exp))
    ys = ys.reshape(cap, D) * buf_w[:, None].astype(ys.dtype)
    out = jnp.zeros((T, D), ys.dtype).at[buf_tok].add(ys)
    return out.reshape(Bsz, S, D).astype(h.dtype)


def setup_inputs(seed: int = 0) -> dict:
    key = jax.random.key(seed)
    ks = jax.random.split(key, 26)
    L, D, f32 = DEPTH, D_MODEL, jnp.float32

    def nrm(k, shape, scale):
        return jax.random.normal(k, shape, f32) * scale

    x = nrm(ks[0], (BATCH, SEQ, D), 1.0)
    col_scale = jnp.concatenate([jnp.ones((N_Q + N_K,), f32), jnp.full((N_V,), DN_BETA, f32),
                                 jnp.ones((N_Z + N_XBC + N_DT,), f32)])
    w_in = nrm(ks[1], (L, D, IN_COLS), D ** -0.5) * col_scale
    lambda_q1 = nrm(ks[2], (L, ATT_HEAD_DIM), 0.1)
    lambda_k1 = nrm(ks[3], (L, ATT_HEAD_DIM), 0.1)
    lambda_q2 = nrm(ks[4], (L, ATT_HEAD_DIM), 0.1)
    lambda_k2 = nrm(ks[5], (L, ATT_HEAD_DIM), 0.1)
    attn_norm_w = 1.0 + nrm(ks[6], (L, ATT_V_DIM), 0.02)
    conv_w = nrm(ks[7], (L, SSM_CONV, N_XBC), SSM_CONV ** -0.5)
    conv_b = nrm(ks[8], (L, N_XBC), 0.02)
    dt0 = jnp.exp(jax.random.uniform(ks[9], (L, SSM_HEADS), f32, math.log(1e-3), math.log(1e-1)))
    dt_bias = dt0 + jnp.log(-jnp.expm1(-dt0))
    a_log = jnp.log(jax.random.uniform(ks[10], (L, SSM_HEADS), f32, 1.0, 16.0))
    d_skip = 1.0 + nrm(ks[11], (L, SSM_HEADS), 0.02)
    ssm_norm_w = 1.0 + nrm(ks[12], (L, SSM_WIDTH), 0.02)
    w_out = nrm(ks[13], (L, MIX_WIDTH, D), MIX_WIDTH ** -0.5 * DN_BETA)
    ln1_g = 1.0 + nrm(ks[14], (L, D), 0.02)
    ln1_b = nrm(ks[15], (L, D), 0.02)
    w_router_group = nrm(ks[16], (L, D, N_EXPERT_GROUPS), D ** -0.5)
    b_router_group = nrm(ks[17], (L, N_EXPERT_GROUPS), 0.01)
    w_router_expert = nrm(ks[18], (L, N_EXPERT_GROUPS, D, EXPERTS_PER_GROUP), D ** -0.5)
    b_router_expert = nrm(ks[19], (L, N_EXPERT_GROUPS, EXPERTS_PER_GROUP), 0.01)
    w_gate = nrm(ks[20], (L, N_EXPERTS, D, EXPERT_HIDDEN), D ** -0.5)
    w_up = nrm(ks[21], (L, N_EXPERTS, D, EXPERT_HIDDEN), D ** -0.5 * DN_BETA)
    w_down = nrm(ks[22], (L, N_EXPERTS, EXPERT_HIDDEN, D), EXPERT_HIDDEN ** -0.5 * DN_BETA)
    ln2_g = 1.0 + nrm(ks[23], (L, D), 0.02)
    ln2_b = nrm(ks[24], (L, D), 0.02)
    return {"x": x, "w_in": w_in, "lambda_q1": lambda_q1, "lambda_k1": lambda_k1,
            "lambda_q2": lambda_q2, "lambda_k2": lambda_k2, "attn_norm_w": attn_norm_w,
            "conv_w": conv_w, "conv_b": conv_b, "dt_bias": dt_bias, "a_log": a_log,
            "d_skip": d_skip, "ssm_norm_w": ssm_norm_w, "w_out": w_out,
            "ln1_g": ln1_g, "ln1_b": ln1_b, "w_router_group": w_router_group,
            "b_router_group": b_router_group, "w_router_expert": w_router_expert,
            "b_router_expert": b_router_expert, "w_gate": w_gate, "w_up": w_up,
            "w_down": w_down, "ln2_g": ln2_g, "ln2_b": ln2_b}


def reference(x, w_in, lambda_q1, lambda_k1, lambda_q2, lambda_k2, attn_norm_w, conv_w, conv_b,
              dt_bias, a_log, d_skip, ssm_norm_w, w_out, ln1_g, ln1_b, w_router_group,
              b_router_group, w_router_expert, b_router_expert, w_gate, w_up, w_down,
              ln2_g, ln2_b):
    Bsz, S, _ = x.shape
    f32 = jnp.float32
    offs = [N_Q, N_Q + N_K, N_Q + N_K + N_V, N_Q + N_K + N_V + N_Z,
            N_Q + N_K + N_V + N_Z + N_XBC]
    h = x
    for l in range(DEPTH):
        lambda_init = 0.8 - 0.6 * math.exp(-0.3 * l)
        proj = jnp.einsum('bsd,dc->bsc', h, w_in[l])
        q, k, v, z, xbc, dt_raw = jnp.split(proj, offs, axis=-1)
        q = q.reshape(Bsz, S, ATT_HEADS, 2, ATT_HEAD_DIM)
        k = k.reshape(Bsz, S, ATT_HEADS, 2, ATT_HEAD_DIM)
        v = v.reshape(Bsz, S, ATT_HEADS, ATT_V_DIM)
        lam = (jnp.exp(jnp.sum(lambda_q1[l].astype(f32) * lambda_k1[l].astype(f32)))
               - jnp.exp(jnp.sum(lambda_q2[l].astype(f32) * lambda_k2[l].astype(f32)))
               + lambda_init)
        att = diff_attention(q, k, v, lam, lambda_init, attn_norm_w[l])
        ssm = ssd_mixer(z, xbc, dt_raw, conv_w[l], conv_b[l], dt_bias[l], a_log[l],
                        d_skip[l], ssm_norm_w[l])
        mix = jnp.einsum('bsc,cd->bsd', jnp.concatenate([att, ssm], axis=-1), w_out[l])
        h = layer_norm(DN_ALPHA * h + mix, ln1_g[l], ln1_b[l])
        ffn = hierarchical_moe(h, w_router_group[l], b_router_group[l], w_router_expert[l],
                               b_router_expert[l], w_gate[l], w_up[l], w_down[l])
        h = layer_norm(DN_ALPHA * h + ffn, ln2_g[l], ln2_b[l])
    return h
```

```python
import functools
import math

import jax
import jax.numpy as jnp
from jax import lax
from jax.experimental import pallas as pl
from jax.experimental.pallas import tpu as pltpu

F32 = jnp.float32
BF16 = jnp.bfloat16

CHUNK = 64
ATT_HEADS = 8
ATT_HEAD_DIM = 64
ATT_V_DIM = 128
ATT_WIDTH = ATT_HEADS * ATT_V_DIM
SSM_HEADS = 16
SSM_HEAD_DIM = 64
SSM_WIDTH = SSM_HEADS * SSM_HEAD_DIM
SSM_GROUPS = 2
SSM_STATE = 128
SSM_CONV = 4
N_Q = 1024
N_K = 1024
N_V = 1024
N_Z = 1024
N_XBC = SSM_WIDTH + 2 * SSM_GROUPS * SSM_STATE
N_DT = SSM_HEADS
N_MAIN = N_Q + N_K + N_V + N_Z + N_XBC
N_EXPERT_GROUPS = 4
EXPERTS_PER_GROUP = 8
N_EXPERTS = 32
EXPERT_HIDDEN = 1024
DEPTH = 1
DN_ALPHA = (2 * DEPTH) ** 0.25
LN_EPS = 1e-5
RMS_EPS = 1e-6
LANES = 128

VMEM_LIMIT = 56 * 1024 * 1024

ATT_TQ = 256
SSD_L = 256
MOE_BM = 256
TOK_TB = 256
POS_TB = 512


def _cparams(sem):
    return pltpu.CompilerParams(dimension_semantics=sem, vmem_limit_bytes=VMEM_LIMIT)


def _sigmoid(x):
    return 1.0 / (1.0 + jnp.exp(-x))


def _lane_col(x, idx):
    lane = lax.broadcasted_iota(jnp.int32, x.shape, 1)
    return jnp.sum(jnp.where(lane == idx, x, 0.0), axis=-1, keepdims=True)


def _inproj_kernel(x_ref, w_ref, wdt_ref, o_ref, dt_ref, xb_ref):
    @pl.when(pl.program_id(1) == 0)
    def _():
        xb = x_ref[...].astype(BF16)
        xb_ref[...] = xb
        dt_ref[...] = jnp.dot(xb, wdt_ref[...], preferred_element_type=F32)

    o_ref[...] = jnp.dot(xb_ref[...], w_ref[...], preferred_element_type=F32).astype(o_ref.dtype)


def _in_proj(x2d, w_main, w_dt, tm, tn):
    T, D = x2d.shape
    N = w_main.shape[1]
    return pl.pallas_call(
        _inproj_kernel,
        grid=(T // tm, N // tn),
        in_specs=[
            pl.BlockSpec((tm, D), lambda i, j: (i, 0)),
            pl.BlockSpec((D, tn), lambda i, j: (0, j)),
            pl.BlockSpec((D, LANES), lambda i, j: (0, 0)),
        ],
        out_specs=[
            pl.BlockSpec((tm, tn), lambda i, j: (i, j)),
            pl.BlockSpec((tm, LANES), lambda i, j: (i, 0)),
        ],
        out_shape=[
            jax.ShapeDtypeStruct((T, N), BF16),
            jax.ShapeDtypeStruct((T, LANES), F32),
        ],
        scratch_shapes=[pltpu.VMEM((tm, D), BF16)],
        compiler_params=_cparams(("arbitrary", "arbitrary")),
        name="in_proj",
    )(x2d, w_main, w_dt)


def _attn_kernel(slopes_ref, q_ref, k_ref, vt_ref, lamp_ref, nw_ref, o_ref,
                 rtab_ref, dtab_ref, acc_ref, *, lambda_init):
    h = pl.program_id(1)
    qi = pl.program_id(2)
    tq = q_ref.shape[0]
    tk = tq
    slope = slopes_ref[h]

    @pl.when(qi == 0)
    def _():
        s_rel = lax.broadcasted_iota(jnp.int32, (tk, tq), 0)
        t_rel = lax.broadcasted_iota(jnp.int32, (tk, tq), 1)
        rtab_ref[...] = slope * s_rel.astype(F32)
        allowed = (s_rel // CHUNK) <= (t_rel // CHUNK)
        val = slope * (t_rel - jnp.abs(t_rel - s_rel)).astype(F32)
        dtab_ref[...] = jnp.where(allowed, val, -jnp.inf)

    q = q_ref[...]
    lane = lax.broadcasted_iota(jnp.int32, q.shape, 1)
    scale = ATT_HEAD_DIM ** -0.5
    qs = (q.astype(F32) * scale).astype(BF16)
    zero = jnp.zeros_like(qs)
    qm = (jnp.where(lane < ATT_HEAD_DIM, qs, zero), jnp.where(lane >= ATT_HEAD_DIM, qs, zero))
    nt = (((1,), (1,)), ((), ()))

    kb = k_ref[pl.ds(pl.multiple_of(qi * tk, tk), tk), :]
    vb = vt_ref[qi]
    ms, ls = [], []
    for m in range(2):
        st = lax.dot_general(kb, qm[m], nt, preferred_element_type=F32) + dtab_ref[...]
        mx = jnp.max(st, axis=0, keepdims=True)
        p = jnp.exp(st - mx)
        ms.append(mx)
        ls.append(jnp.sum(p, axis=0, keepdims=True))
        acc_ref[m] = jnp.dot(vb, p.astype(BF16), preferred_element_type=F32)

    def body(j, carry):
        m0, l0, m1, l1 = carry
        kb = k_ref[pl.ds(pl.multiple_of(j * tk, tk), tk), :]
        vb = vt_ref[j]
        cj = -slope * ((qi - j) * tk).astype(F32)
        out = []
        for m, (m_old, l_old) in enumerate(((m0, l0), (m1, l1))):
            st = lax.dot_general(kb, qm[m], nt, preferred_element_type=F32) + rtab_ref[...]
            m_new = jnp.maximum(m_old, jnp.max(st, axis=0, keepdims=True) + cj)
            alpha = jnp.exp(m_old - m_new)
            p = jnp.exp(st - (m_new - cj))
            l_new = alpha * l_old + jnp.sum(p, axis=0, keepdims=True)
            acc_ref[m] = alpha * acc_ref[m] + jnp.dot(vb, p.astype(BF16), preferred_element_type=F32)
            out += [m_new, l_new]
        return tuple(out)

    _, l0, _, l1 = lax.fori_loop(0, qi, body, (ms[0], ls[0], ms[1], ls[1]))

    lamp = lamp_ref[...]
    lam = (jnp.exp(jnp.sum(lamp[0:1] * lamp[1:2], axis=-1, keepdims=True))
           - jnp.exp(jnp.sum(lamp[2:3] * lamp[3:4], axis=-1, keepdims=True)) + lambda_init)
    o = acc_ref[0] / l0 - lam * (acc_ref[1] / l1)
    ms2 = jnp.mean(o * o, axis=0, keepdims=True)
    o = o * lax.rsqrt(ms2 + RMS_EPS) * nw_ref[...] * (1.0 - lambda_init)
    o_ref[...] = o.T.astype(o_ref.dtype)


def _diff_attention(proj, vt, slopes, lamp, nw_col, B, S, lambda_init):
    T = B * S
    tq = ATT_TQ
    nq = S // tq
    kern = functools.partial(_attn_kernel, lambda_init=lambda_init)
    grid_spec = pltpu.PrefetchScalarGridSpec(
        num_scalar_prefetch=1,
        grid=(B, ATT_HEADS, nq),
        in_specs=[
            pl.BlockSpec((tq, LANES), lambda b, h, i, s: (b * nq + i, h)),
            pl.BlockSpec((S, LANES), lambda b, h, i, s: (b, N_Q // LANES + h)),
            pl.BlockSpec((None, None, nq, ATT_V_DIM, tq), lambda b, h, i, s: (b, h, 0, 0, 0)),
            pl.BlockSpec((8, LANES), lambda b, h, i, s: (0, 0)),
            pl.BlockSpec((ATT_V_DIM, 1), lambda b, h, i, s: (0, 0)),
        ],
        out_specs=pl.BlockSpec((tq, ATT_V_DIM), lambda b, h, i, s: (b * nq + i, h)),
        scratch_shapes=[
            pltpu.VMEM((tq, tq), F32),
            pltpu.VMEM((tq, tq), F32),
            pltpu.VMEM((2, ATT_V_DIM, tq), F32),
        ],
    )
    return pl.pallas_call(
        kern,
        grid_spec=grid_spec,
        out_shape=jax.ShapeDtypeStruct((T, ATT_WIDTH), BF16),
        compiler_params=_cparams(("arbitrary", "arbitrary", "arbitrary")),
        name="diff_attention",
    )(slopes, proj, proj, vt, lamp, nw_col)


def _expand_heads(v, e):
    hi = v.astype(BF16)
    lo = (v - hi.astype(F32)).astype(BF16)
    return jnp.dot(hi, e, preferred_element_type=F32) + jnp.dot(lo, e, preferred_element_type=F32)


def _ssd_kernel(z_ref, xs_ref, b_ref, c_ref, dt_ref, cw_ref, cb_ref, dtb_ref, alog_ref,
                dskip_ref, nw_ref, e_ref, o_ref, ext_ref, st_ref, y_ref):
    blk = pl.program_id(1)
    L = z_ref.shape[0]
    nchunk = L // CHUNK
    gw = SSM_WIDTH // SSM_GROUPS
    hpg = SSM_HEADS // SSM_GROUPS

    @pl.when(blk == 0)
    def _():
        ext_ref[0:8, :] = jnp.zeros((8, N_XBC), F32)
        st_ref[...] = jnp.zeros_like(st_ref)

    cur = jnp.concatenate([xs_ref[...], b_ref[...], c_ref[...]], axis=1).astype(F32)
    ext_ref[8:, :] = cur
    cw = cw_ref[...]
    conv = cb_ref[...] + cw[3:4] * cur
    for j in range(SSM_CONV - 1):
        conv = conv + cw[j:j + 1] * ext_ref[pl.ds(8 - (SSM_CONV - 1) + j, L), :]
    ext_ref[0:8, :] = cur[L - 8:, :]
    xbc = conv * _sigmoid(conv)
    xs = xbc[:, :SSM_WIDTH]
    bmb = xbc[:, SSM_WIDTH:SSM_WIDTH + SSM_GROUPS * SSM_STATE].astype(BF16)
    cmb = xbc[:, SSM_WIDTH + SSM_GROUPS * SSM_STATE:].astype(BF16)
    xsb = xs.astype(BF16)

    lane1 = lax.broadcasted_iota(jnp.int32, (1, LANES), 1)
    dtx = dt_ref[...] + dtb_ref[...]
    dtp = jnp.maximum(dtx, 0.0) + jnp.log1p(jnp.exp(-jnp.abs(dtx)))
    a_head = jnp.where(lane1 < SSM_HEADS, -jnp.exp(alog_ref[...]), 0.0)
    acs = dtp * a_head
    row_in_chunk = lax.broadcasted_iota(jnp.int32, (L, LANES), 0) & (CHUNK - 1)
    k = 1
    while k < CHUNK:
        acs = acs + jnp.where(row_in_chunk >= k, pltpu.roll(acs, k, axis=0), 0.0)
        k *= 2
    acs_t = acs.T
    dt_t = dtp.T

    e = e_ref[...]
    acs_last = jnp.concatenate(
        [jnp.broadcast_to(acs[c * CHUNK + CHUNK - 1:(c + 1) * CHUNK, :], (CHUNK, LANES)) for c in range(nchunk)],
        axis=0)
    w_exp = _expand_heads(dtp * jnp.exp(acs_last - acs), e)
    od_exp = _expand_heads(jnp.exp(acs), e)
    row8 = lax.broadcasted_iota(jnp.int32, (8, LANES), 0)
    cd8 = jnp.zeros((8, LANES), F32)
    for c in range(nchunk):
        cd8 = jnp.where(row8 == c, jnp.exp(acs[c * CHUNK + CHUNK - 1:(c + 1) * CHUNK, :]), cd8)
    cd_exp = _expand_heads(cd8, e)
    xw = (xs * w_exp).astype(BF16)

    tn = (((0,), (0,)), ((), ()))
    for c in range(nchunk):
        r0 = c * CHUNK
        for g in range(SSM_GROUPS):
            st = st_ref[g]
            cg = cmb[r0:r0 + CHUNK, g * SSM_STATE:(g + 1) * SSM_STATE]
            bg = bmb[r0:r0 + CHUNK, g * SSM_STATE:(g + 1) * SSM_STATE]
            y_ref[r0:r0 + CHUNK, g * gw:(g + 1) * gw] = jnp.dot(cg, st.astype(BF16), preferred_element_type=F32)
            snew = lax.dot_general(bg, xw[r0:r0 + CHUNK, g * gw:(g + 1) * gw], tn, preferred_element_type=F32)
            st_ref[g] = st * cd_exp[c:c + 1, g * gw:(g + 1) * gw] + snew
    y = y_ref[...] * od_exp + xs * dskip_ref[...]

    pair = 2 * CHUNK
    li = lax.broadcasted_iota(jnp.int32, (pair, pair), 0)
    si = lax.broadcasted_iota(jnp.int32, (pair, pair), 1)
    mask2 = (li >= si) & ((si >= CHUNK) | (li < CHUNK))
    lanep = lax.broadcasted_iota(jnp.int32, (pair, LANES), 1)
    nt = (((1,), (1,)), ((), ()))
    for pp in range(L // pair):
        r0 = pp * pair
        acs_p = acs[r0:r0 + pair, :]
        for g in range(SSM_GROUPS):
            cb2 = lax.dot_general(cmb[r0:r0 + pair, g * SSM_STATE:(g + 1) * SSM_STATE],
                                  bmb[r0:r0 + pair, g * SSM_STATE:(g + 1) * SSM_STATE],
                                  nt, preferred_element_type=F32)
            for hh in range(hpg // 2):
                hp = g * (hpg // 2) + hh
                mats = []
                for u in range(2):
                    hd = 2 * hp + u
                    seg = _lane_col(acs_p, hd) - acs_t[hd:hd + 1, r0:r0 + pair]
                    decay = jnp.exp(jnp.where(mask2, seg, -jnp.inf))
                    mats.append((cb2 * decay * dt_t[hd:hd + 1, r0:r0 + pair]).astype(BF16))
                lhs = jnp.concatenate(mats, axis=1)
                xp = xsb[r0:r0 + pair, hp * LANES:(hp + 1) * LANES]
                zero = jnp.zeros_like(xp)
                rhs = jnp.concatenate([jnp.where(lanep < SSM_HEAD_DIM, xp, zero),
                                       jnp.where(lanep >= SSM_HEAD_DIM, xp, zero)], axis=0)
                y_ref[r0:r0 + pair, hp * LANES:(hp + 1) * LANES] = jnp.dot(lhs, rhs, preferred_element_type=F32)
    y = y + y_ref[...]

    z = z_ref[...].astype(F32)
    y = y * (z * _sigmoid(z))
    outs = []
    for g in range(SSM_GROUPS):
        yg = y[:, g * gw:(g + 1) * gw]
        outs.append(yg * lax.rsqrt(jnp.mean(yg * yg, axis=-1, keepdims=True) + RMS_EPS))
    o_ref[...] = (jnp.concatenate(outs, axis=1) * nw_ref[...]).astype(o_ref.dtype)


def _ssd_mixer(proj, dt_raw, cw8, cb, dtb, alog, dskip_exp, nw, e_mat, B, S):
    T = B * S
    L = SSD_L
    nb = S // L
    row = lambda b, i: b * nb + i
    col0 = (N_Q + N_K + N_V + N_Z)
    const = lambda shape: pl.BlockSpec(shape, lambda b, i: (0, 0))
    return pl.pallas_call(
        _ssd_kernel,
        grid=(B, nb),
        in_specs=[
            pl.BlockSpec((L, N_Z), lambda b, i: (row(b, i), (N_Q + N_K + N_V) // N_Z)),
            pl.BlockSpec((L, SSM_WIDTH), lambda b, i: (row(b, i), col0 // SSM_WIDTH)),
            pl.BlockSpec((L, 256), lambda b, i: (row(b, i), (col0 + SSM_WIDTH) // 256)),
            pl.BlockSpec((L, 256), lambda b, i: (row(b, i), (col0 + SSM_WIDTH + 256) // 256)),
            pl.BlockSpec((L, LANES), lambda b, i: (row(b, i), 0)),
            const((8, N_XBC)),
            const((1, N_XBC)),
            const((1, LANES)),
            const((1, LANES)),
            const((1, SSM_WIDTH)),
            const((1, SSM_WIDTH)),
            const((LANES, SSM_WIDTH)),
        ],
        out_specs=pl.BlockSpec((L, SSM_WIDTH), lambda b, i: (row(b, i), 0)),
        out_shape=jax.ShapeDtypeStruct((T, SSM_WIDTH), BF16),
        scratch_shapes=[
            pltpu.VMEM((L + 8, N_XBC), F32),
            pltpu.VMEM((SSM_GROUPS, SSM_STATE, SSM_WIDTH // SSM_GROUPS), F32),
            pltpu.VMEM((L, SSM_WIDTH), F32),
        ],
        compiler_params=_cparams(("arbitrary", "arbitrary")),
        name="ssd_mixer",
    )(proj, proj, proj, proj, dt_raw, cw8, cb, dtb, alog, dskip_exp, nw, e_mat)


def _layer_norm_rows(r, g, b):
    mu = jnp.mean(r, axis=-1, keepdims=True)
    d = r - mu
    var = jnp.mean(d * d, axis=-1, keepdims=True)
    return d * lax.rsqrt(var + LN_EPS) * g + b


def _outproj_kernel(att_ref, ssm_ref, x_ref, wa_ref, ws_ref, g_ref, b_ref, wr_ref, br_ref,
                    h_ref, eid_ref, ew_ref):
    mix = (jnp.dot(att_ref[...], wa_ref[...], preferred_element_type=F32)
           + jnp.dot(ssm_ref[...], ws_ref[...], preferred_element_type=F32))
    h = _layer_norm_rows(DN_ALPHA * x_ref[...] + mix, g_ref[...], b_ref[...])
    h_ref[...] = h

    logits = jnp.dot(h.astype(BF16), wr_ref[...], preferred_element_type=F32) + br_ref[...]
    lane = lax.broadcasted_iota(jnp.int32, logits.shape, 1)
    lanef = lane.astype(F32)
    big = float(LANES)
    gl = jnp.where(lane < N_EXPERT_GROUPS, logits, -jnp.inf)
    gmax = jnp.max(gl, axis=-1, keepdims=True)
    g_prob = 1.0 / jnp.sum(jnp.exp(gl - gmax), axis=-1, keepdims=True)
    gidx = jnp.min(jnp.where(gl == gmax, lanef, big), axis=-1, keepdims=True)
    lo = N_EXPERT_GROUPS + EXPERTS_PER_GROUP * gidx
    el = jnp.where((lanef >= lo) & (lanef < lo + EXPERTS_PER_GROUP), logits, -jnp.inf)
    t1 = jnp.max(el, axis=-1, keepdims=True)
    i1 = jnp.min(jnp.where(el == t1, lanef, big), axis=-1, keepdims=True)
    el2 = jnp.where(lanef == i1, -jnp.inf, el)
    t2 = jnp.max(el2, axis=-1, keepdims=True)
    i2 = jnp.min(jnp.where(el2 == t2, lanef, big), axis=-1, keepdims=True)
    ex = jnp.exp(t2 - t1)
    w1 = g_prob / (1.0 + ex)
    w2 = g_prob * ex / (1.0 + ex)
    eid = jnp.where(lane == 0, i1 - N_EXPERT_GROUPS, jnp.where(lane == 1, i2 - N_EXPERT_GROUPS, 0.0))
    eid_ref[...] = eid.astype(jnp.int32)
    ew_ref[...] = jnp.where(lane == 0, w1, jnp.where(lane == 1, w2, 0.0))


def _out_proj(att, ssm, x2d, wa, ws, g, b, wr, br, tm):
    T, D = x2d.shape
    const = lambda shape: pl.BlockSpec(shape, lambda i: (0, 0))
    rows = lambda w: pl.BlockSpec((tm, w), lambda i: (i, 0))
    return pl.pallas_call(
        _outproj_kernel,
        grid=(T // tm,),
        in_specs=[rows(ATT_WIDTH), rows(SSM_WIDTH), rows(D), const((ATT_WIDTH, D)), const((SSM_WIDTH, D)),
                  const((1, D)), const((1, D)), const((D, LANES)), const((1, LANES))],
        out_specs=[rows(D), rows(LANES), rows(LANES)],
        out_shape=[jax.ShapeDtypeStruct((T, D), F32),
                   jax.ShapeDtypeStruct((T, LANES), jnp.int32),
                   jax.ShapeDtypeStruct((T, LANES), F32)],
        compiler_params=_cparams(("arbitrary",)),
        name="out_proj_ln1_router",
    )(att, ssm, x2d, wa, ws, g, b, wr, br)


def _pos_kernel(eid_ref, stril_ref, dest_ref, pend_ref, tot_ref, run_ref, pstart_ref):
    ph = pl.program_id(0)
    i = pl.program_id(1)
    tb = eid_ref.shape[0]
    lane = lax.broadcasted_iota(jnp.int32, (tb, LANES), 1)
    lanef = lane.astype(F32)
    ef = eid_ref[...].astype(F32)
    oh1 = (lanef == _lane_col(ef, 0)).astype(F32)
    oh2 = (lanef == _lane_col(ef, 1)).astype(F32)
    cnt = oh1 + oh2

    @pl.when((ph == 0) & (i == 0))
    def _():
        tot_ref[...] = jnp.zeros_like(tot_ref)

    @pl.when(ph == 0)
    def _():
        tot_ref[...] += jnp.sum(cnt, axis=0, keepdims=True)

    @pl.when((ph == 1) & (i == 0))
    def _():
        tot = jnp.broadcast_to(tot_ref[...], (8, LANES))
        padded = jnp.floor((tot + (MOE_BM - 1)) * (1.0 / MOE_BM)) * MOE_BM
        lane8 = lax.broadcasted_iota(jnp.int32, (8, LANES), 1)
        ends = padded
        k = 1
        while k < LANES:
            ends = ends + jnp.where(lane8 >= k, pltpu.roll(ends, k, axis=1), 0.0)
            k *= 2
        pend_ref[...] = ends
        pstart_ref[...] = (ends - padded)[0:1]
        run_ref[...] = jnp.zeros_like(run_ref)

    @pl.when(ph == 1)
    def _():
        pre = jnp.dot(stril_ref[...], cnt.astype(BF16), preferred_element_type=F32)
        slot = pstart_ref[...] + run_ref[...] + pre
        d1 = jnp.sum(oh1 * slot, axis=-1, keepdims=True)
        d2 = jnp.sum(oh2 * slot, axis=-1, keepdims=True)
        dest_ref[...] = jnp.where(lane == 0, d1, jnp.where(lane == 1, d2, 0.0)).astype(jnp.int32)
        run_ref[...] += jnp.sum(cnt, axis=0, keepdims=True)


def _positions(eid, stril):
    T = eid.shape[0]
    tb = POS_TB
    return pl.pallas_call(
        _pos_kernel,
        grid=(2, T // tb),
        in_specs=[pl.BlockSpec((tb, LANES), lambda p, i: (i, 0)),
                  pl.BlockSpec((tb, tb), lambda p, i: (0, 0))],
        out_specs=[pl.BlockSpec((tb, LANES), lambda p, i: (i * p, 0)),
                   pl.BlockSpec((8, LANES), lambda p, i: (0, 0))],
        out_shape=[jax.ShapeDtypeStruct((T, LANES), jnp.int32),
                   jax.ShapeDtypeStruct((8, LANES), F32)],
        scratch_shapes=[pltpu.VMEM((1, LANES), F32), pltpu.VMEM((1, LANES), F32), pltpu.VMEM((1, LANES), F32)],
        compiler_params=_cparams(("arbitrary", "arbitrary")),
        name="dispatch_positions",
    )(eid, stril)


def _row_copy(src, s, dst, d, sem):
    return pltpu.make_async_copy(src.at[pl.ds(s, 1)], dst.at[pl.ds(d, 1)], sem)


def _scatter_kernel(dest_ref, h_hbm, init_hbm, xs_hbm, sem):
    del init_hbm
    i = pl.program_id(0)
    n = pl.num_programs(0)
    tb = dest_ref.shape[2] // 2

    def issue(r, c):
        t = i * tb + r
        _row_copy(h_hbm, t, xs_hbm, dest_ref[0, 0, 2 * r], sem).start()
        _row_copy(h_hbm, t, xs_hbm, dest_ref[0, 0, 2 * r + 1], sem).start()
        return c

    def drain(r, c):
        _row_copy(h_hbm, 0, xs_hbm, 0, sem).wait()
        _row_copy(h_hbm, 0, xs_hbm, 0, sem).wait()
        return c

    lax.fori_loop(0, tb, issue, 0)

    @pl.when(i > 0)
    def _():
        lax.fori_loop(0, tb, drain, 0)

    @pl.when(i == n - 1)
    def _():
        lax.fori_loop(0, tb, drain, 0)


def _dispatch(dest3, h, cap):
    T, D = h.shape
    nb = dest3.shape[0]
    init = jnp.zeros((cap, D), h.dtype)
    return pl.pallas_call(
        _scatter_kernel,
        grid=(nb,),
        in_specs=[pl.BlockSpec((1, 1, dest3.shape[2]), lambda i: (i, 0, 0), memory_space=pltpu.SMEM),
                  pl.BlockSpec(memory_space=pl.ANY),
                  pl.BlockSpec(memory_space=pl.ANY)],
        out_specs=pl.BlockSpec(memory_space=pl.ANY),
        out_shape=jax.ShapeDtypeStruct((cap, D), h.dtype),
        scratch_shapes=[pltpu.SemaphoreType.DMA],
        input_output_aliases={2: 0},
        compiler_params=_cparams(("arbitrary",)),
        name="moe_dispatch",
    )(dest3, h, init)


def _expert_kernel(bexp_ref, nused_ref, x_ref, wg_ref, wu_ref, wd_ref, y_ref):
    i = pl.program_id(0)

    @pl.when(i < nused_ref[0])
    def _():
        x = x_ref[...].astype(BF16)
        gate = jnp.dot(x, wg_ref[0], preferred_element_type=F32)
        up = jnp.dot(x, wu_ref[0], preferred_element_type=F32)
        hid = (gate * _sigmoid(gate) * up).astype(BF16)
        y_ref[...] = jnp.dot(hid, wd_ref[0], preferred_element_type=F32)

    @pl.when(i >= nused_ref[0])
    def _():
        y_ref[...] = jnp.zeros_like(y_ref)


def _expert_mlp(bexp, nused, xs, wg, wu, wd):
    cap, D = xs.shape
    bm = MOE_BM
    H = wg.shape[2]
    grid_spec = pltpu.PrefetchScalarGridSpec(
        num_scalar_prefetch=2,
        grid=(cap // bm,),
        in_specs=[
            pl.BlockSpec((bm, D), lambda i, be, nu: (i, 0)),
            pl.BlockSpec((1, D, H), lambda i, be, nu: (be[i], 0, 0)),
            pl.BlockSpec((1, D, H), lambda i, be, nu: (be[i], 0, 0)),
            pl.BlockSpec((1, H, D), lambda i, be, nu: (be[i], 0, 0)),
        ],
        out_specs=pl.BlockSpec((bm, D), lambda i, be, nu: (i, 0)),
    )
    return pl.pallas_call(
        _expert_kernel,
        grid_spec=grid_spec,
        out_shape=jax.ShapeDtypeStruct((cap, D), F32),
        compiler_params=_cparams(("arbitrary",)),
        name="expert_mlp",
    )(bexp, nused, xs, wg, wu, wd)


def _combine_kernel(dest_ref, h_ref, ew_ref, g_ref, b_ref, y_hbm, o_ref, ybuf, sem):
    tb = h_ref.shape[0]

    def issue(r, c):
        _row_copy(y_hbm, dest_ref[0, 0, 2 * r], ybuf.at[0], r, sem).start()
        _row_copy(y_hbm, dest_ref[0, 0, 2 * r + 1], ybuf.at[1], r, sem).start()
        return c

    def drain(r, c):
        _row_copy(y_hbm, 0, ybuf.at[0], 0, sem).wait()
        _row_copy(y_hbm, 0, ybuf.at[1], 0, sem).wait()
        return c

    lax.fori_loop(0, tb, issue, 0)
    lax.fori_loop(0, tb, drain, 0)
    ew = ew_ref[...]
    ffn = _lane_col(ew, 0) * ybuf[0] + _lane_col(ew, 1) * ybuf[1]
    o_ref[...] = _layer_norm_rows(DN_ALPHA * h_ref[...] + ffn, g_ref[...], b_ref[...])


def _combine(dest3, h, ew, g, b, y):
    T, D = h.shape
    tb = dest3.shape[2] // 2
    return pl.pallas_call(
        _combine_kernel,
        grid=(T // tb,),
        in_specs=[pl.BlockSpec((1, 1, 2 * tb), lambda i: (i, 0, 0), memory_space=pltpu.SMEM),
                  pl.BlockSpec((tb, D), lambda i: (i, 0)),
                  pl.BlockSpec((tb, LANES), lambda i: (i, 0)),
                  pl.BlockSpec((1, D), lambda i: (0, 0)),
                  pl.BlockSpec((1, D), lambda i: (0, 0)),
                  pl.BlockSpec(memory_space=pl.ANY)],
        out_specs=pl.BlockSpec((tb, D), lambda i: (i, 0)),
        out_shape=jax.ShapeDtypeStruct((T, D), F32),
        scratch_shapes=[pltpu.VMEM((2, tb, D), F32), pltpu.SemaphoreType.DMA],
        compiler_params=_cparams(("arbitrary",)),
        name="moe_combine_ln2",
    )(dest3, h, ew, g, b, y)


def _pad_lanes(v, n=LANES):
    v = v.reshape(1, -1).astype(F32)
    return jnp.pad(v, ((0, 0), (0, n - v.shape[1])))


def kernel(x, w_in, lambda_q1, lambda_k1, lambda_q2, lambda_k2, attn_norm_w, conv_w, conv_b, dt_bias, a_log, d_skip, ssm_norm_w, w_out, ln1_g, ln1_b, w_router_group, b_router_group, w_router_expert, b_router_expert, w_gate, w_up, w_down, ln2_g, ln2_b):
    B, S, D = x.shape
    T = B * S
    assert w_in.shape[0] == DEPTH == 1
    assert S % ATT_TQ == 0 and S % SSD_L == 0 and T % POS_TB == 0 and T % TOK_TB == 0
    l = 0
    lambda_init = 0.8 - 0.6 * math.exp(-0.3 * l)
    x2d = x.reshape(T, D)

    w_main = w_in[l][:, :N_MAIN].astype(BF16)
    w_dt = jnp.pad(w_in[l][:, N_MAIN:], ((0, 0), (0, LANES - N_DT))).astype(BF16)
    slopes = jnp.exp2(-8.0 * jnp.arange(1, ATT_HEADS + 1, dtype=F32) / ATT_HEADS)
    lamp = jnp.concatenate([_pad_lanes(lambda_q1[l]), _pad_lanes(lambda_k1[l]),
                            _pad_lanes(lambda_q2[l]), _pad_lanes(lambda_k2[l]),
                            jnp.zeros((4, LANES), F32)], axis=0)
    nw_col = attn_norm_w[l].astype(F32).reshape(ATT_V_DIM, 1)
    cw8 = jnp.pad(conv_w[l].astype(F32), ((0, 8 - SSM_CONV), (0, 0)))
    cb = conv_b[l].astype(F32).reshape(1, N_XBC)
    dskip_exp = jnp.repeat(d_skip[l].astype(F32), SSM_HEAD_DIM).reshape(1, SSM_WIDTH)
    ssm_nw = ssm_norm_w[l].astype(F32).reshape(1, SSM_WIDTH)
    head_of_lane = jnp.arange(SSM_WIDTH, dtype=jnp.int32) // SSM_HEAD_DIM
    e_mat = (jnp.arange(LANES, dtype=jnp.int32)[:, None] == head_of_lane[None, :]).astype(BF16)
    wa = w_out[l][:ATT_WIDTH].astype(BF16)
    ws = w_out[l][ATT_WIDTH:].astype(BF16)
    wr = jnp.concatenate(
        [w_router_group[l], jnp.transpose(w_router_expert[l], (1, 0, 2)).reshape(D, N_EXPERTS)], axis=1)
    wr = jnp.pad(wr, ((0, 0), (0, LANES - wr.shape[1]))).astype(BF16)
    br = _pad_lanes(jnp.concatenate([b_router_group[l], b_router_expert[l].reshape(-1)]))
    row = lambda v: v.astype(F32).reshape(1, D)

    tm_in = 1024 if T % 1024 == 0 else 256
    proj, dt_raw = _in_proj(x2d, w_main, w_dt, tm_in, 512)
    nq = S // ATT_TQ
    v = proj[:, N_Q + N_K:N_Q + N_K + N_V].reshape(B, nq, ATT_TQ, ATT_HEADS, ATT_V_DIM)
    vt = jnp.transpose(v, (0, 3, 1, 4, 2))
    att = _diff_attention(proj, vt, slopes, lamp, nw_col, B, S, lambda_init)
    ssm = _ssd_mixer(proj, dt_raw, cw8, cb, _pad_lanes(dt_bias[l]), _pad_lanes(a_log[l]),
                     dskip_exp, ssm_nw, e_mat, B, S)
    h1, eid, ew = _out_proj(att, ssm, x2d, wa, ws, row(ln1_g[l]), row(ln1_b[l]), wr, br, 256)

    stril = (jnp.arange(POS_TB)[:, None] > jnp.arange(POS_TB)[None, :]).astype(BF16)
    dest, pend = _positions(eid, stril)
    nblk = (T * 2) // MOE_BM + N_EXPERTS
    cap = nblk * MOE_BM
    pad_ends = pend[0, :N_EXPERTS].astype(jnp.int32)
    blk_start = jnp.arange(nblk, dtype=jnp.int32) * MOE_BM
    nused = (pad_ends[N_EXPERTS - 1] // MOE_BM).astype(jnp.int32)
    last_used = jnp.maximum(nused - 1, 0) * MOE_BM
    bexp = jnp.sum(pad_ends[None, :] <= jnp.minimum(blk_start, last_used)[:, None], axis=1).astype(jnp.int32)
    bexp = jnp.minimum(bexp, N_EXPERTS - 1)
    dest3 = dest[:, :2].reshape(T // TOK_TB, 1, 2 * TOK_TB)
    xs_sorted = _dispatch(dest3, h1, cap)
    y_sorted = _expert_mlp(bexp, nused.reshape(1), xs_sorted,
                           w_gate[l].astype(BF16), w_up[l].astype(BF16), w_down[l].astype(BF16))
    out = _combine(dest3, h1, ew, row(ln2_g[l]), row(ln2_b[l]), y_sorted)
    return out.reshape(B, S, D)
```

```python
import functools
import math

import jax
import jax.numpy as jnp
from jax import lax
from jax.experimental import pallas as pl
from jax.experimental.pallas import tpu as pltpu

F32 = jnp.float32
BF16 = jnp.bfloat16

CHUNK = 64
ATT_HEADS = 8
ATT_HEAD_DIM = 64
ATT_V_DIM = 128
ATT_WIDTH = ATT_HEADS * ATT_V_DIM
ATT_VT_ROWS = ATT_V_DIM + 16
LOG2E = 1.4426950408889634
SSM_HEADS = 16
SSM_HEAD_DIM = 64
SSM_WIDTH = SSM_HEADS * SSM_HEAD_DIM
SSM_GROUPS = 2
SSM_STATE = 128
SSM_CONV = 4
N_Q = 1024
N_K = 1024
N_V = 1024
N_Z = 1024
N_XBC = SSM_WIDTH + 2 * SSM_GROUPS * SSM_STATE
N_DT = SSM_HEADS
N_MAIN = N_Q + N_K + N_V + N_Z + N_XBC
N_EXPERT_GROUPS = 4
EXPERTS_PER_GROUP = 8
N_EXPERTS = 32
EXPERT_HIDDEN = 1024
DEPTH = 1
DN_ALPHA = (2 * DEPTH) ** 0.25
LN_EPS = 1e-5
RMS_EPS = 1e-6
LANES = 128

VMEM_LIMIT = 56 * 1024 * 1024

ATT_TQ = 512
SSD_L = 256
MOE_BM = 256
TOK_TB = 256
POS_TB = 512


def _cparams(sem, flags=None):
    return pltpu.CompilerParams(dimension_semantics=sem, vmem_limit_bytes=VMEM_LIMIT, flags=flags)


def _sigmoid(x):
    return 1.0 / (1.0 + jnp.exp(-x))


def _lane_col(x, idx):
    lane = lax.broadcasted_iota(jnp.int32, x.shape, 1)
    return jnp.sum(jnp.where(lane == idx, x, 0.0), axis=-1, keepdims=True)


def _inproj_kernel(x_ref, w_ref, wdt_ref, o_ref, dt_ref, xb_ref):
    @pl.when(pl.program_id(1) == 0)
    def _():
        xb = x_ref[...].astype(BF16)
        xb_ref[...] = xb
        dt_ref[...] = jnp.dot(xb, wdt_ref[...], preferred_element_type=F32)

    o_ref[...] = jnp.dot(xb_ref[...], w_ref[...], preferred_element_type=F32).astype(o_ref.dtype)


def _in_proj(x2d, w_main, w_dt, tm, tn):
    T, D = x2d.shape
    N = w_main.shape[1]
    return pl.pallas_call(
        _inproj_kernel,
        grid=(T // tm, N // tn),
        in_specs=[
            pl.BlockSpec((tm, D), lambda i, j: (i, 0)),
            pl.BlockSpec((D, tn), lambda i, j: (0, j)),
            pl.BlockSpec((D, LANES), lambda i, j: (0, 0)),
        ],
        out_specs=[
            pl.BlockSpec((tm, tn), lambda i, j: (i, j)),
            pl.BlockSpec((tm, LANES), lambda i, j: (i, 0)),
        ],
        out_shape=[
            jax.ShapeDtypeStruct((T, N), BF16),
            jax.ShapeDtypeStruct((T, LANES), F32),
        ],
        scratch_shapes=[pltpu.VMEM((tm, D), BF16)],
        compiler_params=_cparams(("arbitrary", "arbitrary")),
        name="in_proj",
    )(x2d, w_main, w_dt)


def _attn_kernel(slopes_ref, q_ref, k_ref, vt_ref, lamp_ref, nw_ref, o_ref,
                 rtab_ref, dtab_ref, acc_ref, *, lambda_init):
    h = pl.program_id(1)
    qi = pl.program_id(2)
    tq = q_ref.shape[0]
    tk = tq
    slope2 = slopes_ref[h] * LOG2E

    @pl.when(qi == 0)
    def _():
        s_rel = lax.broadcasted_iota(jnp.int32, (tk, tq), 0)
        t_rel = lax.broadcasted_iota(jnp.int32, (tk, tq), 1)
        rtab_ref[...] = slope2 * s_rel.astype(F32)
        allowed = (s_rel // CHUNK) <= (t_rel // CHUNK)
        val = slope2 * (t_rel - jnp.abs(t_rel - s_rel)).astype(F32)
        dtab_ref[...] = jnp.where(allowed, val, -jnp.inf)

    q = q_ref[...]
    lane = lax.broadcasted_iota(jnp.int32, q.shape, 1)
    qs = (q.astype(F32) * (ATT_HEAD_DIM ** -0.5 * LOG2E)).astype(BF16)
    zero = jnp.zeros_like(qs)
    qm = (jnp.where(lane < ATT_HEAD_DIM, qs, zero), jnp.where(lane >= ATT_HEAD_DIM, qs, zero))
    nt = (((1,), (1,)), ((), ()))
    acc_ref[...] = jnp.zeros_like(acc_ref)

    def tile(j, tab_ref, m_old):
        kb = k_ref[pl.ds(pl.multiple_of(j * tk, tk), tk), :]
        vb = vt_ref[j]
        cj = -slope2 * ((qi - j) * tk).astype(F32)
        m_out = []
        for m in range(2):
            st = lax.dot_general(kb, qm[m], nt, preferred_element_type=F32) + tab_ref[...]
            m_new = jnp.maximum(m_old[m], jnp.max(st, axis=0, keepdims=True) + cj)
            alpha = jnp.exp2(m_old[m] - m_new)
            p = jnp.exp2(st - (m_new - cj)).astype(BF16)
            acc_ref[m] = alpha * acc_ref[m] + jnp.dot(vb, p, preferred_element_type=F32)
            m_out.append(m_new)
        return tuple(m_out)

    def pair_body(jj, m_old):
        return tile(2 * jj + 1, rtab_ref, tile(2 * jj, rtab_ref, m_old))

    m_init = jnp.full((1, tq), -jnp.inf, F32)
    m_run = lax.fori_loop(0, lax.shift_right_logical(qi, 1), pair_body, (m_init, m_init))
    odd = (qi & 1) == 1

    @pl.when(odd)
    def _():
        tile(qi, dtab_ref, tile(qi - 1, rtab_ref, m_run))

    @pl.when(jnp.logical_not(odd))
    def _():
        tile(qi, dtab_ref, m_run)

    lamp = lamp_ref[...]
    lam = (jnp.exp(jnp.sum(lamp[0:1] * lamp[1:2], axis=-1, keepdims=True))
           - jnp.exp(jnp.sum(lamp[2:3] * lamp[3:4], axis=-1, keepdims=True)) + lambda_init)
    a0 = acc_ref[0]
    a1 = acc_ref[1]
    dv = ATT_V_DIM
    o = a0[:dv] / a0[dv:dv + 1] - lam * (a1[:dv] / a1[dv:dv + 1])
    ms2 = jnp.mean(o * o, axis=0, keepdims=True)
    o = o * lax.rsqrt(ms2 + RMS_EPS) * nw_ref[...] * (1.0 - lambda_init)
    o_ref[...] = o.T.astype(o_ref.dtype)


def _v_transposed(proj, B, S):
    nq = S // ATT_TQ
    v = proj[:, N_Q + N_K:N_Q + N_K + N_V].reshape(B, nq, ATT_TQ, ATT_HEADS, ATT_V_DIM)
    vt = jnp.transpose(v, (0, 3, 1, 4, 2))
    extra = ATT_VT_ROWS - ATT_V_DIM
    ones_rows = (jnp.arange(extra) == 0).astype(BF16)[:, None]
    return jnp.concatenate([vt, jnp.broadcast_to(ones_rows, (B, ATT_HEADS, nq, extra, ATT_TQ))], axis=3)


def _diff_attention(proj, vt, slopes, lamp, nw_col, B, S, lambda_init):
    T = B * S
    tq = ATT_TQ
    nq = S // tq
    kern = functools.partial(_attn_kernel, lambda_init=lambda_init)
    grid_spec = pltpu.PrefetchScalarGridSpec(
        num_scalar_prefetch=1,
        grid=(B, ATT_HEADS, nq),
        in_specs=[
            pl.BlockSpec((tq, LANES), lambda b, h, i, s: (b * nq + i, h)),
            pl.BlockSpec((S, LANES), lambda b, h, i, s: (b, N_Q // LANES + h)),
            pl.BlockSpec((None, None, nq, ATT_VT_ROWS, tq), lambda b, h, i, s: (b, h, 0, 0, 0)),
            pl.BlockSpec((8, LANES), lambda b, h, i, s: (0, 0)),
            pl.BlockSpec((ATT_V_DIM, 1), lambda b, h, i, s: (0, 0)),
        ],
        out_specs=pl.BlockSpec((tq, ATT_V_DIM), lambda b, h, i, s: (b * nq + i, h)),
        scratch_shapes=[
            pltpu.VMEM((tq, tq), F32),
            pltpu.VMEM((tq, tq), F32),
            pltpu.VMEM((2, ATT_VT_ROWS, tq), F32),
        ],
    )
    return pl.pallas_call(
        kern,
        grid_spec=grid_spec,
        out_shape=jax.ShapeDtypeStruct((T, ATT_WIDTH), BF16),
        compiler_params=_cparams(("arbitrary", "arbitrary", "arbitrary")),
        name="diff_attention",
    )(slopes, proj, proj, vt, lamp, nw_col)


def _expand_heads(v, e):
    hi = v.astype(BF16)
    lo = (v - hi.astype(F32)).astype(BF16)
    return jnp.dot(hi, e, preferred_element_type=F32) + jnp.dot(lo, e, preferred_element_type=F32)


def _ssd_kernel(z_ref, xs_ref, b_ref, c_ref, dt_ref, cw_ref, cb_ref, dtb_ref, alog_ref,
                dskip_ref, nw_ref, e_ref, o_ref, ext_ref, st_ref, y_ref):
    blk = pl.program_id(1)
    L = z_ref.shape[0]
    nchunk = L // CHUNK
    gw = SSM_WIDTH // SSM_GROUPS
    hpg = SSM_HEADS // SSM_GROUPS

    @pl.when(blk == 0)
    def _():
        ext_ref[0:8, :] = jnp.zeros((8, N_XBC), F32)
        st_ref[...] = jnp.zeros_like(st_ref)

    cur = jnp.concatenate([xs_ref[...], b_ref[...], c_ref[...]], axis=1).astype(F32)
    ext_ref[8:, :] = cur
    cw = cw_ref[...]
    conv = cb_ref[...] + cw[3:4] * cur
    for j in range(SSM_CONV - 1):
        conv = conv + cw[j:j + 1] * ext_ref[pl.ds(8 - (SSM_CONV - 1) + j, L), :]
    ext_ref[0:8, :] = cur[L - 8:, :]
    xbc = conv * _sigmoid(conv)
    xs = xbc[:, :SSM_WIDTH]
    bmb = xbc[:, SSM_WIDTH:SSM_WIDTH + SSM_GROUPS * SSM_STATE].astype(BF16)
    cmb = xbc[:, SSM_WIDTH + SSM_GROUPS * SSM_STATE:].astype(BF16)
    xsb = xs.astype(BF16)

    lane1 = lax.broadcasted_iota(jnp.int32, (1, LANES), 1)
    dtx = dt_ref[...] + dtb_ref[...]
    dtp = jnp.maximum(dtx, 0.0) + jnp.log1p(jnp.exp(-jnp.abs(dtx)))
    a_head = jnp.where(lane1 < SSM_HEADS, -jnp.exp(alog_ref[...]), 0.0)
    acs = dtp * a_head
    row_in_chunk = lax.broadcasted_iota(jnp.int32, (L, LANES), 0) & (CHUNK - 1)
    k = 1
    while k < CHUNK:
        acs = acs + jnp.where(row_in_chunk >= k, pltpu.roll(acs, k, axis=0), 0.0)
        k *= 2
    acs_t = acs.T
    dt_t = dtp.T

    e = e_ref[...]
    acs_last = jnp.concatenate(
        [jnp.broadcast_to(acs[c * CHUNK + CHUNK - 1:(c + 1) * CHUNK, :], (CHUNK, LANES)) for c in range(nchunk)],
        axis=0)
    w_exp = _expand_heads(dtp * jnp.exp(acs_last - acs), e)
    od_exp = _expand_heads(jnp.exp(acs), e)
    row8 = lax.broadcasted_iota(jnp.int32, (8, LANES), 0)
    cd8 = jnp.zeros((8, LANES), F32)
    for c in range(nchunk):
        cd8 = jnp.where(row8 == c, jnp.exp(acs[c * CHUNK + CHUNK - 1:(c + 1) * CHUNK, :]), cd8)
    cd_exp = _expand_heads(cd8, e)
    xw = (xs * w_exp).astype(BF16)

    tn = (((0,), (0,)), ((), ()))
    for c in range(nchunk):
        r0 = c * CHUNK
        for g in range(SSM_GROUPS):
            st = st_ref[g]
            cg = cmb[r0:r0 + CHUNK, g * SSM_STATE:(g + 1) * SSM_STATE]
            bg = bmb[r0:r0 + CHUNK, g * SSM_STATE:(g + 1) * SSM_STATE]
            y_ref[r0:r0 + CHUNK, g * gw:(g + 1) * gw] = jnp.dot(cg, st.astype(BF16), preferred_element_type=F32)
            snew = lax.dot_general(bg, xw[r0:r0 + CHUNK, g * gw:(g + 1) * gw], tn, preferred_element_type=F32)
            st_ref[g] = st * cd_exp[c:c + 1, g * gw:(g + 1) * gw] + snew
    y = y_ref[...] * od_exp + xs * dskip_ref[...]

    pair = 2 * CHUNK
    li = lax.broadcasted_iota(jnp.int32, (pair, pair), 0)
    si = lax.broadcasted_iota(jnp.int32, (pair, pair), 1)
    mask2 = (li >= si) & ((si >= CHUNK) | (li < CHUNK))
    lanep = lax.broadcasted_iota(jnp.int32, (pair, LANES), 1)
    nt = (((1,), (1,)), ((), ()))
    for pp in range(L // pair):
        r0 = pp * pair
        acs_p = acs[r0:r0 + pair, :]
        for g in range(SSM_GROUPS):
            cb2 = lax.dot_general(cmb[r0:r0 + pair, g * SSM_STATE:(g + 1) * SSM_STATE],
                                  bmb[r0:r0 + pair, g * SSM_STATE:(g + 1) * SSM_STATE],
                                  nt, preferred_element_type=F32)
            for hh in range(hpg // 2):
                hp = g * (hpg // 2) + hh
                mats = []
                for u in range(2):
                    hd = 2 * hp + u
                    seg = _lane_col(acs_p, hd) - acs_t[hd:hd + 1, r0:r0 + pair]
                    decay = jnp.exp(jnp.where(mask2, seg, -jnp.inf))
                    mats.append((cb2 * decay * dt_t[hd:hd + 1, r0:r0 + pair]).astype(BF16))
                lhs = jnp.concatenate(mats, axis=1)
                xp = xsb[r0:r0 + pair, hp * LANES:(hp + 1) * LANES]
                zero = jnp.zeros_like(xp)
                rhs = jnp.concatenate([jnp.where(lanep < SSM_HEAD_DIM, xp, zero),
                                       jnp.where(lanep >= SSM_HEAD_DIM, xp, zero)], axis=0)
                y_ref[r0:r0 + pair, hp * LANES:(hp + 1) * LANES] = jnp.dot(lhs, rhs, preferred_element_type=F32)
    y = y + y_ref[...]

    z = z_ref[...].astype(F32)
    y = y * (z * _sigmoid(z))
    outs = []
    for g in range(SSM_GROUPS):
        yg = y[:, g * gw:(g + 1) * gw]
        outs.append(yg * lax.rsqrt(jnp.mean(yg * yg, axis=-1, keepdims=True) + RMS_EPS))
    o_ref[...] = (jnp.concatenate(outs, axis=1) * nw_ref[...]).astype(o_ref.dtype)


def _ssd_mixer(proj, dt_raw, cw8, cb, dtb, alog, dskip_exp, nw, e_mat, B, S):
    T = B * S
    L = SSD_L
    nb = S // L
    row = lambda b, i: b * nb + i
    col0 = (N_Q + N_K + N_V + N_Z)
    const = lambda shape: pl.BlockSpec(shape, lambda b, i: (0, 0))
    return pl.pallas_call(
        _ssd_kernel,
        grid=(B, nb),
        in_specs=[
            pl.BlockSpec((L, N_Z), lambda b, i: (row(b, i), (N_Q + N_K + N_V) // N_Z)),
            pl.BlockSpec((L, SSM_WIDTH), lambda b, i: (row(b, i), col0 // SSM_WIDTH)),
            pl.BlockSpec((L, 256), lambda b, i: (row(b, i), (col0 + SSM_WIDTH) // 256)),
            pl.BlockSpec((L, 256), lambda b, i: (row(b, i), (col0 + SSM_WIDTH + 256) // 256)),
            pl.BlockSpec((L, LANES), lambda b, i: (row(b, i), 0)),
            const((8, N_XBC)),
            const((1, N_XBC)),
            const((1, LANES)),
            const((1, LANES)),
            const((1, SSM_WIDTH)),
            const((1, SSM_WIDTH)),
            const((LANES, SSM_WIDTH)),
        ],
        out_specs=pl.BlockSpec((L, SSM_WIDTH), lambda b, i: (row(b, i), 0)),
        out_shape=jax.ShapeDtypeStruct((T, SSM_WIDTH), BF16),
        scratch_shapes=[
            pltpu.VMEM((L + 8, N_XBC), F32),
            pltpu.VMEM((SSM_GROUPS, SSM_STATE, SSM_WIDTH // SSM_GROUPS), F32),
            pltpu.VMEM((L, SSM_WIDTH), F32),
        ],
        compiler_params=_cparams(("arbitrary", "arbitrary")),
        name="ssd_mixer",
    )(proj, proj, proj, proj, dt_raw, cw8, cb, dtb, alog, dskip_exp, nw, e_mat)


def _layer_norm_rows(r, g, b):
    mu = jnp.mean(r, axis=-1, keepdims=True)
    d = r - mu
    var = jnp.mean(d * d, axis=-1, keepdims=True)
    return d * lax.rsqrt(var + LN_EPS) * g + b


def _outproj_kernel(att_ref, ssm_ref, x_ref, wa_ref, ws_ref, g_ref, b_ref, wr_ref, br_ref,
                    h_ref, eid_ref, ew_ref):
    mix = (jnp.dot(att_ref[...], wa_ref[...], preferred_element_type=F32)
           + jnp.dot(ssm_ref[...], ws_ref[...], preferred_element_type=F32))
    h = _layer_norm_rows(DN_ALPHA * x_ref[...] + mix, g_ref[...], b_ref[...])
    h_ref[...] = h

    logits = jnp.dot(h.astype(BF16), wr_ref[...], preferred_element_type=F32) + br_ref[...]
    lane = lax.broadcasted_iota(jnp.int32, logits.shape, 1)
    lanef = lane.astype(F32)
    big = float(LANES)
    gl = jnp.where(lane < N_EXPERT_GROUPS, logits, -jnp.inf)
    gmax = jnp.max(gl, axis=-1, keepdims=True)
    g_prob = 1.0 / jnp.sum(jnp.exp(gl - gmax), axis=-1, keepdims=True)
    gidx = jnp.min(jnp.where(gl == gmax, lanef, big), axis=-1, keepdims=True)
    lo = N_EXPERT_GROUPS + EXPERTS_PER_GROUP * gidx
    el = jnp.where((lanef >= lo) & (lanef < lo + EXPERTS_PER_GROUP), logits, -jnp.inf)
    t1 = jnp.max(el, axis=-1, keepdims=True)
    i1 = jnp.min(jnp.where(el == t1, lanef, big), axis=-1, keepdims=True)
    el2 = jnp.where(lanef == i1, -jnp.inf, el)
    t2 = jnp.max(el2, axis=-1, keepdims=True)
    i2 = jnp.min(jnp.where(el2 == t2, lanef, big), axis=-1, keepdims=True)
    ex = jnp.exp(t2 - t1)
    w1 = g_prob / (1.0 + ex)
    w2 = g_prob * ex / (1.0 + ex)
    eid = jnp.where(lane == 0, i1 - N_EXPERT_GROUPS, jnp.where(lane == 1, i2 - N_EXPERT_GROUPS, 0.0))
    eid_ref[...] = eid.astype(jnp.int32)
    ew_ref[...] = jnp.where(lane == 0, w1, jnp.where(lane == 1, w2, 0.0))


def _out_proj(att, ssm, x2d, wa, ws, g, b, wr, br, tm):
    T, D = x2d.shape
    const = lambda shape: pl.BlockSpec(shape, lambda i: (0, 0))
    rows = lambda w: pl.BlockSpec((tm, w), lambda i: (i, 0))
    return pl.pallas_call(
        _outproj_kernel,
        grid=(T // tm,),
        in_specs=[rows(ATT_WIDTH), rows(SSM_WIDTH), rows(D), const((ATT_WIDTH, D)), const((SSM_WIDTH, D)),
                  const((1, D)), const((1, D)), const((D, LANES)), const((1, LANES))],
        out_specs=[rows(D), rows(LANES), rows(LANES)],
        out_shape=[jax.ShapeDtypeStruct((T, D), F32),
                   jax.ShapeDtypeStruct((T, LANES), jnp.int32),
                   jax.ShapeDtypeStruct((T, LANES), F32)],
        compiler_params=_cparams(("arbitrary",)),
        name="out_proj_ln1_router",
    )(att, ssm, x2d, wa, ws, g, b, wr, br)


def _pos_kernel(eid_ref, stril_ref, dest_ref, pend_ref, tot_ref, run_ref, pstart_ref):
    ph = pl.program_id(0)
    i = pl.program_id(1)
    tb = eid_ref.shape[0]
    lane = lax.broadcasted_iota(jnp.int32, (tb, LANES), 1)
    lanef = lane.astype(F32)
    ef = eid_ref[...].astype(F32)
    oh1 = (lanef == _lane_col(ef, 0)).astype(F32)
    oh2 = (lanef == _lane_col(ef, 1)).astype(F32)
    cnt = oh1 + oh2

    @pl.when((ph == 0) & (i == 0))
    def _():
        tot_ref[...] = jnp.zeros_like(tot_ref)

    @pl.when(ph == 0)
    def _():
        tot_ref[...] += jnp.sum(cnt, axis=0, keepdims=True)

    @pl.when((ph == 1) & (i == 0))
    def _():
        tot = jnp.broadcast_to(tot_ref[...], (8, LANES))
        padded = jnp.floor((tot + (MOE_BM - 1)) * (1.0 / MOE_BM)) * MOE_BM
        lane8 = lax.broadcasted_iota(jnp.int32, (8, LANES), 1)
        ends = padded
        k = 1
        while k < LANES:
            ends = ends + jnp.where(lane8 >= k, pltpu.roll(ends, k, axis=1), 0.0)
            k *= 2
        pend_ref[...] = ends
        pstart_ref[...] = (ends - padded)[0:1]
        run_ref[...] = jnp.zeros_like(run_ref)

    @pl.when(ph == 1)
    def _():
        pre = jnp.dot(stril_ref[...], cnt.astype(BF16), preferred_element_type=F32)
        slot = pstart_ref[...] + run_ref[...] + pre
        d1 = jnp.sum(oh1 * slot, axis=-1, keepdims=True)
        d2 = jnp.sum(oh2 * slot, axis=-1, keepdims=True)
        dest_ref[...] = jnp.where(lane == 0, d1, jnp.where(lane == 1, d2, 0.0)).astype(jnp.int32)
        run_ref[...] += jnp.sum(cnt, axis=0, keepdims=True)


def _positions(eid, stril):
    T = eid.shape[0]
    tb = POS_TB
    return pl.pallas_call(
        _pos_kernel,
        grid=(2, T // tb),
        in_specs=[pl.BlockSpec((tb, LANES), lambda p, i: (i, 0)),
                  pl.BlockSpec((tb, tb), lambda p, i: (0, 0))],
        out_specs=[pl.BlockSpec((tb, LANES), lambda p, i: (i * p, 0)),
                   pl.BlockSpec((8, LANES), lambda p, i: (0, 0))],
        out_shape=[jax.ShapeDtypeStruct((T, LANES), jnp.int32),
                   jax.ShapeDtypeStruct((8, LANES), F32)],
        scratch_shapes=[pltpu.VMEM((1, LANES), F32), pltpu.VMEM((1, LANES), F32), pltpu.VMEM((1, LANES), F32)],
        compiler_params=_cparams(("arbitrary", "arbitrary")),
        name="dispatch_positions",
    )(eid, stril)


def _row_copy(src, s, dst, d, sem):
    return pltpu.make_async_copy(src.at[pl.ds(s, 1)], dst.at[pl.ds(d, 1)], sem)


def _scatter_kernel(dest_ref, h_ref, init_hbm, xs_hbm, sem):
    del init_hbm
    tb = h_ref.shape[0]

    def issue(r, c):
        _row_copy(h_ref, r, xs_hbm, dest_ref[0, 0, 2 * r], sem).start()
        _row_copy(h_ref, r, xs_hbm, dest_ref[0, 0, 2 * r + 1], sem).start()
        return c

    def drain(r, c):
        _row_copy(h_ref, 0, xs_hbm, 0, sem).wait()
        _row_copy(h_ref, 0, xs_hbm, 0, sem).wait()
        return c

    lax.fori_loop(0, tb, issue, 0, unroll=8)
    lax.fori_loop(0, tb, drain, 0, unroll=8)


def _dispatch(dest3, h, cap):
    T, D = h.shape
    nb = dest3.shape[0]
    tb = dest3.shape[2] // 2
    init = jnp.zeros((cap, D), h.dtype)
    return pl.pallas_call(
        _scatter_kernel,
        grid=(nb,),
        in_specs=[pl.BlockSpec((1, 1, 2 * tb), lambda i: (i, 0, 0), memory_space=pltpu.SMEM),
                  pl.BlockSpec((tb, D), lambda i: (i, 0)),
                  pl.BlockSpec(memory_space=pl.ANY)],
        out_specs=pl.BlockSpec(memory_space=pl.ANY),
        out_shape=jax.ShapeDtypeStruct((cap, D), h.dtype),
        scratch_shapes=[pltpu.SemaphoreType.DMA],
        input_output_aliases={2: 0},
        compiler_params=_cparams(("arbitrary",)),
        name="moe_dispatch",
    )(dest3, h, init)


def _expert_kernel(bexp_ref, nused_ref, x_ref, wg_ref, wu_ref, wd_ref, y_ref):
    i = pl.program_id(0)

    @pl.when(i < nused_ref[0])
    def _():
        x = x_ref[...].astype(BF16)
        gate = jnp.dot(x, wg_ref[0], preferred_element_type=F32)
        up = jnp.dot(x, wu_ref[0], preferred_element_type=F32)
        hid = (gate * _sigmoid(gate) * up).astype(BF16)
        y_ref[...] = jnp.dot(hid, wd_ref[0], preferred_element_type=F32)

    @pl.when(i >= nused_ref[0])
    def _():
        y_ref[...] = jnp.zeros_like(y_ref)


def _expert_mlp(bexp, nused, xs, wg, wu, wd):
    cap, D = xs.shape
    bm = MOE_BM
    H = wg.shape[2]
    grid_spec = pltpu.PrefetchScalarGridSpec(
        num_scalar_prefetch=2,
        grid=(cap // bm,),
        in_specs=[
            pl.BlockSpec((bm, D), lambda i, be, nu: (i, 0)),
            pl.BlockSpec((1, D, H), lambda i, be, nu: (be[i], 0, 0)),
            pl.BlockSpec((1, D, H), lambda i, be, nu: (be[i], 0, 0)),
            pl.BlockSpec((1, H, D), lambda i, be, nu: (be[i], 0, 0)),
        ],
        out_specs=pl.BlockSpec((bm, D), lambda i, be, nu: (i, 0)),
    )
    return pl.pallas_call(
        _expert_kernel,
        grid_spec=grid_spec,
        out_shape=jax.ShapeDtypeStruct((cap, D), F32),
        compiler_params=_cparams(("arbitrary",)),
        name="expert_mlp",
    )(bexp, nused, xs, wg, wu, wd)


def _combine_kernel(dest_ref, h_ref, ew_ref, g_ref, b_ref, y_hbm, o_ref, ybuf, sem):
    tb = h_ref.shape[0]

    def issue(r, c):
        _row_copy(y_hbm, dest_ref[0, 0, 2 * r], ybuf.at[0], r, sem).start()
        _row_copy(y_hbm, dest_ref[0, 0, 2 * r + 1], ybuf.at[1], r, sem).start()
        return c

    def drain(r, c):
        _row_copy(y_hbm, 0, ybuf.at[0], 0, sem).wait()
        _row_copy(y_hbm, 0, ybuf.at[1], 0, sem).wait()
        return c

    lax.fori_loop(0, tb, issue, 0, unroll=8)
    lax.fori_loop(0, tb, drain, 0, unroll=8)
    ew = ew_ref[...]
    ffn = _lane_col(ew, 0) * ybuf[0] + _lane_col(ew, 1) * ybuf[1]
    o_ref[...] = _layer_norm_rows(DN_ALPHA * h_ref[...] + ffn, g_ref[...], b_ref[...])


def _combine(dest3, h, ew, g, b, y):
    T, D = h.shape
    tb = dest3.shape[2] // 2
    return pl.pallas_call(
        _combine_kernel,
        grid=(T // tb,),
        in_specs=[pl.BlockSpec((1, 1, 2 * tb), lambda i: (i, 0, 0), memory_space=pltpu.SMEM),
                  pl.BlockSpec((tb, D), lambda i: (i, 0)),
                  pl.BlockSpec((tb, LANES), lambda i: (i, 0)),
                  pl.BlockSpec((1, D), lambda i: (0, 0)),
                  pl.BlockSpec((1, D), lambda i: (0, 0)),
                  pl.BlockSpec(memory_space=pl.ANY)],
        out_specs=pl.BlockSpec((tb, D), lambda i: (i, 0)),
        out_shape=jax.ShapeDtypeStruct((T, D), F32),
        scratch_shapes=[pltpu.VMEM((2, tb, D), F32), pltpu.SemaphoreType.DMA],
        compiler_params=_cparams(("arbitrary",)),
        name="moe_combine_ln2",
    )(dest3, h, ew, g, b, y)


def _pad_lanes(v, n=LANES):
    v = v.reshape(1, -1).astype(F32)
    return jnp.pad(v, ((0, 0), (0, n - v.shape[1])))


def kernel(x, w_in, lambda_q1, lambda_k1, lambda_q2, lambda_k2, attn_norm_w, conv_w, conv_b, dt_bias, a_log, d_skip, ssm_norm_w, w_out, ln1_g, ln1_b, w_router_group, b_router_group, w_router_expert, b_router_expert, w_gate, w_up, w_down, ln2_g, ln2_b):
    B, S, D = x.shape
    T = B * S
    assert w_in.shape[0] == DEPTH == 1
    assert S % ATT_TQ == 0 and S % SSD_L == 0 and T % POS_TB == 0 and T % TOK_TB == 0
    l = 0
    lambda_init = 0.8 - 0.6 * math.exp(-0.3 * l)
    x2d = x.reshape(T, D)

    w_main = w_in[l][:, :N_MAIN].astype(BF16)
    w_dt = jnp.pad(w_in[l][:, N_MAIN:], ((0, 0), (0, LANES - N_DT))).astype(BF16)
    slopes = jnp.exp2(-8.0 * jnp.arange(1, ATT_HEADS + 1, dtype=F32) / ATT_HEADS)
    lamp = jnp.concatenate([_pad_lanes(lambda_q1[l]), _pad_lanes(lambda_k1[l]),
                            _pad_lanes(lambda_q2[l]), _pad_lanes(lambda_k2[l]),
                            jnp.zeros((4, LANES), F32)], axis=0)
    nw_col = attn_norm_w[l].astype(F32).reshape(ATT_V_DIM, 1)
    cw8 = jnp.pad(conv_w[l].astype(F32), ((0, 8 - SSM_CONV), (0, 0)))
    cb = conv_b[l].astype(F32).reshape(1, N_XBC)
    dskip_exp = jnp.repeat(d_skip[l].astype(F32), SSM_HEAD_DIM).reshape(1, SSM_WIDTH)
    ssm_nw = ssm_norm_w[l].astype(F32).reshape(1, SSM_WIDTH)
    head_of_lane = jnp.arange(SSM_WIDTH, dtype=jnp.int32) // SSM_HEAD_DIM
    e_mat = (jnp.arange(LANES, dtype=jnp.int32)[:, None] == head_of_lane[None, :]).astype(BF16)
    wa = w_out[l][:ATT_WIDTH].astype(BF16)
    ws = w_out[l][ATT_WIDTH:].astype(BF16)
    wr = jnp.concatenate(
        [w_router_group[l], jnp.transpose(w_router_expert[l], (1, 0, 2)).reshape(D, N_EXPERTS)], axis=1)
    wr = jnp.pad(wr, ((0, 0), (0, LANES - wr.shape[1]))).astype(BF16)
    br = _pad_lanes(jnp.concatenate([b_router_group[l], b_router_expert[l].reshape(-1)]))
    row = lambda v: v.astype(F32).reshape(1, D)

    tm_in = 1024 if T % 1024 == 0 else 256
    proj, dt_raw = _in_proj(x2d, w_main, w_dt, tm_in, 512)
    vt = _v_transposed(proj, B, S)
    att = _diff_attention(proj, vt, slopes, lamp, nw_col, B, S, lambda_init)
    ssm = _ssd_mixer(proj, dt_raw, cw8, cb, _pad_lanes(dt_bias[l]), _pad_lanes(a_log[l]),
                     dskip_exp, ssm_nw, e_mat, B, S)
    h1, eid, ew = _out_proj(att, ssm, x2d, wa, ws, row(ln1_g[l]), row(ln1_b[l]), wr, br, 256)

    stril = (jnp.arange(POS_TB)[:, None] > jnp.arange(POS_TB)[None, :]).astype(BF16)
    dest, pend = _positions(eid, stril)
    nblk = (T * 2) // MOE_BM + N_EXPERTS
    cap = nblk * MOE_BM
    pad_ends = pend[0, :N_EXPERTS].astype(jnp.int32)
    blk_start = jnp.arange(nblk, dtype=jnp.int32) * MOE_BM
    nused = (pad_ends[N_EXPERTS - 1] // MOE_BM).astype(jnp.int32)
    last_used = jnp.maximum(nused - 1, 0) * MOE_BM
    bexp = jnp.sum(pad_ends[None, :] <= jnp.minimum(blk_start, last_used)[:, None], axis=1).astype(jnp.int32)
    bexp = jnp.minimum(bexp, N_EXPERTS - 1)
    dest3 = dest[:, :2].reshape(T // TOK_TB, 1, 2 * TOK_TB)
    xs_sorted = _dispatch(dest3, h1, cap)
    y_sorted = _expert_mlp(bexp, nused.reshape(1), xs_sorted,
                           w_gate[l].astype(BF16), w_up[l].astype(BF16), w_down[l].astype(BF16))
    out = _combine(dest3, h1, ew, row(ln2_g[l]), row(ln2_b[l]), y_sorted)
    return out.reshape(B, S, D)
```

```python
import functools
import math

import jax
import jax.numpy as jnp
from jax import lax
from jax.experimental import pallas as pl
from jax.experimental.pallas import tpu as pltpu

F32 = jnp.float32
BF16 = jnp.bfloat16

CHUNK = 64
ATT_HEADS = 8
ATT_HEAD_DIM = 64
ATT_V_DIM = 128
ATT_WIDTH = ATT_HEADS * ATT_V_DIM
ATT_VT_ROWS = ATT_V_DIM + 16
LOG2E = 1.4426950408889634
SSM_HEADS = 16
SSM_HEAD_DIM = 64
SSM_WIDTH = SSM_HEADS * SSM_HEAD_DIM
SSM_GROUPS = 2
SSM_STATE = 128
SSM_CONV = 4
N_Q = 1024
N_K = 1024
N_V = 1024
N_Z = 1024
N_XBC = SSM_WIDTH + 2 * SSM_GROUPS * SSM_STATE
N_DT = SSM_HEADS
N_MAIN = N_Q + N_K + N_V + N_Z + N_XBC
N_EXPERT_GROUPS = 4
EXPERTS_PER_GROUP = 8
N_EXPERTS = 32
EXPERT_HIDDEN = 1024
DEPTH = 1
DN_ALPHA = (2 * DEPTH) ** 0.25
LN_EPS = 1e-5
RMS_EPS = 1e-6
LANES = 128

VMEM_LIMIT = 56 * 1024 * 1024

ATT_TQ = 512
SSD_L = 256
MOE_BM = 256
TOK_TB = 256
POS_TB = 512


def _cparams(sem, flags=None):
    return pltpu.CompilerParams(dimension_semantics=sem, vmem_limit_bytes=VMEM_LIMIT, flags=flags)


def _sigmoid(x):
    return 1.0 / (1.0 + jnp.exp(-x))


def _lane_col(x, idx):
    lane = lax.broadcasted_iota(jnp.int32, x.shape, 1)
    return jnp.sum(jnp.where(lane == idx, x, 0.0), axis=-1, keepdims=True)


def _inproj_kernel(x_ref, w_ref, wdt_ref, o_ref, dt_ref, xb_ref):
    @pl.when(pl.program_id(1) == 0)
    def _():
        xb = x_ref[...].astype(BF16)
        xb_ref[...] = xb
        dt_ref[...] = jnp.dot(xb, wdt_ref[...], preferred_element_type=F32)

    o_ref[...] = jnp.dot(xb_ref[...], w_ref[...], preferred_element_type=F32).astype(o_ref.dtype)


def _in_proj(x2d, w_main, w_dt, tm, tn):
    T, D = x2d.shape
    N = w_main.shape[1]
    return pl.pallas_call(
        _inproj_kernel,
        grid=(T // tm, N // tn),
        in_specs=[
            pl.BlockSpec((tm, D), lambda i, j: (i, 0)),
            pl.BlockSpec((D, tn), lambda i, j: (0, j)),
            pl.BlockSpec((D, LANES), lambda i, j: (0, 0)),
        ],
        out_specs=[
            pl.BlockSpec((tm, tn), lambda i, j: (i, j)),
            pl.BlockSpec((tm, LANES), lambda i, j: (i, 0)),
        ],
        out_shape=[
            jax.ShapeDtypeStruct((T, N), BF16),
            jax.ShapeDtypeStruct((T, LANES), F32),
        ],
        scratch_shapes=[pltpu.VMEM((tm, D), BF16)],
        compiler_params=_cparams(("arbitrary", "arbitrary")),
        name="in_proj",
    )(x2d, w_main, w_dt)


def _attn_kernel(slopes_ref, q_ref, k_ref, vt_ref, lamp_ref, nw_ref, o_ref,
                 tab_ref, acc_ref, sa_ref, sb_ref, pa_ref, pb_ref, mxa_ref, mxb_ref, ala_ref, alb_ref,
                 *, lambda_init):
    h = pl.program_id(1)
    qi = pl.program_id(2)
    tq = q_ref.shape[0]
    tk = tq
    slope2 = slopes_ref[h] * LOG2E

    @pl.when(qi == 0)
    def _():
        s_rel = lax.broadcasted_iota(jnp.int32, (tk, tq), 0)
        t_rel = lax.broadcasted_iota(jnp.int32, (tk, tq), 1)
        tab_ref[0] = slope2 * s_rel.astype(F32)
        allowed = (s_rel // CHUNK) <= (t_rel // CHUNK)
        val = slope2 * (t_rel - jnp.abs(t_rel - s_rel)).astype(F32)
        tab_ref[1] = jnp.where(allowed, val, -jnp.inf)

    q = q_ref[...]
    lane = lax.broadcasted_iota(jnp.int32, q.shape, 1)
    qs = (q.astype(F32) * (ATT_HEAD_DIM ** -0.5 * LOG2E)).astype(BF16)
    zero = jnp.zeros_like(qs)
    qm = (jnp.where(lane < ATT_HEAD_DIM, qs, zero), jnp.where(lane >= ATT_HEAD_DIM, qs, zero))
    nt = (((1,), (1,)), ((), ()))
    def scores(j, s_out, mx_out):
        jc = jnp.minimum(j, qi)
        kb = k_ref[pl.ds(pl.multiple_of(jc * tk, tk), tk), :]
        tab = tab_ref[(jc == qi).astype(jnp.int32)]
        for m in range(2):
            st = lax.dot_general(kb, qm[m], nt, preferred_element_type=F32) + tab
            s_out[m] = st
            mx_out[m] = jnp.max(st, axis=0, keepdims=True)

    def probs(j, s_in, mx_in, p_out, al_out, m_old):
        cj = -slope2 * ((qi - j) * tk).astype(F32)
        m_out = []
        for m in range(2):
            m_new = jnp.maximum(m_old[m], mx_in[m] + cj)
            al_out[m] = jnp.exp2(m_old[m] - m_new)
            p_out[m] = jnp.exp2(s_in[m] - (m_new - cj)).astype(BF16)
            m_out.append(m_new)
        return tuple(m_out)

    def accum(j, p_in, al_in):
        vb = vt_ref[jnp.maximum(j, 0)]
        for m in range(2):
            acc_ref[m] = al_in[m] * acc_ref[m] + jnp.dot(vb, p_in[m], preferred_element_type=F32)

    half = tq // 2
    strip = 16

    def trip(i, cur, nxt, m_old):
        s_c, mx_c, p_c, al_c = cur
        s_n, mx_n, p_n, al_n = nxt
        cj = -slope2 * ((qi - i) * tk).astype(F32)
        refs, m_out = [], []
        for m in range(2):
            m_new = jnp.maximum(m_old[m], mx_c[m] + cj)
            al_c[m] = jnp.exp2(m_old[m] - m_new)
            refs.append(m_new - cj)
            m_out.append(m_new)

        jn = jnp.minimum(i + 1, qi)
        kb = k_ref[pl.ds(pl.multiple_of(jn * tk, tk), tk), :]
        tab_i = (jn == qi).astype(jnp.int32)
        vb = vt_ref[jnp.maximum(i - 1, 0)]

        def accum_piece(m, c):
            cols = slice(c * half, (c + 1) * half)
            acc_ref[m, :, cols] = (al_n[m][:, cols] * acc_ref[m, :, cols]
                                   + jnp.dot(vb, p_n[m, :, cols], preferred_element_type=F32))

        def scores_piece(m, c):
            cols = slice(c * half, (c + 1) * half)
            st = lax.dot_general(kb, qm[m][cols, :], nt, preferred_element_type=F32) + tab_ref[tab_i, :, cols]
            s_n[m, :, cols] = st
            mx_n[m, :, cols] = jnp.max(st, axis=0, keepdims=True)

        pieces = ([functools.partial(accum_piece, m, c) for m in range(2) for c in range(2)]
                  + [functools.partial(scores_piece, m, c) for m in range(2) for c in range(2)])
        strips = [(m, r) for m in range(2) for r in range(tk // strip)]
        per = len(strips) // len(pieces)
        for g, piece in enumerate(pieces):
            piece()
            for m, r in strips[g * per:(g + 1) * per]:
                rows = slice(r * strip, (r + 1) * strip)
                p_c[m, rows, :] = jnp.exp2(s_c[m, rows, :] - refs[m]).astype(BF16)
        return tuple(m_out)

    set_a = (sa_ref, mxa_ref, pa_ref, ala_ref)
    set_b = (sb_ref, mxb_ref, pb_ref, alb_ref)
    acc_ref[...] = jnp.zeros_like(acc_ref)
    pb_ref[...] = jnp.zeros_like(pb_ref)
    alb_ref[...] = jnp.ones_like(alb_ref)
    scores(0, sa_ref, mxa_ref)

    def body(i, m_old):
        return lax.cond((i & 1) == 0,
                        lambda mm: trip(i, set_a, set_b, mm),
                        lambda mm: trip(i, set_b, set_a, mm), m_old)

    m_init = jnp.full((1, tq), -jnp.inf, F32)
    lax.fori_loop(0, qi + 1, body, (m_init, m_init))

    @pl.when((qi & 1) == 0)
    def _():
        accum(qi, pa_ref, ala_ref)

    @pl.when((qi & 1) == 1)
    def _():
        accum(qi, pb_ref, alb_ref)

    lamp = lamp_ref[...]
    lam = (jnp.exp(jnp.sum(lamp[0:1] * lamp[1:2], axis=-1, keepdims=True))
           - jnp.exp(jnp.sum(lamp[2:3] * lamp[3:4], axis=-1, keepdims=True)) + lambda_init)
    a0 = acc_ref[0]
    a1 = acc_ref[1]
    dv = ATT_V_DIM
    o = a0[:dv] / a0[dv:dv + 1] - lam * (a1[:dv] / a1[dv:dv + 1])
    ms2 = jnp.mean(o * o, axis=0, keepdims=True)
    o = o * lax.rsqrt(ms2 + RMS_EPS) * nw_ref[...] * (1.0 - lambda_init)
    o_ref[...] = o.T.astype(o_ref.dtype)


def _v_transposed(proj, B, S):
    nq = S // ATT_TQ
    v = proj[:, N_Q + N_K:N_Q + N_K + N_V].reshape(B, nq, ATT_TQ, ATT_HEADS, ATT_V_DIM)
    vt = jnp.transpose(v, (0, 3, 1, 4, 2))
    extra = ATT_VT_ROWS - ATT_V_DIM
    ones_rows = (jnp.arange(extra) == 0).astype(BF16)[:, None]
    return jnp.concatenate([vt, jnp.broadcast_to(ones_rows, (B, ATT_HEADS, nq, extra, ATT_TQ))], axis=3)


def _diff_attention(proj, vt, slopes, lamp, nw_col, B, S, lambda_init):
    T = B * S
    tq = ATT_TQ
    nq = S // tq
    kern = functools.partial(_attn_kernel, lambda_init=lambda_init)
    grid_spec = pltpu.PrefetchScalarGridSpec(
        num_scalar_prefetch=1,
        grid=(B, ATT_HEADS, nq),
        in_specs=[
            pl.BlockSpec((tq, LANES), lambda b, h, i, s: (b * nq + i, h)),
            pl.BlockSpec((S, LANES), lambda b, h, i, s: (b, N_Q // LANES + h)),
            pl.BlockSpec((None, None, nq, ATT_VT_ROWS, tq), lambda b, h, i, s: (b, h, 0, 0, 0)),
            pl.BlockSpec((8, LANES), lambda b, h, i, s: (0, 0)),
            pl.BlockSpec((ATT_V_DIM, 1), lambda b, h, i, s: (0, 0)),
        ],
        out_specs=pl.BlockSpec((tq, ATT_V_DIM), lambda b, h, i, s: (b * nq + i, h)),
        scratch_shapes=[
            pltpu.VMEM((2, tq, tq), F32),
            pltpu.VMEM((2, ATT_VT_ROWS, tq), F32),
            pltpu.VMEM((2, tq, tq), F32), pltpu.VMEM((2, tq, tq), F32),
            pltpu.VMEM((2, tq, tq), BF16), pltpu.VMEM((2, tq, tq), BF16),
            pltpu.VMEM((2, 1, tq), F32), pltpu.VMEM((2, 1, tq), F32),
            pltpu.VMEM((2, 1, tq), F32), pltpu.VMEM((2, 1, tq), F32),
        ],
    )
    return pl.pallas_call(
        kern,
        grid_spec=grid_spec,
        out_shape=jax.ShapeDtypeStruct((T, ATT_WIDTH), BF16),
        compiler_params=_cparams(("arbitrary", "arbitrary", "arbitrary")),
        name="diff_attention",
    )(slopes, proj, proj, vt, lamp, nw_col)


def _expand_heads(v, e):
    hi = v.astype(BF16)
    lo = (v - hi.astype(F32)).astype(BF16)
    return jnp.dot(hi, e, preferred_element_type=F32) + jnp.dot(lo, e, preferred_element_type=F32)


def _ssd_kernel(z_ref, xs_ref, b_ref, c_ref, dt_ref, cw_ref, cb_ref, dtb_ref, alog_ref,
                dskip_ref, nw_ref, e_ref, o_ref, ext_ref, st_ref, y_ref):
    blk = pl.program_id(1)
    L = z_ref.shape[0]
    nchunk = L // CHUNK
    gw = SSM_WIDTH // SSM_GROUPS
    hpg = SSM_HEADS // SSM_GROUPS

    @pl.when(blk == 0)
    def _():
        ext_ref[0:8, :] = jnp.zeros((8, N_XBC), F32)
        st_ref[...] = jnp.zeros_like(st_ref)

    cur = jnp.concatenate([xs_ref[...], b_ref[...], c_ref[...]], axis=1).astype(F32)
    ext_ref[8:, :] = cur
    cw = cw_ref[...]
    conv = cb_ref[...] + cw[3:4] * cur
    for j in range(SSM_CONV - 1):
        conv = conv + cw[j:j + 1] * ext_ref[pl.ds(8 - (SSM_CONV - 1) + j, L), :]
    ext_ref[0:8, :] = cur[L - 8:, :]
    xbc = conv * _sigmoid(conv)
    xs = xbc[:, :SSM_WIDTH]
    bmb = xbc[:, SSM_WIDTH:SSM_WIDTH + SSM_GROUPS * SSM_STATE].astype(BF16)
    cmb = xbc[:, SSM_WIDTH + SSM_GROUPS * SSM_STATE:].astype(BF16)
    xsb = xs.astype(BF16)

    lane1 = lax.broadcasted_iota(jnp.int32, (1, LANES), 1)
    dtx = dt_ref[...] + dtb_ref[...]
    dtp = jnp.maximum(dtx, 0.0) + jnp.log1p(jnp.exp(-jnp.abs(dtx)))
    a_head = jnp.where(lane1 < SSM_HEADS, -jnp.exp(alog_ref[...]), 0.0)
    acs = dtp * a_head
    row_in_chunk = lax.broadcasted_iota(jnp.int32, (L, LANES), 0) & (CHUNK - 1)
    k = 1
    while k < CHUNK:
        acs = acs + jnp.where(row_in_chunk >= k, pltpu.roll(acs, k, axis=0), 0.0)
        k *= 2
    acs_t = acs.T
    dt_t = dtp.T

    e = e_ref[...]
    acs_last = jnp.concatenate(
        [jnp.broadcast_to(acs[c * CHUNK + CHUNK - 1:(c + 1) * CHUNK, :], (CHUNK, LANES)) for c in range(nchunk)],
        axis=0)
    w_exp = _expand_heads(dtp * jnp.exp(acs_last - acs), e)
    od_exp = _expand_heads(jnp.exp(acs), e)
    row8 = lax.broadcasted_iota(jnp.int32, (8, LANES), 0)
    cd8 = jnp.zeros((8, LANES), F32)
    for c in range(nchunk):
        cd8 = jnp.where(row8 == c, jnp.exp(acs[c * CHUNK + CHUNK - 1:(c + 1) * CHUNK, :]), cd8)
    cd_exp = _expand_heads(cd8, e)
    xw = (xs * w_exp).astype(BF16)

    tn = (((0,), (0,)), ((), ()))
    for c in range(nchunk):
        r0 = c * CHUNK
        for g in range(SSM_GROUPS):
            st = st_ref[g]
            cg = cmb[r0:r0 + CHUNK, g * SSM_STATE:(g + 1) * SSM_STATE]
            bg = bmb[r0:r0 + CHUNK, g * SSM_STATE:(g + 1) * SSM_STATE]
            y_ref[r0:r0 + CHUNK, g * gw:(g + 1) * gw] = jnp.dot(cg, st.astype(BF16), preferred_element_type=F32)
            snew = lax.dot_general(bg, xw[r0:r0 + CHUNK, g * gw:(g + 1) * gw], tn, preferred_element_type=F32)
            st_ref[g] = st * cd_exp[c:c + 1, g * gw:(g + 1) * gw] + snew
    y = y_ref[...] * od_exp + xs * dskip_ref[...]

    pair = 2 * CHUNK
    li = lax.broadcasted_iota(jnp.int32, (pair, pair), 0)
    si = lax.broadcasted_iota(jnp.int32, (pair, pair), 1)
    mask2 = (li >= si) & ((si >= CHUNK) | (li < CHUNK))
    lanep = lax.broadcasted_iota(jnp.int32, (pair, LANES), 1)
    nt = (((1,), (1,)), ((), ()))
    for pp in range(L // pair):
        r0 = pp * pair
        acs_p = acs[r0:r0 + pair, :]
        for g in range(SSM_GROUPS):
            cb2 = lax.dot_general(cmb[r0:r0 + pair, g * SSM_STATE:(g + 1) * SSM_STATE],
                                  bmb[r0:r0 + pair, g * SSM_STATE:(g + 1) * SSM_STATE],
                                  nt, preferred_element_type=F32)
            for hh in range(hpg // 2):
                hp = g * (hpg // 2) + hh
                mats = []
                for u in range(2):
                    hd = 2 * hp + u
                    seg = _lane_col(acs_p, hd) - acs_t[hd:hd + 1, r0:r0 + pair]
                    decay = jnp.exp(jnp.where(mask2, seg, -jnp.inf))
                    mats.append((cb2 * decay * dt_t[hd:hd + 1, r0:r0 + pair]).astype(BF16))
                lhs = jnp.concatenate(mats, axis=1)
                xp = xsb[r0:r0 + pair, hp * LANES:(hp + 1) * LANES]
                zero = jnp.zeros_like(xp)
                rhs = jnp.concatenate([jnp.where(lanep < SSM_HEAD_DIM, xp, zero),
                                       jnp.where(lanep >= SSM_HEAD_DIM, xp, zero)], axis=0)
                y_ref[r0:r0 + pair, hp * LANES:(hp + 1) * LANES] = jnp.dot(lhs, rhs, preferred_element_type=F32)
    y = y + y_ref[...]

    z = z_ref[...].astype(F32)
    y = y * (z * _sigmoid(z))
    outs = []
    for g in range(SSM_GROUPS):
        yg = y[:, g * gw:(g + 1) * gw]
        outs.append(yg * lax.rsqrt(jnp.mean(yg * yg, axis=-1, keepdims=True) + RMS_EPS))
    o_ref[...] = (jnp.concatenate(outs, axis=1) * nw_ref[...]).astype(o_ref.dtype)


def _ssd_mixer(proj, dt_raw, cw8, cb, dtb, alog, dskip_exp, nw, e_mat, B, S):
    T = B * S
    L = SSD_L
    nb = S // L
    row = lambda b, i: b * nb + i
    col0 = (N_Q + N_K + N_V + N_Z)
    const = lambda shape: pl.BlockSpec(shape, lambda b, i: (0, 0))
    return pl.pallas_call(
        _ssd_kernel,
        grid=(B, nb),
        in_specs=[
            pl.BlockSpec((L, N_Z), lambda b, i: (row(b, i), (N_Q + N_K + N_V) // N_Z)),
            pl.BlockSpec((L, SSM_WIDTH), lambda b, i: (row(b, i), col0 // SSM_WIDTH)),
            pl.BlockSpec((L, 256), lambda b, i: (row(b, i), (col0 + SSM_WIDTH) // 256)),
            pl.BlockSpec((L, 256), lambda b, i: (row(b, i), (col0 + SSM_WIDTH + 256) // 256)),
            pl.BlockSpec((L, LANES), lambda b, i: (row(b, i), 0)),
            const((8, N_XBC)),
            const((1, N_XBC)),
            const((1, LANES)),
            const((1, LANES)),
            const((1, SSM_WIDTH)),
            const((1, SSM_WIDTH)),
            const((LANES, SSM_WIDTH)),
        ],
        out_specs=pl.BlockSpec((L, SSM_WIDTH), lambda b, i: (row(b, i), 0)),
        out_shape=jax.ShapeDtypeStruct((T, SSM_WIDTH), BF16),
        scratch_shapes=[
            pltpu.VMEM((L + 8, N_XBC), F32),
            pltpu.VMEM((SSM_GROUPS, SSM_STATE, SSM_WIDTH // SSM_GROUPS), F32),
            pltpu.VMEM((L, SSM_WIDTH), F32),
        ],
        compiler_params=_cparams(("arbitrary", "arbitrary")),
        name="ssd_mixer",
    )(proj, proj, proj, proj, dt_raw, cw8, cb, dtb, alog, dskip_exp, nw, e_mat)


def _layer_norm_rows(r, g, b):
    mu = jnp.mean(r, axis=-1, keepdims=True)
    d = r - mu
    var = jnp.mean(d * d, axis=-1, keepdims=True)
    return d * lax.rsqrt(var + LN_EPS) * g + b


def _outproj_kernel(att_ref, ssm_ref, x_ref, wa_ref, ws_ref, g_ref, b_ref, wr_ref, br_ref,
                    h_ref, eid_ref, ew_ref):
    mix = (jnp.dot(att_ref[...], wa_ref[...], preferred_element_type=F32)
           + jnp.dot(ssm_ref[...], ws_ref[...], preferred_element_type=F32))
    h = _layer_norm_rows(DN_ALPHA * x_ref[...] + mix, g_ref[...], b_ref[...])
    h_ref[...] = h

    logits = jnp.dot(h.astype(BF16), wr_ref[...], preferred_element_type=F32) + br_ref[...]
    lane = lax.broadcasted_iota(jnp.int32, logits.shape, 1)
    lanef = lane.astype(F32)
    big = float(LANES)
    gl = jnp.where(lane < N_EXPERT_GROUPS, logits, -jnp.inf)
    gmax = jnp.max(gl, axis=-1, keepdims=True)
    g_prob = 1.0 / jnp.sum(jnp.exp(gl - gmax), axis=-1, keepdims=True)
    gidx = jnp.min(jnp.where(gl == gmax, lanef, big), axis=-1, keepdims=True)
    lo = N_EXPERT_GROUPS + EXPERTS_PER_GROUP * gidx
    el = jnp.where((lanef >= lo) & (lanef < lo + EXPERTS_PER_GROUP), logits, -jnp.inf)
    t1 = jnp.max(el, axis=-1, keepdims=True)
    i1 = jnp.min(jnp.where(el == t1, lanef, big), axis=-1, keepdims=True)
    el2 = jnp.where(lanef == i1, -jnp.inf, el)
    t2 = jnp.max(el2, axis=-1, keepdims=True)
    i2 = jnp.min(jnp.where(el2 == t2, lanef, big), axis=-1, keepdims=True)
    ex = jnp.exp(t2 - t1)
    w1 = g_prob / (1.0 + ex)
    w2 = g_prob * ex / (1.0 + ex)
    eid = jnp.where(lane == 0, i1 - N_EXPERT_GROUPS, jnp.where(lane == 1, i2 - N_EXPERT_GROUPS, 0.0))
    eid_ref[...] = eid.astype(jnp.int32)
    ew_ref[...] = jnp.where(lane == 0, w1, jnp.where(lane == 1, w2, 0.0))


def _out_proj(att, ssm, x2d, wa, ws, g, b, wr, br, tm):
    T, D = x2d.shape
    const = lambda shape: pl.BlockSpec(shape, lambda i: (0, 0))
    rows = lambda w: pl.BlockSpec((tm, w), lambda i: (i, 0))
    return pl.pallas_call(
        _outproj_kernel,
        grid=(T // tm,),
        in_specs=[rows(ATT_WIDTH), rows(SSM_WIDTH), rows(D), const((ATT_WIDTH, D)), const((SSM_WIDTH, D)),
                  const((1, D)), const((1, D)), const((D, LANES)), const((1, LANES))],
        out_specs=[rows(D), rows(LANES), rows(LANES)],
        out_shape=[jax.ShapeDtypeStruct((T, D), F32),
                   jax.ShapeDtypeStruct((T, LANES), jnp.int32),
                   jax.ShapeDtypeStruct((T, LANES), F32)],
        compiler_params=_cparams(("arbitrary",)),
        name="out_proj_ln1_router",
    )(att, ssm, x2d, wa, ws, g, b, wr, br)


def _pos_kernel(eid_ref, stril_ref, dest_ref, pend_ref, tot_ref, run_ref, pstart_ref):
    ph = pl.program_id(0)
    i = pl.program_id(1)
    tb = eid_ref.shape[0]
    lane = lax.broadcasted_iota(jnp.int32, (tb, LANES), 1)
    lanef = lane.astype(F32)
    ef = eid_ref[...].astype(F32)
    oh1 = (lanef == _lane_col(ef, 0)).astype(F32)
    oh2 = (lanef == _lane_col(ef, 1)).astype(F32)
    cnt = oh1 + oh2

    @pl.when((ph == 0) & (i == 0))
    def _():
        tot_ref[...] = jnp.zeros_like(tot_ref)

    @pl.when(ph == 0)
    def _():
        tot_ref[...] += jnp.sum(cnt, axis=0, keepdims=True)

    @pl.when((ph == 1) & (i == 0))
    def _():
        tot = jnp.broadcast_to(tot_ref[...], (8, LANES))
        padded = jnp.floor((tot + (MOE_BM - 1)) * (1.0 / MOE_BM)) * MOE_BM
        lane8 = lax.broadcasted_iota(jnp.int32, (8, LANES), 1)
        ends = padded
        k = 1
        while k < LANES:
            ends = ends + jnp.where(lane8 >= k, pltpu.roll(ends, k, axis=1), 0.0)
            k *= 2
        pend_ref[...] = ends
        pstart_ref[...] = (ends - padded)[0:1]
        run_ref[...] = jnp.zeros_like(run_ref)

    @pl.when(ph == 1)
    def _():
        pre = jnp.dot(stril_ref[...], cnt.astype(BF16), preferred_element_type=F32)
        slot = pstart_ref[...] + run_ref[...] + pre
        d1 = jnp.sum(oh1 * slot, axis=-1, keepdims=True)
        d2 = jnp.sum(oh2 * slot, axis=-1, keepdims=True)
        dest_ref[...] = jnp.where(lane == 0, d1, jnp.where(lane == 1, d2, 0.0)).astype(jnp.int32)
        run_ref[...] += jnp.sum(cnt, axis=0, keepdims=True)


def _positions(eid, stril):
    T = eid.shape[0]
    tb = POS_TB
    return pl.pallas_call(
        _pos_kernel,
        grid=(2, T // tb),
        in_specs=[pl.BlockSpec((tb, LANES), lambda p, i: (i, 0)),
                  pl.BlockSpec((tb, tb), lambda p, i: (0, 0))],
        out_specs=[pl.BlockSpec((tb, LANES), lambda p, i: (i * p, 0)),
                   pl.BlockSpec((8, LANES), lambda p, i: (0, 0))],
        out_shape=[jax.ShapeDtypeStruct((T, LANES), jnp.int32),
                   jax.ShapeDtypeStruct((8, LANES), F32)],
        scratch_shapes=[pltpu.VMEM((1, LANES), F32), pltpu.VMEM((1, LANES), F32), pltpu.VMEM((1, LANES), F32)],
        compiler_params=_cparams(("arbitrary", "arbitrary")),
        name="dispatch_positions",
    )(eid, stril)


def _row_copy(src, s, dst, d, sem):
    return pltpu.make_async_copy(src.at[pl.ds(s, 1)], dst.at[pl.ds(d, 1)], sem)


def _scatter_kernel(dest_ref, h_ref, init_hbm, xs_hbm, sem):
    del init_hbm
    tb = h_ref.shape[0]

    def issue(r, c):
        _row_copy(h_ref, r, xs_hbm, dest_ref[0, 0, 2 * r], sem).start()
        _row_copy(h_ref, r, xs_hbm, dest_ref[0, 0, 2 * r + 1], sem).start()
        return c

    def drain(r, c):
        _row_copy(h_ref, 0, xs_hbm, 0, sem).wait()
        _row_copy(h_ref, 0, xs_hbm, 0, sem).wait()
        return c

    lax.fori_loop(0, tb, issue, 0, unroll=8)
    lax.fori_loop(0, tb, drain, 0, unroll=8)


def _dispatch(dest3, h, cap):
    T, D = h.shape
    nb = dest3.shape[0]
    tb = dest3.shape[2] // 2
    init = jnp.zeros((cap, D), h.dtype)
    return pl.pallas_call(
        _scatter_kernel,
        grid=(nb,),
        in_specs=[pl.BlockSpec((1, 1, 2 * tb), lambda i: (i, 0, 0), memory_space=pltpu.SMEM),
                  pl.BlockSpec((tb, D), lambda i: (i, 0)),
                  pl.BlockSpec(memory_space=pl.ANY)],
        out_specs=pl.BlockSpec(memory_space=pl.ANY),
        out_shape=jax.ShapeDtypeStruct((cap, D), h.dtype),
        scratch_shapes=[pltpu.SemaphoreType.DMA],
        input_output_aliases={2: 0},
        compiler_params=_cparams(("arbitrary",)),
        name="moe_dispatch",
    )(dest3, h, init)


def _expert_kernel(bexp_ref, nused_ref, x_ref, wg_ref, wu_ref, wd_ref, y_ref):
    i = pl.program_id(0)

    @pl.when(i < nused_ref[0])
    def _():
        x = x_ref[...].astype(BF16)
        gate = jnp.dot(x, wg_ref[0], preferred_element_type=F32)
        up = jnp.dot(x, wu_ref[0], preferred_element_type=F32)
        hid = (gate * _sigmoid(gate) * up).astype(BF16)
        y_ref[...] = jnp.dot(hid, wd_ref[0], preferred_element_type=F32)

    @pl.when(i >= nused_ref[0])
    def _():
        y_ref[...] = jnp.zeros_like(y_ref)


def _expert_mlp(bexp, nused, xs, wg, wu, wd):
    cap, D = xs.shape
    bm = MOE_BM
    H = wg.shape[2]
    grid_spec = pltpu.PrefetchScalarGridSpec(
        num_scalar_prefetch=2,
        grid=(cap // bm,),
        in_specs=[
            pl.BlockSpec((bm, D), lambda i, be, nu: (i, 0)),
            pl.BlockSpec((1, D, H), lambda i, be, nu: (be[i], 0, 0)),
            pl.BlockSpec((1, D, H), lambda i, be, nu: (be[i], 0, 0)),
            pl.BlockSpec((1, H, D), lambda i, be, nu: (be[i], 0, 0)),
        ],
        out_specs=pl.BlockSpec((bm, D), lambda i, be, nu: (i, 0)),
    )
    return pl.pallas_call(
        _expert_kernel,
        grid_spec=grid_spec,
        out_shape=jax.ShapeDtypeStruct((cap, D), F32),
        compiler_params=_cparams(("arbitrary",)),
        name="expert_mlp",
    )(bexp, nused, xs, wg, wu, wd)


def _combine_kernel(dest_ref, h_ref, ew_ref, g_ref, b_ref, y_hbm, o_ref, ybuf, sem):
    tb = h_ref.shape[0]

    def issue(r, c):
        _row_copy(y_hbm, dest_ref[0, 0, 2 * r], ybuf.at[0], r, sem).start()
        _row_copy(y_hbm, dest_ref[0, 0, 2 * r + 1], ybuf.at[1], r, sem).start()
        return c

    def drain(r, c):
        _row_copy(y_hbm, 0, ybuf.at[0], 0, sem).wait()
        _row_copy(y_hbm, 0, ybuf.at[1], 0, sem).wait()
        return c

    lax.fori_loop(0, tb, issue, 0, unroll=8)
    lax.fori_loop(0, tb, drain, 0, unroll=8)
    ew = ew_ref[...]
    ffn = _lane_col(ew, 0) * ybuf[0] + _lane_col(ew, 1) * ybuf[1]
    o_ref[...] = _layer_norm_rows(DN_ALPHA * h_ref[...] + ffn, g_ref[...], b_ref[...])


def _combine(dest3, h, ew, g, b, y):
    T, D = h.shape
    tb = dest3.shape[2] // 2
    return pl.pallas_call(
        _combine_kernel,
        grid=(T // tb,),
        in_specs=[pl.BlockSpec((1, 1, 2 * tb), lambda i: (i, 0, 0), memory_space=pltpu.SMEM),
                  pl.BlockSpec((tb, D), lambda i: (i, 0)),
                  pl.BlockSpec((tb, LANES), lambda i: (i, 0)),
                  pl.BlockSpec((1, D), lambda i: (0, 0)),
                  pl.BlockSpec((1, D), lambda i: (0, 0)),
                  pl.BlockSpec(memory_space=pl.ANY)],
        out_specs=pl.BlockSpec((tb, D), lambda i: (i, 0)),
        out_shape=jax.ShapeDtypeStruct((T, D), F32),
        scratch_shapes=[pltpu.VMEM((2, tb, D), F32), pltpu.SemaphoreType.DMA],
        compiler_params=_cparams(("arbitrary",)),
        name="moe_combine_ln2",
    )(dest3, h, ew, g, b, y)


def _pad_lanes(v, n=LANES):
    v = v.reshape(1, -1).astype(F32)
    return jnp.pad(v, ((0, 0), (0, n - v.shape[1])))


def kernel(x, w_in, lambda_q1, lambda_k1, lambda_q2, lambda_k2, attn_norm_w, conv_w, conv_b, dt_bias, a_log, d_skip, ssm_norm_w, w_out, ln1_g, ln1_b, w_router_group, b_router_group, w_router_expert, b_router_expert, w_gate, w_up, w_down, ln2_g, ln2_b):
    B, S, D = x.shape
    T = B * S
    assert w_in.shape[0] == DEPTH == 1
    assert S % ATT_TQ == 0 and S % SSD_L == 0 and T % POS_TB == 0 and T % TOK_TB == 0
    l = 0
    lambda_init = 0.8 - 0.6 * math.exp(-0.3 * l)
    x2d = x.reshape(T, D)

    w_main = w_in[l][:, :N_MAIN].astype(BF16)
    w_dt = jnp.pad(w_in[l][:, N_MAIN:], ((0, 0), (0, LANES - N_DT))).astype(BF16)
    slopes = jnp.exp2(-8.0 * jnp.arange(1, ATT_HEADS + 1, dtype=F32) / ATT_HEADS)
    lamp = jnp.concatenate([_pad_lanes(lambda_q1[l]), _pad_lanes(lambda_k1[l]),
                            _pad_lanes(lambda_q2[l]), _pad_lanes(lambda_k2[l]),
                            jnp.zeros((4, LANES), F32)], axis=0)
    nw_col = attn_norm_w[l].astype(F32).reshape(ATT_V_DIM, 1)
    cw8 = jnp.pad(conv_w[l].astype(F32), ((0, 8 - SSM_CONV), (0, 0)))
    cb = conv_b[l].astype(F32).reshape(1, N_XBC)
    dskip_exp = jnp.repeat(d_skip[l].astype(F32), SSM_HEAD_DIM).reshape(1, SSM_WIDTH)
    ssm_nw = ssm_norm_w[l].astype(F32).reshape(1, SSM_WIDTH)
    head_of_lane = jnp.arange(SSM_WIDTH, dtype=jnp.int32) // SSM_HEAD_DIM
    e_mat = (jnp.arange(LANES, dtype=jnp.int32)[:, None] == head_of_lane[None, :]).astype(BF16)
    wa = w_out[l][:ATT_WIDTH].astype(BF16)
    ws = w_out[l][ATT_WIDTH:].astype(BF16)
    wr = jnp.concatenate(
        [w_router_group[l], jnp.transpose(w_router_expert[l], (1, 0, 2)).reshape(D, N_EXPERTS)], axis=1)
    wr = jnp.pad(wr, ((0, 0), (0, LANES - wr.shape[1]))).astype(BF16)
    br = _pad_lanes(jnp.concatenate([b_router_group[l], b_router_expert[l].reshape(-1)]))
    row = lambda v: v.astype(F32).reshape(1, D)

    tm_in = 1024 if T % 1024 == 0 else 256
    proj, dt_raw = _in_proj(x2d, w_main, w_dt, tm_in, 512)
    vt = _v_transposed(proj, B, S)
    att = _diff_attention(proj, vt, slopes, lamp, nw_col, B, S, lambda_init)
    ssm = _ssd_mixer(proj, dt_raw, cw8, cb, _pad_lanes(dt_bias[l]), _pad_lanes(a_log[l]),
                     dskip_exp, ssm_nw, e_mat, B, S)
    h1, eid, ew = _out_proj(att, ssm, x2d, wa, ws, row(ln1_g[l]), row(ln1_b[l]), wr, br, 256)

    stril = (jnp.arange(POS_TB)[:, None] > jnp.arange(POS_TB)[None, :]).astype(BF16)
    dest, pend = _positions(eid, stril)
    nblk = (T * 2) // MOE_BM + N_EXPERTS
    cap = nblk * MOE_BM
    pad_ends = pend[0, :N_EXPERTS].astype(jnp.int32)
    blk_start = jnp.arange(nblk, dtype=jnp.int32) * MOE_BM
    nused = (pad_ends[N_EXPERTS - 1] // MOE_BM).astype(jnp.int32)
    last_used = jnp.maximum(nused - 1, 0) * MOE_BM
    bexp = jnp.sum(pad_ends[None, :] <= jnp.minimum(blk_start, last_used)[:, None], axis=1).astype(jnp.int32)
    bexp = jnp.minimum(bexp, N_EXPERTS - 1)
    dest3 = dest[:, :2].reshape(T // TOK_TB, 1, 2 * TOK_TB)
    xs_sorted = _dispatch(dest3, h1, cap)
    y_sorted = _expert_mlp(bexp, nused.reshape(1), xs_sorted,
                           w_gate[l].astype(BF16), w_up[l].astype(BF16), w_down[l].astype(BF16))
    out = _combine(dest3, h1, ew, row(ln2_g[l]), row(ln2_b[l]), y_sorted)
    return out.reshape(B, S, D)
```

```python
import functools
import math

import jax
import jax.numpy as jnp
from jax import lax
from jax.experimental import pallas as pl
from jax.experimental.pallas import tpu as pltpu

F32 = jnp.float32
BF16 = jnp.bfloat16

CHUNK = 64
ATT_HEADS = 8
ATT_HEAD_DIM = 64
ATT_V_DIM = 128
ATT_WIDTH = ATT_HEADS * ATT_V_DIM
ATT_VT_ROWS = ATT_V_DIM + 16
LOG2E = 1.4426950408889634
SSM_HEADS = 16
SSM_HEAD_DIM = 64
SSM_WIDTH = SSM_HEADS * SSM_HEAD_DIM
SSM_GROUPS = 2
SSM_STATE = 128
SSM_CONV = 4
N_Q = 1024
N_K = 1024
N_V = 1024
N_Z = 1024
N_XBC = SSM_WIDTH + 2 * SSM_GROUPS * SSM_STATE
N_DT = SSM_HEADS
N_MAIN = N_Q + N_K + N_V + N_Z + N_XBC
N_EXPERT_GROUPS = 4
EXPERTS_PER_GROUP = 8
N_EXPERTS = 32
EXPERT_HIDDEN = 1024
DEPTH = 1
DN_ALPHA = (2 * DEPTH) ** 0.25
LN_EPS = 1e-5
RMS_EPS = 1e-6
LANES = 128

VMEM_LIMIT = 56 * 1024 * 1024

INPROJ_TN = 1408
ATT_TQ = 512
SSD_L = 256
MOE_BM = 256
TOK_TB = 256
POS_TB = 512


def _cparams(sem, flags=None):
    return pltpu.CompilerParams(dimension_semantics=sem, vmem_limit_bytes=VMEM_LIMIT, flags=flags)


def _sigmoid(x):
    return 1.0 / (1.0 + jnp.exp(-x))


def _lane_col(x, idx):
    lane = lax.broadcasted_iota(jnp.int32, x.shape, 1)
    return jnp.sum(jnp.where(lane == idx, x, 0.0), axis=-1, keepdims=True)


def _inproj_kernel(x_ref, w_ref, wdt_ref, o_ref, dt_ref, xb_ref):
    @pl.when(pl.program_id(1) == 0)
    def _():
        xb = x_ref[...].astype(BF16)
        xb_ref[...] = xb
        dt_ref[...] = jnp.dot(xb, wdt_ref[...], preferred_element_type=F32)

    o_ref[...] = jnp.dot(xb_ref[...], w_ref[...], preferred_element_type=F32).astype(o_ref.dtype)


def _in_proj(x2d, w_main, w_dt, tm, tn):
    T, D = x2d.shape
    N = w_main.shape[1]
    return pl.pallas_call(
        _inproj_kernel,
        grid=(T // tm, N // tn),
        in_specs=[
            pl.BlockSpec((tm, D), lambda i, j: (i, 0)),
            pl.BlockSpec((D, tn), lambda i, j: (0, j)),
            pl.BlockSpec((D, LANES), lambda i, j: (0, 0)),
        ],
        out_specs=[
            pl.BlockSpec((tm, tn), lambda i, j: (i, j)),
            pl.BlockSpec((tm, LANES), lambda i, j: (i, 0)),
        ],
        out_shape=[
            jax.ShapeDtypeStruct((T, N), BF16),
            jax.ShapeDtypeStruct((T, LANES), F32),
        ],
        scratch_shapes=[pltpu.VMEM((tm, D), BF16)],
        compiler_params=_cparams(("arbitrary", "arbitrary")),
        name="in_proj",
    )(x2d, w_main, w_dt)


def _attn_kernel(slopes_ref, q_ref, k_ref, vt_ref, lamp_ref, nw_ref, o_ref,
                 tab_ref, acc_ref, sa_ref, sb_ref, pa_ref, pb_ref, mxa_ref, mxb_ref, ala_ref, alb_ref,
                 qm_ref, *, lambda_init):
    h = pl.program_id(1)
    tq = tab_ref.shape[1]
    tk = tq
    nq = q_ref.shape[0] // tq
    slope2 = slopes_ref[h] * LOG2E

    s_rel = lax.broadcasted_iota(jnp.int32, (tk, tq), 0)
    t_rel = lax.broadcasted_iota(jnp.int32, (tk, tq), 1)
    tab_ref[0] = slope2 * s_rel.astype(F32)
    allowed = (s_rel // CHUNK) <= (t_rel // CHUNK)
    val = slope2 * (t_rel - jnp.abs(t_rel - s_rel)).astype(F32)
    tab_ref[1] = jnp.where(allowed, val, -jnp.inf)

    def prep_q(i, c):
        rows = pl.ds(pl.multiple_of(i * tq, tq), tq)
        q = q_ref[rows, :]
        lane = lax.broadcasted_iota(jnp.int32, q.shape, 1)
        qs = (q.astype(F32) * (ATT_HEAD_DIM ** -0.5 * LOG2E)).astype(BF16)
        zero = jnp.zeros_like(qs)
        qm_ref[0, rows, :] = jnp.where(lane < ATT_HEAD_DIM, qs, zero)
        qm_ref[1, rows, :] = jnp.where(lane >= ATT_HEAD_DIM, qs, zero)
        return c

    lax.fori_loop(0, nq, prep_q, 0)
    nt = (((1,), (1,)), ((), ()))
    lamp = lamp_ref[...]
    lam = (jnp.exp(jnp.sum(lamp[0:1] * lamp[1:2], axis=-1, keepdims=True))
           - jnp.exp(jnp.sum(lamp[2:3] * lamp[3:4], axis=-1, keepdims=True)) + lambda_init)

    half = tq // 2
    strip = 16

    def scores_piece(pair, s_out, mx_out, m, c):
        qi, j = pair
        cols = slice(c * half, (c + 1) * half)
        kb = k_ref[pl.ds(pl.multiple_of(j * tk, tk), tk), :]
        qh = qm_ref[m, pl.ds(pl.multiple_of(qi * tq + c * half, half), half), :]
        st = (lax.dot_general(kb, qh, nt, preferred_element_type=F32)
              + tab_ref[(j == qi).astype(jnp.int32), :, cols])
        s_out[m, :, cols] = st
        mx_out[m, :, cols] = jnp.max(st, axis=0, keepdims=True)

    def accum_piece(pair, p_in, al_in, m, c):
        cols = slice(c * half, (c + 1) * half)
        acc_ref[m, :, cols] = (al_in[m][:, cols] * acc_ref[m, :, cols]
                               + jnp.dot(vt_ref[pair[1]], p_in[m, :, cols], preferred_element_type=F32))

    def finalize(qi):
        a0 = acc_ref[0]
        a1 = acc_ref[1]
        dv = ATT_V_DIM
        o = a0[:dv] / a0[dv:dv + 1] - lam * (a1[:dv] / a1[dv:dv + 1])
        ms2 = jnp.mean(o * o, axis=0, keepdims=True)
        o = o * lax.rsqrt(ms2 + RMS_EPS) * nw_ref[...] * (1.0 - lambda_init)
        o_ref[pl.ds(pl.multiple_of(qi * tq, tq), tq), :] = o.T.astype(o_ref.dtype)

    def next_pair(pair):
        qi, j = pair
        wrap = j == qi
        return jnp.where(wrap, qi + 1, qi), jnp.where(wrap, 0, j + 1)

    def trip(cur_set, nxt_set, state):
        s_c, mx_c, p_c, al_c = cur_set
        s_n, mx_n, p_n, al_n = nxt_set
        prv, cur, m_old = state
        nxt = next_pair(cur)
        nxt_c = (jnp.minimum(nxt[0], nq - 1), jnp.where(nxt[0] >= nq, nq - 1, nxt[1]))
        prv_c = (prv[0], jnp.maximum(prv[1], 0))
        cj = -slope2 * ((cur[0] - cur[1]) * tk).astype(F32)
        refs, m_out = [], []
        for m in range(2):
            m_prev = jnp.where(cur[1] == 0, -jnp.inf, m_old[m])
            m_new = jnp.maximum(m_prev, mx_c[m] + cj)
            al_c[m] = jnp.exp2(m_prev - m_new)
            refs.append(m_new - cj)
            m_out.append(m_new)

        pieces = ([functools.partial(accum_piece, prv_c, p_n, al_n, m, c) for m in range(2) for c in range(2)]
                  + [functools.partial(scores_piece, nxt_c, s_n, mx_n, m, c) for m in range(2) for c in range(2)])
        strips = [(m, r) for m in range(2) for r in range(tk // strip)]
        per = len(strips) // len(pieces)
        for g, piece in enumerate(pieces):
            piece()
            for m, r in strips[g * per:(g + 1) * per]:
                rows = slice(r * strip, (r + 1) * strip)
                p_c[m, rows, :] = jnp.exp2(s_c[m, rows, :] - refs[m]).astype(BF16)

        @pl.when(prv[1] == prv[0])
        def _():
            finalize(prv[0])

        return cur, nxt, tuple(m_out)

    set_a = (sa_ref, mxa_ref, pa_ref, ala_ref)
    set_b = (sb_ref, mxb_ref, pb_ref, alb_ref)
    acc_ref[...] = jnp.zeros_like(acc_ref)
    pb_ref[...] = jnp.zeros_like(pb_ref)
    alb_ref[...] = jnp.ones_like(alb_ref)
    zero_i = jnp.int32(0)
    for m in range(2):
        for c in range(2):
            scores_piece((zero_i, zero_i), sa_ref, mxa_ref, m, c)

    def body(t, state):
        return trip(set_b, set_a, trip(set_a, set_b, state))

    n_pairs = nq * (nq + 1) // 2
    m_init = jnp.full((1, tq), -jnp.inf, F32)
    first = ((zero_i, jnp.int32(-1)), (zero_i, zero_i), (m_init, m_init))
    state = lax.fori_loop(0, n_pairs // 2, body, first)
    if n_pairs % 2:
        trip(set_a, set_b, state)

    last_set = set_a if (n_pairs - 1) % 2 == 0 else set_b
    last_pair = (jnp.int32(nq - 1), jnp.int32(nq - 1))
    for m in range(2):
        for c in range(2):
            accum_piece(last_pair, last_set[2], last_set[3], m, c)
    finalize(last_pair[0])


def _v_transposed(proj, B, S):
    nq = S // ATT_TQ
    v = proj[:, N_Q + N_K:N_Q + N_K + N_V].reshape(B, nq, ATT_TQ, ATT_HEADS, ATT_V_DIM)
    vt = jnp.transpose(v, (0, 3, 1, 4, 2))
    extra = ATT_VT_ROWS - ATT_V_DIM
    ones_rows = (jnp.arange(extra) == 0).astype(BF16)[:, None]
    return jnp.concatenate([vt, jnp.broadcast_to(ones_rows, (B, ATT_HEADS, nq, extra, ATT_TQ))], axis=3)


def _diff_attention(proj, vt, slopes, lamp, nw_col, B, S, lambda_init):
    T = B * S
    tq = ATT_TQ
    nq = S // tq
    kern = functools.partial(_attn_kernel, lambda_init=lambda_init)
    grid_spec = pltpu.PrefetchScalarGridSpec(
        num_scalar_prefetch=1,
        grid=(B, ATT_HEADS),
        in_specs=[
            pl.BlockSpec((S, LANES), lambda b, h, s: (b, h)),
            pl.BlockSpec((S, LANES), lambda b, h, s: (b, N_Q // LANES + h)),
            pl.BlockSpec((None, None, nq, ATT_VT_ROWS, tq), lambda b, h, s: (b, h, 0, 0, 0)),
            pl.BlockSpec((8, LANES), lambda b, h, s: (0, 0)),
            pl.BlockSpec((ATT_V_DIM, 1), lambda b, h, s: (0, 0)),
        ],
        out_specs=pl.BlockSpec((S, ATT_V_DIM), lambda b, h, s: (b, h)),
        scratch_shapes=[
            pltpu.VMEM((2, tq, tq), F32),
            pltpu.VMEM((2, ATT_VT_ROWS, tq), F32),
            pltpu.VMEM((2, tq, tq), F32), pltpu.VMEM((2, tq, tq), F32),
            pltpu.VMEM((2, tq, tq), BF16), pltpu.VMEM((2, tq, tq), BF16),
            pltpu.VMEM((2, 1, tq), F32), pltpu.VMEM((2, 1, tq), F32),
            pltpu.VMEM((2, 1, tq), F32), pltpu.VMEM((2, 1, tq), F32),
            pltpu.VMEM((2, S, LANES), BF16),
        ],
    )
    return pl.pallas_call(
        kern,
        grid_spec=grid_spec,
        out_shape=jax.ShapeDtypeStruct((T, ATT_WIDTH), BF16),
        compiler_params=_cparams(("arbitrary", "arbitrary")),
        name="diff_attention",
    )(slopes, proj, proj, vt, lamp, nw_col)


def _expand_heads(v, e):
    hi = v.astype(BF16)
    lo = (v - hi.astype(F32)).astype(BF16)
    return jnp.dot(hi, e, preferred_element_type=F32) + jnp.dot(lo, e, preferred_element_type=F32)


def _ssd_kernel(z_ref, xs_ref, b_ref, c_ref, dt_ref, cw_ref, cb_ref, dtb_ref, alog_ref,
                dskip_ref, nw_ref, e_ref, o_ref, ext_ref, st_ref, y_ref):
    blk = pl.program_id(1)
    L = z_ref.shape[0]
    nchunk = L // CHUNK
    gw = SSM_WIDTH // SSM_GROUPS
    hpg = SSM_HEADS // SSM_GROUPS

    @pl.when(blk == 0)
    def _():
        ext_ref[0:8, :] = jnp.zeros((8, N_XBC), F32)
        st_ref[...] = jnp.zeros_like(st_ref)

    cur = jnp.concatenate([xs_ref[...], b_ref[...], c_ref[...]], axis=1).astype(F32)
    ext_ref[8:, :] = cur
    cw = cw_ref[...]
    conv = cb_ref[...] + cw[3:4] * cur
    for j in range(SSM_CONV - 1):
        conv = conv + cw[j:j + 1] * ext_ref[pl.ds(8 - (SSM_CONV - 1) + j, L), :]
    ext_ref[0:8, :] = cur[L - 8:, :]
    xbc = conv * _sigmoid(conv)
    xs = xbc[:, :SSM_WIDTH]
    bmb = xbc[:, SSM_WIDTH:SSM_WIDTH + SSM_GROUPS * SSM_STATE].astype(BF16)
    cmb = xbc[:, SSM_WIDTH + SSM_GROUPS * SSM_STATE:].astype(BF16)
    xsb = xs.astype(BF16)

    lane1 = lax.broadcasted_iota(jnp.int32, (1, LANES), 1)
    dtx = dt_ref[...] + dtb_ref[...]
    dtp = jnp.maximum(dtx, 0.0) + jnp.log1p(jnp.exp(-jnp.abs(dtx)))
    a_head = jnp.where(lane1 < SSM_HEADS, -jnp.exp(alog_ref[...]), 0.0)
    acs = dtp * a_head
    row_in_chunk = lax.broadcasted_iota(jnp.int32, (L, LANES), 0) & (CHUNK - 1)
    k = 1
    while k < CHUNK:
        acs = acs + jnp.where(row_in_chunk >= k, pltpu.roll(acs, k, axis=0), 0.0)
        k *= 2
    acs_t = acs.T
    dt_t = dtp.T

    e = e_ref[...]
    acs_last = jnp.concatenate(
        [jnp.broadcast_to(acs[c * CHUNK + CHUNK - 1:(c + 1) * CHUNK, :], (CHUNK, LANES)) for c in range(nchunk)],
        axis=0)
    w_exp = _expand_heads(dtp * jnp.exp(acs_last - acs), e)
    od_exp = _expand_heads(jnp.exp(acs), e)
    row8 = lax.broadcasted_iota(jnp.int32, (8, LANES), 0)
    cd8 = jnp.zeros((8, LANES), F32)
    for c in range(nchunk):
        cd8 = jnp.where(row8 == c, jnp.exp(acs[c * CHUNK + CHUNK - 1:(c + 1) * CHUNK, :]), cd8)
    cd_exp = _expand_heads(cd8, e)
    xw = (xs * w_exp).astype(BF16)

    tn = (((0,), (0,)), ((), ()))
    for c in range(nchunk):
        r0 = c * CHUNK
        for g in range(SSM_GROUPS):
            st = st_ref[g]
            cg = cmb[r0:r0 + CHUNK, g * SSM_STATE:(g + 1) * SSM_STATE]
            bg = bmb[r0:r0 + CHUNK, g * SSM_STATE:(g + 1) * SSM_STATE]
            y_ref[r0:r0 + CHUNK, g * gw:(g + 1) * gw] = jnp.dot(cg, st.astype(BF16), preferred_element_type=F32)
            snew = lax.dot_general(bg, xw[r0:r0 + CHUNK, g * gw:(g + 1) * gw], tn, preferred_element_type=F32)
            st_ref[g] = st * cd_exp[c:c + 1, g * gw:(g + 1) * gw] + snew
    y = y_ref[...] * od_exp + xs * dskip_ref[...]

    pair = 2 * CHUNK
    li = lax.broadcasted_iota(jnp.int32, (pair, pair), 0)
    si = lax.broadcasted_iota(jnp.int32, (pair, pair), 1)
    mask2 = (li >= si) & ((si >= CHUNK) | (li < CHUNK))
    lanep = lax.broadcasted_iota(jnp.int32, (pair, LANES), 1)
    nt = (((1,), (1,)), ((), ()))
    for pp in range(L // pair):
        r0 = pp * pair
        acs_p = acs[r0:r0 + pair, :]
        for g in range(SSM_GROUPS):
            cb2 = lax.dot_general(cmb[r0:r0 + pair, g * SSM_STATE:(g + 1) * SSM_STATE],
                                  bmb[r0:r0 + pair, g * SSM_STATE:(g + 1) * SSM_STATE],
                                  nt, preferred_element_type=F32)
            for hh in range(hpg // 2):
                hp = g * (hpg // 2) + hh
                mats = []
                for u in range(2):
                    hd = 2 * hp + u
                    seg = _lane_col(acs_p, hd) - acs_t[hd:hd + 1, r0:r0 + pair]
                    decay = jnp.exp(jnp.where(mask2, seg, -jnp.inf))
                    mats.append((cb2 * decay * dt_t[hd:hd + 1, r0:r0 + pair]).astype(BF16))
                lhs = jnp.concatenate(mats, axis=1)
                xp = xsb[r0:r0 + pair, hp * LANES:(hp + 1) * LANES]
                zero = jnp.zeros_like(xp)
                rhs = jnp.concatenate([jnp.where(lanep < SSM_HEAD_DIM, xp, zero),
                                       jnp.where(lanep >= SSM_HEAD_DIM, xp, zero)], axis=0)
                y_ref[r0:r0 + pair, hp * LANES:(hp + 1) * LANES] = jnp.dot(lhs, rhs, preferred_element_type=F32)
    y = y + y_ref[...]

    z = z_ref[...].astype(F32)
    y = y * (z * _sigmoid(z))
    outs = []
    for g in range(SSM_GROUPS):
        yg = y[:, g * gw:(g + 1) * gw]
        outs.append(yg * lax.rsqrt(jnp.mean(yg * yg, axis=-1, keepdims=True) + RMS_EPS))
    o_ref[...] = (jnp.concatenate(outs, axis=1) * nw_ref[...]).astype(o_ref.dtype)


def _ssd_mixer(proj, dt_raw, cw8, cb, dtb, alog, dskip_exp, nw, e_mat, B, S):
    T = B * S
    L = SSD_L
    nb = S // L
    row = lambda b, i: b * nb + i
    col0 = (N_Q + N_K + N_V + N_Z)
    const = lambda shape: pl.BlockSpec(shape, lambda b, i: (0, 0))
    return pl.pallas_call(
        _ssd_kernel,
        grid=(B, nb),
        in_specs=[
            pl.BlockSpec((L, N_Z), lambda b, i: (row(b, i), (N_Q + N_K + N_V) // N_Z)),
            pl.BlockSpec((L, SSM_WIDTH), lambda b, i: (row(b, i), col0 // SSM_WIDTH)),
            pl.BlockSpec((L, 256), lambda b, i: (row(b, i), (col0 + SSM_WIDTH) // 256)),
            pl.BlockSpec((L, 256), lambda b, i: (row(b, i), (col0 + SSM_WIDTH + 256) // 256)),
            pl.BlockSpec((L, LANES), lambda b, i: (row(b, i), 0)),
            const((8, N_XBC)),
            const((1, N_XBC)),
            const((1, LANES)),
            const((1, LANES)),
            const((1, SSM_WIDTH)),
            const((1, SSM_WIDTH)),
            const((LANES, SSM_WIDTH)),
        ],
        out_specs=pl.BlockSpec((L, SSM_WIDTH), lambda b, i: (row(b, i), 0)),
        out_shape=jax.ShapeDtypeStruct((T, SSM_WIDTH), BF16),
        scratch_shapes=[
            pltpu.VMEM((L + 8, N_XBC), F32),
            pltpu.VMEM((SSM_GROUPS, SSM_STATE, SSM_WIDTH // SSM_GROUPS), F32),
            pltpu.VMEM((L, SSM_WIDTH), F32),
        ],
        compiler_params=_cparams(("arbitrary", "arbitrary")),
        name="ssd_mixer",
    )(proj, proj, proj, proj, dt_raw, cw8, cb, dtb, alog, dskip_exp, nw, e_mat)


def _layer_norm_rows(r, g, b):
    mu = jnp.mean(r, axis=-1, keepdims=True)
    d = r - mu
    var = jnp.mean(d * d, axis=-1, keepdims=True)
    return d * lax.rsqrt(var + LN_EPS) * g + b


def _outproj_kernel(att_ref, ssm_ref, x_ref, wa_ref, ws_ref, g_ref, b_ref, wr_ref, br_ref,
                    h_ref, eid_ref, ew_ref):
    mix = (jnp.dot(att_ref[...], wa_ref[...], preferred_element_type=F32)
           + jnp.dot(ssm_ref[...], ws_ref[...], preferred_element_type=F32))
    h = _layer_norm_rows(DN_ALPHA * x_ref[...] + mix, g_ref[...], b_ref[...])
    h_ref[...] = h

    logits = jnp.dot(h.astype(BF16), wr_ref[...], preferred_element_type=F32) + br_ref[...]
    lane = lax.broadcasted_iota(jnp.int32, logits.shape, 1)
    lanef = lane.astype(F32)
    big = float(LANES)
    gl = jnp.where(lane < N_EXPERT_GROUPS, logits, -jnp.inf)
    gmax = jnp.max(gl, axis=-1, keepdims=True)
    g_prob = 1.0 / jnp.sum(jnp.exp(gl - gmax), axis=-1, keepdims=True)
    gidx = jnp.min(jnp.where(gl == gmax, lanef, big), axis=-1, keepdims=True)
    lo = N_EXPERT_GROUPS + EXPERTS_PER_GROUP * gidx
    el = jnp.where((lanef >= lo) & (lanef < lo + EXPERTS_PER_GROUP), logits, -jnp.inf)
    t1 = jnp.max(el, axis=-1, keepdims=True)
    i1 = jnp.min(jnp.where(el == t1, lanef, big), axis=-1, keepdims=True)
    el2 = jnp.where(lanef == i1, -jnp.inf, el)
    t2 = jnp.max(el2, axis=-1, keepdims=True)
    i2 = jnp.min(jnp.where(el2 == t2, lanef, big), axis=-1, keepdims=True)
    ex = jnp.exp(t2 - t1)
    w1 = g_prob / (1.0 + ex)
    w2 = g_prob * ex / (1.0 + ex)
    eid = jnp.where(lane == 0, i1 - N_EXPERT_GROUPS, jnp.where(lane == 1, i2 - N_EXPERT_GROUPS, 0.0))
    eid_ref[...] = eid.astype(jnp.int32)
    ew_ref[...] = jnp.where(lane == 0, w1, jnp.where(lane == 1, w2, 0.0))


def _out_proj(att, ssm, x2d, wa, ws, g, b, wr, br, tm):
    T, D = x2d.shape
    const = lambda shape: pl.BlockSpec(shape, lambda i: (0, 0))
    rows = lambda w: pl.BlockSpec((tm, w), lambda i: (i, 0))
    return pl.pallas_call(
        _outproj_kernel,
        grid=(T // tm,),
        in_specs=[rows(ATT_WIDTH), rows(SSM_WIDTH), rows(D), const((ATT_WIDTH, D)), const((SSM_WIDTH, D)),
                  const((1, D)), const((1, D)), const((D, LANES)), const((1, LANES))],
        out_specs=[rows(D), rows(LANES), rows(LANES)],
        out_shape=[jax.ShapeDtypeStruct((T, D), F32),
                   jax.ShapeDtypeStruct((T, LANES), jnp.int32),
                   jax.ShapeDtypeStruct((T, LANES), F32)],
        compiler_params=_cparams(("arbitrary",)),
        name="out_proj_ln1_router",
    )(att, ssm, x2d, wa, ws, g, b, wr, br)


def _pos_kernel(eid_ref, stril_ref, dest_ref, pend_ref, tot_ref, run_ref, pstart_ref):
    ph = pl.program_id(0)
    i = pl.program_id(1)
    tb = eid_ref.shape[0]
    lane = lax.broadcasted_iota(jnp.int32, (tb, LANES), 1)
    lanef = lane.astype(F32)
    ef = eid_ref[...].astype(F32)
    oh1 = (lanef == _lane_col(ef, 0)).astype(F32)
    oh2 = (lanef == _lane_col(ef, 1)).astype(F32)
    cnt = oh1 + oh2

    @pl.when((ph == 0) & (i == 0))
    def _():
        tot_ref[...] = jnp.zeros_like(tot_ref)

    @pl.when(ph == 0)
    def _():
        tot_ref[...] += jnp.sum(cnt, axis=0, keepdims=True)

    @pl.when((ph == 1) & (i == 0))
    def _():
        tot = jnp.broadcast_to(tot_ref[...], (8, LANES))
        padded = jnp.floor((tot + (MOE_BM - 1)) * (1.0 / MOE_BM)) * MOE_BM
        lane8 = lax.broadcasted_iota(jnp.int32, (8, LANES), 1)
        ends = padded
        k = 1
        while k < LANES:
            ends = ends + jnp.where(lane8 >= k, pltpu.roll(ends, k, axis=1), 0.0)
            k *= 2
        row8 = lax.broadcasted_iota(jnp.int32, (8, LANES), 0)
        pend_ref[...] = jnp.where(row8 == 0, ends, jnp.where(row8 == 1, tot, 0.0))
        pstart_ref[...] = (ends - padded)[0:1]
        run_ref[...] = jnp.zeros_like(run_ref)

    @pl.when(ph == 1)
    def _():
        pre = jnp.dot(stril_ref[...], cnt.astype(BF16), preferred_element_type=F32)
        slot = pstart_ref[...] + run_ref[...] + pre
        d1 = jnp.sum(oh1 * slot, axis=-1, keepdims=True)
        d2 = jnp.sum(oh2 * slot, axis=-1, keepdims=True)
        dest_ref[...] = jnp.where(lane == 0, d1, jnp.where(lane == 1, d2, 0.0)).astype(jnp.int32)
        run_ref[...] += jnp.sum(cnt, axis=0, keepdims=True)


def _positions(eid, stril):
    T = eid.shape[0]
    tb = POS_TB
    return pl.pallas_call(
        _pos_kernel,
        grid=(2, T // tb),
        in_specs=[pl.BlockSpec((tb, LANES), lambda p, i: (i, 0)),
                  pl.BlockSpec((tb, tb), lambda p, i: (0, 0))],
        out_specs=[pl.BlockSpec((tb, LANES), lambda p, i: (i * p, 0)),
                   pl.BlockSpec((8, LANES), lambda p, i: (0, 0))],
        out_shape=[jax.ShapeDtypeStruct((T, LANES), jnp.int32),
                   jax.ShapeDtypeStruct((8, LANES), F32)],
        scratch_shapes=[pltpu.VMEM((1, LANES), F32), pltpu.VMEM((1, LANES), F32), pltpu.VMEM((1, LANES), F32)],
        compiler_params=_cparams(("arbitrary", "arbitrary")),
        name="dispatch_positions",
    )(eid, stril)


def _row_copy(src, s, dst, d, sem):
    return pltpu.make_async_copy(src.at[pl.ds(s, 1)], dst.at[pl.ds(d, 1)], sem)


def _scatter_kernel(zstart_ref, zcnt_ref, dest_ref, h_ref, xs_hbm, zrow_ref, sem, zsem):
    tb = h_ref.shape[0]

    @pl.when(pl.program_id(0) == 0)
    def _():
        zrow_ref[...] = jnp.zeros_like(zrow_ref)

        def fill(e, c):
            lax.fori_loop(0, zcnt_ref[e],
                          lambda r, c2: (_row_copy(zrow_ref, 0, xs_hbm, zstart_ref[e] + r, zsem).start(), c2)[1], 0)
            return c

        def fill_wait(e, c):
            lax.fori_loop(0, zcnt_ref[e],
                          lambda r, c2: (_row_copy(zrow_ref, 0, xs_hbm, 0, zsem).wait(), c2)[1], 0)
            return c

        nblk = xs_hbm.shape[0] // MOE_BM
        first_free = zstart_ref[N_EXPERTS]

        def tail_copy(b):
            return pltpu.make_async_copy(zrow_ref, xs_hbm.at[pl.ds(b * MOE_BM, MOE_BM)], zsem)

        lax.fori_loop(0, N_EXPERTS, fill, 0)
        lax.fori_loop(first_free, nblk, lambda b, c: (tail_copy(b).start(), c)[1], 0)
        lax.fori_loop(0, N_EXPERTS, fill_wait, 0)
        lax.fori_loop(first_free, nblk, lambda b, c: (tail_copy(b).wait(), c)[1], 0)

    def issue(r, c):
        _row_copy(h_ref, r, xs_hbm, dest_ref[0, 0, 2 * r], sem).start()
        _row_copy(h_ref, r, xs_hbm, dest_ref[0, 0, 2 * r + 1], sem).start()
        return c

    def drain(r, c):
        _row_copy(h_ref, 0, xs_hbm, 0, sem).wait()
        _row_copy(h_ref, 0, xs_hbm, 0, sem).wait()
        return c

    lax.fori_loop(0, tb, issue, 0, unroll=8)
    lax.fori_loop(0, tb, drain, 0, unroll=8)


def _dispatch(zstart, zcnt, dest3, h, cap):
    T, D = h.shape
    nb = dest3.shape[0]
    tb = dest3.shape[2] // 2
    grid_spec = pltpu.PrefetchScalarGridSpec(
        num_scalar_prefetch=2,
        grid=(nb,),
        in_specs=[pl.BlockSpec((1, 1, 2 * tb), lambda i, zs, zc: (i, 0, 0), memory_space=pltpu.SMEM),
                  pl.BlockSpec((tb, D), lambda i, zs, zc: (i, 0))],
        out_specs=pl.BlockSpec(memory_space=pl.ANY),
        scratch_shapes=[pltpu.VMEM((MOE_BM, D), h.dtype), pltpu.SemaphoreType.DMA, pltpu.SemaphoreType.DMA],
    )
    return pl.pallas_call(
        _scatter_kernel,
        grid_spec=grid_spec,
        out_shape=jax.ShapeDtypeStruct((cap, D), h.dtype),
        compiler_params=_cparams(("arbitrary",)),
        name="moe_dispatch",
    )(zstart, zcnt, dest3, h)


def _expert_kernel(bexp_ref, nused_ref, x_ref, wg_ref, wu_ref, wd_ref, y_ref):
    i = pl.program_id(0)

    @pl.when(i < nused_ref[0])
    def _():
        x = x_ref[...].astype(BF16)
        gate = jnp.dot(x, wg_ref[0], preferred_element_type=F32)
        up = jnp.dot(x, wu_ref[0], preferred_element_type=F32)
        hid = (gate * _sigmoid(gate) * up).astype(BF16)
        y_ref[...] = jnp.dot(hid, wd_ref[0], preferred_element_type=F32)

    @pl.when(i >= nused_ref[0])
    def _():
        y_ref[...] = jnp.zeros_like(y_ref)


def _expert_mlp(bexp, nused, xs, wg, wu, wd):
    cap, D = xs.shape
    bm = MOE_BM
    H = wg.shape[2]
    grid_spec = pltpu.PrefetchScalarGridSpec(
        num_scalar_prefetch=2,
        grid=(cap // bm,),
        in_specs=[
            pl.BlockSpec((bm, D), lambda i, be, nu: (i, 0)),
            pl.BlockSpec((1, D, H), lambda i, be, nu: (be[i], 0, 0)),
            pl.BlockSpec((1, D, H), lambda i, be, nu: (be[i], 0, 0)),
            pl.BlockSpec((1, H, D), lambda i, be, nu: (be[i], 0, 0)),
        ],
        out_specs=pl.BlockSpec((bm, D), lambda i, be, nu: (i, 0)),
    )
    return pl.pallas_call(
        _expert_kernel,
        grid_spec=grid_spec,
        out_shape=jax.ShapeDtypeStruct((cap, D), F32),
        compiler_params=_cparams(("arbitrary",)),
        name="expert_mlp",
    )(bexp, nused, xs, wg, wu, wd)


def _combine_kernel(dest_ref, dnext_ref, h_ref, ew_ref, g_ref, b_ref, y_hbm, o_ref, ybuf, sems):
    i = pl.program_id(0)
    n = pl.num_programs(0)
    tb = h_ref.shape[0]
    slot = i & 1

    def gather(d_ref, s):
        def issue(r, c):
            _row_copy(y_hbm, d_ref[0, 0, 2 * r], ybuf.at[s, 0], r, sems.at[s]).start()
            _row_copy(y_hbm, d_ref[0, 0, 2 * r + 1], ybuf.at[s, 1], r, sems.at[s]).start()
            return c
        lax.fori_loop(0, tb, issue, 0, unroll=8)

    @pl.when(i == 0)
    def _():
        gather(dest_ref, slot)

    @pl.when(i + 1 < n)
    def _():
        gather(dnext_ref, 1 - slot)

    def drain(r, c):
        _row_copy(y_hbm, 0, ybuf.at[slot, 0], 0, sems.at[slot]).wait()
        _row_copy(y_hbm, 0, ybuf.at[slot, 1], 0, sems.at[slot]).wait()
        return c

    lax.fori_loop(0, tb, drain, 0, unroll=8)
    ew = ew_ref[...]
    ffn = _lane_col(ew, 0) * ybuf[slot, 0] + _lane_col(ew, 1) * ybuf[slot, 1]
    o_ref[...] = _layer_norm_rows(DN_ALPHA * h_ref[...] + ffn, g_ref[...], b_ref[...])


def _combine(dest3, h, ew, g, b, y):
    T, D = h.shape
    tb = dest3.shape[2] // 2
    nb = T // tb
    return pl.pallas_call(
        _combine_kernel,
        grid=(nb,),
        in_specs=[pl.BlockSpec((1, 1, 2 * tb), lambda i: (i, 0, 0), memory_space=pltpu.SMEM),
                  pl.BlockSpec((1, 1, 2 * tb), lambda i: (jnp.minimum(i + 1, nb - 1), 0, 0),
                               memory_space=pltpu.SMEM),
                  pl.BlockSpec((tb, D), lambda i: (i, 0)),
                  pl.BlockSpec((tb, LANES), lambda i: (i, 0)),
                  pl.BlockSpec((1, D), lambda i: (0, 0)),
                  pl.BlockSpec((1, D), lambda i: (0, 0)),
                  pl.BlockSpec(memory_space=pl.ANY)],
        out_specs=pl.BlockSpec((tb, D), lambda i: (i, 0)),
        out_shape=jax.ShapeDtypeStruct((T, D), F32),
        scratch_shapes=[pltpu.VMEM((2, 2, tb, D), F32), pltpu.SemaphoreType.DMA((2,))],
        compiler_params=_cparams(("arbitrary",)),
        name="moe_combine_ln2",
    )(dest3, dest3, h, ew, g, b, y)


def _pad_lanes(v, n=LANES):
    v = v.reshape(1, -1).astype(F32)
    return jnp.pad(v, ((0, 0), (0, n - v.shape[1])))


def kernel(x, w_in, lambda_q1, lambda_k1, lambda_q2, lambda_k2, attn_norm_w, conv_w, conv_b, dt_bias, a_log, d_skip, ssm_norm_w, w_out, ln1_g, ln1_b, w_router_group, b_router_group, w_router_expert, b_router_expert, w_gate, w_up, w_down, ln2_g, ln2_b):
    B, S, D = x.shape
    T = B * S
    assert w_in.shape[0] == DEPTH == 1
    assert S % ATT_TQ == 0 and S % SSD_L == 0 and T % POS_TB == 0 and T % TOK_TB == 0
    l = 0
    lambda_init = 0.8 - 0.6 * math.exp(-0.3 * l)
    x2d = x.reshape(T, D)

    w_main = w_in[l][:, :N_MAIN].astype(BF16)
    w_dt = jnp.pad(w_in[l][:, N_MAIN:], ((0, 0), (0, LANES - N_DT))).astype(BF16)
    slopes = jnp.exp2(-8.0 * jnp.arange(1, ATT_HEADS + 1, dtype=F32) / ATT_HEADS)
    lamp = jnp.concatenate([_pad_lanes(lambda_q1[l]), _pad_lanes(lambda_k1[l]),
                            _pad_lanes(lambda_q2[l]), _pad_lanes(lambda_k2[l]),
                            jnp.zeros((4, LANES), F32)], axis=0)
    nw_col = attn_norm_w[l].astype(F32).reshape(ATT_V_DIM, 1)
    cw8 = jnp.pad(conv_w[l].astype(F32), ((0, 8 - SSM_CONV), (0, 0)))
    cb = conv_b[l].astype(F32).reshape(1, N_XBC)
    dskip_exp = jnp.repeat(d_skip[l].astype(F32), SSM_HEAD_DIM).reshape(1, SSM_WIDTH)
    ssm_nw = ssm_norm_w[l].astype(F32).reshape(1, SSM_WIDTH)
    head_of_lane = jnp.arange(SSM_WIDTH, dtype=jnp.int32) // SSM_HEAD_DIM
    e_mat = (jnp.arange(LANES, dtype=jnp.int32)[:, None] == head_of_lane[None, :]).astype(BF16)
    wa = w_out[l][:ATT_WIDTH].astype(BF16)
    ws = w_out[l][ATT_WIDTH:].astype(BF16)
    wr = jnp.concatenate(
        [w_router_group[l], jnp.transpose(w_router_expert[l], (1, 0, 2)).reshape(D, N_EXPERTS)], axis=1)
    wr = jnp.pad(wr, ((0, 0), (0, LANES - wr.shape[1]))).astype(BF16)
    br = _pad_lanes(jnp.concatenate([b_router_group[l], b_router_expert[l].reshape(-1)]))
    row = lambda v: v.astype(F32).reshape(1, D)

    tm_in = 1024 if T % 1024 == 0 else 256
    proj, dt_raw = _in_proj(x2d, w_main, w_dt, tm_in, INPROJ_TN)
    vt = _v_transposed(proj, B, S)
    att = _diff_attention(proj, vt, slopes, lamp, nw_col, B, S, lambda_init)
    ssm = _ssd_mixer(proj, dt_raw, cw8, cb, _pad_lanes(dt_bias[l]), _pad_lanes(a_log[l]),
                     dskip_exp, ssm_nw, e_mat, B, S)
    h1, eid, ew = _out_proj(att, ssm, x2d, wa, ws, row(ln1_g[l]), row(ln1_b[l]), wr, br,
                            512 if T % 512 == 0 else 256)

    stril = (jnp.arange(POS_TB)[:, None] > jnp.arange(POS_TB)[None, :]).astype(BF16)
    dest, pend = _positions(eid, stril)
    nblk = (T * 2) // MOE_BM + N_EXPERTS
    cap = nblk * MOE_BM
    pad_ends = pend[0, :N_EXPERTS].astype(jnp.int32)
    blk_start = jnp.arange(nblk, dtype=jnp.int32) * MOE_BM
    nused = (pad_ends[N_EXPERTS - 1] // MOE_BM).astype(jnp.int32)
    last_used = jnp.maximum(nused - 1, 0) * MOE_BM
    bexp = jnp.sum(pad_ends[None, :] <= jnp.minimum(blk_start, last_used)[:, None], axis=1).astype(jnp.int32)
    bexp = jnp.minimum(bexp, N_EXPERTS - 1)
    dest3 = dest[:, :2].reshape(T // TOK_TB, 1, 2 * TOK_TB)
    counts = pend[1, :N_EXPERTS].astype(jnp.int32)
    padded = pad_ends - jnp.concatenate([jnp.zeros((1,), jnp.int32), pad_ends[:-1]])
    zstart = jnp.concatenate([pad_ends - padded + counts, nused.reshape(1)])
    xs_sorted = _dispatch(zstart, padded - counts, dest3, h1, cap)
    y_sorted = _expert_mlp(bexp, nused.reshape(1), xs_sorted,
                           w_gate[l].astype(BF16), w_up[l].astype(BF16), w_down[l].astype(BF16))
    out = _combine(dest3, h1, ew, row(ln2_g[l]), row(ln2_b[l]), y_sorted)
    return out.reshape(B, S, D)
```

```python
import functools
import math

import jax
import jax.numpy as jnp
from jax import lax
from jax.experimental import pallas as pl
from jax.experimental.pallas import tpu as pltpu

F32 = jnp.float32
BF16 = jnp.bfloat16

CHUNK = 64
ATT_HEADS = 8
ATT_HEAD_DIM = 64
ATT_V_DIM = 128
ATT_WIDTH = ATT_HEADS * ATT_V_DIM
ATT_VT_ROWS = ATT_V_DIM + 16
LOG2E = 1.4426950408889634
SSM_HEADS = 16
SSM_HEAD_DIM = 64
SSM_WIDTH = SSM_HEADS * SSM_HEAD_DIM
SSM_GROUPS = 2
SSM_STATE = 128
SSM_CONV = 4
N_Q = 1024
N_K = 1024
N_V = 1024
N_Z = 1024
N_XBC = SSM_WIDTH + 2 * SSM_GROUPS * SSM_STATE
N_DT = SSM_HEADS
N_MAIN = N_Q + N_K + N_V + N_Z + N_XBC
N_EXPERT_GROUPS = 4
EXPERTS_PER_GROUP = 8
N_EXPERTS = 32
EXPERT_HIDDEN = 1024
DEPTH = 1
DN_ALPHA = (2 * DEPTH) ** 0.25
LN_EPS = 1e-5
RMS_EPS = 1e-6
LANES = 128

VMEM_LIMIT = 56 * 1024 * 1024

INPROJ_TN = 1408
ATT_TQ = 512
SSD_L = 256
MOE_BM = 256
TOK_TB = 256
POS_TB = 512


def _cparams(sem, flags=None):
    return pltpu.CompilerParams(dimension_semantics=sem, vmem_limit_bytes=VMEM_LIMIT, flags=flags)


def _sigmoid(x):
    return 1.0 / (1.0 + jnp.exp(-x))


def _lane_col(x, idx):
    lane = lax.broadcasted_iota(jnp.int32, x.shape, 1)
    return jnp.sum(jnp.where(lane == idx, x, 0.0), axis=-1, keepdims=True)


def _inproj_kernel(x_ref, w_ref, wdt_ref, o_ref, dt_ref, xb_ref):
    @pl.when(pl.program_id(1) == 0)
    def _():
        xb = x_ref[...].astype(BF16)
        xb_ref[...] = xb
        dt_ref[...] = jnp.dot(xb, wdt_ref[...], preferred_element_type=F32)

    o_ref[...] = jnp.dot(xb_ref[...], w_ref[...], preferred_element_type=F32).astype(o_ref.dtype)


def _in_proj(x2d, w_main, w_dt, tm, tn):
    T, D = x2d.shape
    N = w_main.shape[1]
    return pl.pallas_call(
        _inproj_kernel,
        grid=(T // tm, N // tn),
        in_specs=[
            pl.BlockSpec((tm, D), lambda i, j: (i, 0)),
            pl.BlockSpec((D, tn), lambda i, j: (0, j)),
            pl.BlockSpec((D, LANES), lambda i, j: (0, 0)),
        ],
        out_specs=[
            pl.BlockSpec((tm, tn), lambda i, j: (i, j)),
            pl.BlockSpec((tm, LANES), lambda i, j: (i, 0)),
        ],
        out_shape=[
            jax.ShapeDtypeStruct((T, N), BF16),
            jax.ShapeDtypeStruct((T, LANES), F32),
        ],
        scratch_shapes=[pltpu.VMEM((tm, D), BF16)],
        compiler_params=_cparams(("arbitrary", "arbitrary")),
        name="in_proj",
    )(x2d, w_main, w_dt)


def _attn_kernel(slopes_ref, q_ref, k_ref, v_ref, lamp_ref, nw_ref, o_ref,
                 tab_ref, acc_ref, sa_ref, sb_ref, pa_ref, pb_ref, mxa_ref, mxb_ref, ala_ref, alb_ref,
                 qm_ref, vt_ref, *, lambda_init):
    h = pl.program_id(1)
    tq = tab_ref.shape[1]
    tk = tq
    nq = q_ref.shape[0] // tq
    slope2 = slopes_ref[h] * LOG2E

    s_rel = lax.broadcasted_iota(jnp.int32, (tk, tq), 0)
    t_rel = lax.broadcasted_iota(jnp.int32, (tk, tq), 1)
    tab_ref[0] = slope2 * s_rel.astype(F32)
    allowed = (s_rel // CHUNK) <= (t_rel // CHUNK)
    val = slope2 * (t_rel - jnp.abs(t_rel - s_rel)).astype(F32)
    tab_ref[1] = jnp.where(allowed, val, -jnp.inf)

    def prep_q(i, c):
        rows = pl.ds(pl.multiple_of(i * tq, tq), tq)
        q = q_ref[rows, :]
        lane = lax.broadcasted_iota(jnp.int32, q.shape, 1)
        qs = (q.astype(F32) * (ATT_HEAD_DIM ** -0.5 * LOG2E)).astype(BF16)
        zero = jnp.zeros_like(qs)
        qm_ref[0, rows, :] = jnp.where(lane < ATT_HEAD_DIM, qs, zero)
        qm_ref[1, rows, :] = jnp.where(lane >= ATT_HEAD_DIM, qs, zero)
        vt_ref[i, :ATT_V_DIM, :] = v_ref[rows, :].astype(F32).T.astype(BF16)
        extra_row = lax.broadcasted_iota(jnp.int32, (ATT_VT_ROWS - ATT_V_DIM, tq), 0)
        vt_ref[i, ATT_V_DIM:, :] = (extra_row == 0).astype(BF16)
        return c

    lax.fori_loop(0, nq, prep_q, 0)
    nt = (((1,), (1,)), ((), ()))
    lamp = lamp_ref[...]
    lam = (jnp.exp(jnp.sum(lamp[0:1] * lamp[1:2], axis=-1, keepdims=True))
           - jnp.exp(jnp.sum(lamp[2:3] * lamp[3:4], axis=-1, keepdims=True)) + lambda_init)

    half = tq // 2
    strip = 16

    def scores_piece(pair, s_out, mx_out, m, c):
        qi, j = pair
        cols = slice(c * half, (c + 1) * half)
        kb = k_ref[pl.ds(pl.multiple_of(j * tk, tk), tk), :]
        qh = qm_ref[m, pl.ds(pl.multiple_of(qi * tq + c * half, half), half), :]
        st = (lax.dot_general(kb, qh, nt, preferred_element_type=F32)
              + tab_ref[(j == qi).astype(jnp.int32), :, cols])
        s_out[m, :, cols] = st
        mx_out[m, :, cols] = jnp.max(st, axis=0, keepdims=True)

    def accum_piece(pair, p_in, al_in, m, c):
        cols = slice(c * half, (c + 1) * half)
        acc_ref[m, :, cols] = (al_in[m][:, cols] * acc_ref[m, :, cols]
                               + jnp.dot(vt_ref[pair[1]], p_in[m, :, cols], preferred_element_type=F32))

    def finalize(qi):
        a0 = acc_ref[0]
        a1 = acc_ref[1]
        dv = ATT_V_DIM
        o = a0[:dv] / a0[dv:dv + 1] - lam * (a1[:dv] / a1[dv:dv + 1])
        ms2 = jnp.mean(o * o, axis=0, keepdims=True)
        o = o * lax.rsqrt(ms2 + RMS_EPS) * nw_ref[...] * (1.0 - lambda_init)
        o_ref[pl.ds(pl.multiple_of(qi * tq, tq), tq), :] = o.T.astype(o_ref.dtype)

    def next_pair(pair):
        qi, j = pair
        wrap = j == qi
        return jnp.where(wrap, qi + 1, qi), jnp.where(wrap, 0, j + 1)

    def trip(cur_set, nxt_set, state):
        s_c, mx_c, p_c, al_c = cur_set
        s_n, mx_n, p_n, al_n = nxt_set
        prv, cur, m_old = state
        nxt = next_pair(cur)
        nxt_c = (jnp.minimum(nxt[0], nq - 1), jnp.where(nxt[0] >= nq, nq - 1, nxt[1]))
        prv_c = (prv[0], jnp.maximum(prv[1], 0))
        cj = -slope2 * ((cur[0] - cur[1]) * tk).astype(F32)
        refs, m_out = [], []
        for m in range(2):
            m_prev = jnp.where(cur[1] == 0, -jnp.inf, m_old[m])
            m_new = jnp.maximum(m_prev, mx_c[m] + cj)
            al_c[m] = jnp.exp2(m_prev - m_new)
            refs.append(m_new - cj)
            m_out.append(m_new)

        acc_p = [functools.partial(accum_piece, prv_c, p_n, al_n, m, c) for m in range(2) for c in range(2)]
        sco_p = [functools.partial(scores_piece, nxt_c, s_n, mx_n, m, c) for m in range(2) for c in range(2)]
        pieces = acc_p[:2] + sco_p + acc_p[2:]
        strips = [(m, r) for m in range(2) for r in range(tk // strip)]
        per = len(strips) // len(pieces)
        for g, piece in enumerate(pieces):
            piece()
            for m, r in strips[g * per:(g + 1) * per]:
                rows = slice(r * strip, (r + 1) * strip)
                p_c[m, rows, :] = jnp.exp2(s_c[m, rows, :] - refs[m]).astype(BF16)

        @pl.when(prv[1] == prv[0])
        def _():
            finalize(prv[0])

        return cur, nxt, tuple(m_out)

    set_a = (sa_ref, mxa_ref, pa_ref, ala_ref)
    set_b = (sb_ref, mxb_ref, pb_ref, alb_ref)
    acc_ref[...] = jnp.zeros_like(acc_ref)
    pb_ref[...] = jnp.zeros_like(pb_ref)
    alb_ref[...] = jnp.ones_like(alb_ref)
    zero_i = jnp.int32(0)
    for m in range(2):
        for c in range(2):
            scores_piece((zero_i, zero_i), sa_ref, mxa_ref, m, c)

    def body(t, state):
        return trip(set_b, set_a, trip(set_a, set_b, state))

    n_pairs = nq * (nq + 1) // 2
    m_init = jnp.full((1, tq), -jnp.inf, F32)
    first = ((zero_i, jnp.int32(-1)), (zero_i, zero_i), (m_init, m_init))
    state = lax.fori_loop(0, n_pairs // 2, body, first)
    if n_pairs % 2:
        trip(set_a, set_b, state)

    last_set = set_a if (n_pairs - 1) % 2 == 0 else set_b
    last_pair = (jnp.int32(nq - 1), jnp.int32(nq - 1))
    for m in range(2):
        for c in range(2):
            accum_piece(last_pair, last_set[2], last_set[3], m, c)
    finalize(last_pair[0])


def _diff_attention(proj, slopes, lamp, nw_col, B, S, lambda_init):
    T = B * S
    tq = ATT_TQ
    nq = S // tq
    kern = functools.partial(_attn_kernel, lambda_init=lambda_init)
    grid_spec = pltpu.PrefetchScalarGridSpec(
        num_scalar_prefetch=1,
        grid=(B, ATT_HEADS),
        in_specs=[
            pl.BlockSpec((S, LANES), lambda b, h, s: (b, h)),
            pl.BlockSpec((S, LANES), lambda b, h, s: (b, N_Q // LANES + h)),
            pl.BlockSpec((S, LANES), lambda b, h, s: (b, (N_Q + N_K) // LANES + h)),
            pl.BlockSpec((8, LANES), lambda b, h, s: (0, 0)),
            pl.BlockSpec((ATT_V_DIM, 1), lambda b, h, s: (0, 0)),
        ],
        out_specs=pl.BlockSpec((S, ATT_V_DIM), lambda b, h, s: (b, h)),
        scratch_shapes=[
            pltpu.VMEM((2, tq, tq), F32),
            pltpu.VMEM((2, ATT_VT_ROWS, tq), F32),
            pltpu.VMEM((2, tq, tq), F32), pltpu.VMEM((2, tq, tq), F32),
            pltpu.VMEM((2, tq, tq), BF16), pltpu.VMEM((2, tq, tq), BF16),
            pltpu.VMEM((2, 1, tq), F32), pltpu.VMEM((2, 1, tq), F32),
            pltpu.VMEM((2, 1, tq), F32), pltpu.VMEM((2, 1, tq), F32),
            pltpu.VMEM((2, S, LANES), BF16),
            pltpu.VMEM((nq, ATT_VT_ROWS, tq), BF16),
        ],
    )
    return pl.pallas_call(
        kern,
        grid_spec=grid_spec,
        out_shape=jax.ShapeDtypeStruct((T, ATT_WIDTH), BF16),
        compiler_params=_cparams(("arbitrary", "arbitrary")),
        name="diff_attention",
    )(slopes, proj, proj, proj, lamp, nw_col)


def _expand_heads(v, e):
    hi = v.astype(BF16)
    lo = (v - hi.astype(F32)).astype(BF16)
    return jnp.dot(hi, e, preferred_element_type=F32) + jnp.dot(lo, e, preferred_element_type=F32)


def _ssd_kernel(z_ref, xs_ref, b_ref, c_ref, dt_ref, cw_ref, cb_ref, dtb_ref, alog_ref,
                dskip_ref, nw_ref, e_ref, *rest, n_side):
    side_in = rest[:n_side]
    o_ref = rest[n_side]
    side_out = rest[n_side + 1:2 * n_side + 1]
    ext_ref, st_ref, y_ref = rest[2 * n_side + 1:]
    for w_in_ref, w_out_ref in zip(side_in, side_out):
        w_out_ref[...] = w_in_ref[...].astype(w_out_ref.dtype)
    blk = pl.program_id(1)
    L = z_ref.shape[0]
    nchunk = L // CHUNK
    gw = SSM_WIDTH // SSM_GROUPS
    hpg = SSM_HEADS // SSM_GROUPS

    @pl.when(blk == 0)
    def _():
        ext_ref[0:8, :] = jnp.zeros((8, N_XBC), F32)
        st_ref[...] = jnp.zeros_like(st_ref)

    cur = jnp.concatenate([xs_ref[...], b_ref[...], c_ref[...]], axis=1).astype(F32)
    ext_ref[8:, :] = cur
    cw = cw_ref[...]
    conv = cb_ref[...] + cw[3:4] * cur
    for j in range(SSM_CONV - 1):
        conv = conv + cw[j:j + 1] * ext_ref[pl.ds(8 - (SSM_CONV - 1) + j, L), :]
    ext_ref[0:8, :] = cur[L - 8:, :]
    xbc = conv * _sigmoid(conv)
    xs = xbc[:, :SSM_WIDTH]
    bmb = xbc[:, SSM_WIDTH:SSM_WIDTH + SSM_GROUPS * SSM_STATE].astype(BF16)
    cmb = xbc[:, SSM_WIDTH + SSM_GROUPS * SSM_STATE:].astype(BF16)
    xsb = xs.astype(BF16)

    lane1 = lax.broadcasted_iota(jnp.int32, (1, LANES), 1)
    dtx = dt_ref[...] + dtb_ref[...]
    dtp = jnp.maximum(dtx, 0.0) + jnp.log1p(jnp.exp(-jnp.abs(dtx)))
    a_head = jnp.where(lane1 < SSM_HEADS, -jnp.exp(alog_ref[...]), 0.0)
    acs = dtp * a_head
    row_in_chunk = lax.broadcasted_iota(jnp.int32, (L, LANES), 0) & (CHUNK - 1)
    k = 1
    while k < CHUNK:
        acs = acs + jnp.where(row_in_chunk >= k, pltpu.roll(acs, k, axis=0), 0.0)
        k *= 2
    acs_t = acs.T
    dt_t = dtp.T

    e = e_ref[...]
    acs_last = jnp.concatenate(
        [jnp.broadcast_to(acs[c * CHUNK + CHUNK - 1:(c + 1) * CHUNK, :], (CHUNK, LANES)) for c in range(nchunk)],
        axis=0)
    w_exp = _expand_heads(dtp * jnp.exp(acs_last - acs), e)
    od_exp = _expand_heads(jnp.exp(acs), e)
    row8 = lax.broadcasted_iota(jnp.int32, (8, LANES), 0)
    cd8 = jnp.zeros((8, LANES), F32)
    for c in range(nchunk):
        cd8 = jnp.where(row8 == c, jnp.exp(acs[c * CHUNK + CHUNK - 1:(c + 1) * CHUNK, :]), cd8)
    cd_exp = _expand_heads(cd8, e)
    xw = (xs * w_exp).astype(BF16)

    tn = (((0,), (0,)), ((), ()))
    for c in range(nchunk):
        r0 = c * CHUNK
        for g in range(SSM_GROUPS):
            st = st_ref[g]
            cg = cmb[r0:r0 + CHUNK, g * SSM_STATE:(g + 1) * SSM_STATE]
            bg = bmb[r0:r0 + CHUNK, g * SSM_STATE:(g + 1) * SSM_STATE]
            y_ref[r0:r0 + CHUNK, g * gw:(g + 1) * gw] = jnp.dot(cg, st.astype(BF16), preferred_element_type=F32)
            snew = lax.dot_general(bg, xw[r0:r0 + CHUNK, g * gw:(g + 1) * gw], tn, preferred_element_type=F32)
            st_ref[g] = st * cd_exp[c:c + 1, g * gw:(g + 1) * gw] + snew
    y = y_ref[...] * od_exp + xs * dskip_ref[...]

    pair = 2 * CHUNK
    li = lax.broadcasted_iota(jnp.int32, (pair, pair), 0)
    si = lax.broadcasted_iota(jnp.int32, (pair, pair), 1)
    mask2 = (li >= si) & ((si >= CHUNK) | (li < CHUNK))
    lanep = lax.broadcasted_iota(jnp.int32, (pair, LANES), 1)
    nt = (((1,), (1,)), ((), ()))
    for pp in range(L // pair):
        r0 = pp * pair
        acs_p = acs[r0:r0 + pair, :]
        for g in range(SSM_GROUPS):
            cb2 = lax.dot_general(cmb[r0:r0 + pair, g * SSM_STATE:(g + 1) * SSM_STATE],
                                  bmb[r0:r0 + pair, g * SSM_STATE:(g + 1) * SSM_STATE],
                                  nt, preferred_element_type=F32)
            for hh in range(hpg // 2):
                hp = g * (hpg // 2) + hh
                mats = []
                for u in range(2):
                    hd = 2 * hp + u
                    seg = _lane_col(acs_p, hd) - acs_t[hd:hd + 1, r0:r0 + pair]
                    decay = jnp.exp(jnp.where(mask2, seg, -jnp.inf))
                    mats.append((cb2 * decay * dt_t[hd:hd + 1, r0:r0 + pair]).astype(BF16))
                lhs = jnp.concatenate(mats, axis=1)
                xp = xsb[r0:r0 + pair, hp * LANES:(hp + 1) * LANES]
                zero = jnp.zeros_like(xp)
                rhs = jnp.concatenate([jnp.where(lanep < SSM_HEAD_DIM, xp, zero),
                                       jnp.where(lanep >= SSM_HEAD_DIM, xp, zero)], axis=0)
                y_ref[r0:r0 + pair, hp * LANES:(hp + 1) * LANES] = jnp.dot(lhs, rhs, preferred_element_type=F32)
    y = y + y_ref[...]

    z = z_ref[...].astype(F32)
    y = y * (z * _sigmoid(z))
    outs = []
    for g in range(SSM_GROUPS):
        yg = y[:, g * gw:(g + 1) * gw]
        outs.append(yg * lax.rsqrt(jnp.mean(yg * yg, axis=-1, keepdims=True) + RMS_EPS))
    o_ref[...] = (jnp.concatenate(outs, axis=1) * nw_ref[...]).astype(o_ref.dtype)


def _ssd_mixer(proj, dt_raw, cw8, cb, dtb, alog, dskip_exp, nw, e_mat, B, S, side=()):
    T = B * S
    L = SSD_L
    nb = S // L
    row = lambda b, i: b * nb + i
    col0 = (N_Q + N_K + N_V + N_Z)
    const = lambda shape: pl.BlockSpec(shape, lambda b, i: (0, 0))
    steps = B * nb
    assert all(w.shape[0] % (8 * steps) == 0 for w in side)
    side_specs = [pl.BlockSpec((w.shape[0] // steps, w.shape[1]), lambda b, i: (row(b, i), 0)) for w in side]
    return pl.pallas_call(
        functools.partial(_ssd_kernel, n_side=len(side)),
        grid=(B, nb),
        in_specs=[
            pl.BlockSpec((L, N_Z), lambda b, i: (row(b, i), (N_Q + N_K + N_V) // N_Z)),
            pl.BlockSpec((L, SSM_WIDTH), lambda b, i: (row(b, i), col0 // SSM_WIDTH)),
            pl.BlockSpec((L, 256), lambda b, i: (row(b, i), (col0 + SSM_WIDTH) // 256)),
            pl.BlockSpec((L, 256), lambda b, i: (row(b, i), (col0 + SSM_WIDTH + 256) // 256)),
            pl.BlockSpec((L, LANES), lambda b, i: (row(b, i), 0)),
            const((8, N_XBC)),
            const((1, N_XBC)),
            const((1, LANES)),
            const((1, LANES)),
            const((1, SSM_WIDTH)),
            const((1, SSM_WIDTH)),
            const((LANES, SSM_WIDTH)),
        ] + side_specs,
        out_specs=[pl.BlockSpec((L, SSM_WIDTH), lambda b, i: (row(b, i), 0))] + side_specs,
        out_shape=[jax.ShapeDtypeStruct((T, SSM_WIDTH), BF16)]
        + [jax.ShapeDtypeStruct(w.shape, BF16) for w in side],
        scratch_shapes=[
            pltpu.VMEM((L + 8, N_XBC), F32),
            pltpu.VMEM((SSM_GROUPS, SSM_STATE, SSM_WIDTH // SSM_GROUPS), F32),
            pltpu.VMEM((L, SSM_WIDTH), F32),
        ],
        compiler_params=_cparams(("arbitrary", "arbitrary")),
        name="ssd_mixer",
    )(proj, proj, proj, proj, dt_raw, cw8, cb, dtb, alog, dskip_exp, nw, e_mat, *side)


def _layer_norm_rows(r, g, b):
    mu = jnp.mean(r, axis=-1, keepdims=True)
    d = r - mu
    var = jnp.mean(d * d, axis=-1, keepdims=True)
    return d * lax.rsqrt(var + LN_EPS) * g + b


def _outproj_kernel(att_ref, ssm_ref, x_ref, wa_ref, ws_ref, g_ref, b_ref, wr_ref, br_ref,
                    h_ref, eid_ref, ew_ref):
    mix = (jnp.dot(att_ref[...], wa_ref[...], preferred_element_type=F32)
           + jnp.dot(ssm_ref[...], ws_ref[...], preferred_element_type=F32))
    h = _layer_norm_rows(DN_ALPHA * x_ref[...] + mix, g_ref[...], b_ref[...])
    h_ref[...] = h

    logits = jnp.dot(h.astype(BF16), wr_ref[...], preferred_element_type=F32) + br_ref[...]
    lane = lax.broadcasted_iota(jnp.int32, logits.shape, 1)
    lanef = lane.astype(F32)
    big = float(LANES)
    gl = jnp.where(lane < N_EXPERT_GROUPS, logits, -jnp.inf)
    gmax = jnp.max(gl, axis=-1, keepdims=True)
    g_prob = 1.0 / jnp.sum(jnp.exp(gl - gmax), axis=-1, keepdims=True)
    gidx = jnp.min(jnp.where(gl == gmax, lanef, big), axis=-1, keepdims=True)
    lo = N_EXPERT_GROUPS + EXPERTS_PER_GROUP * gidx
    el = jnp.where((lanef >= lo) & (lanef < lo + EXPERTS_PER_GROUP), logits, -jnp.inf)
    t1 = jnp.max(el, axis=-1, keepdims=True)
    i1 = jnp.min(jnp.where(el == t1, lanef, big), axis=-1, keepdims=True)
    el2 = jnp.where(lanef == i1, -jnp.inf, el)
    t2 = jnp.max(el2, axis=-1, keepdims=True)
    i2 = jnp.min(jnp.where(el2 == t2, lanef, big), axis=-1, keepdims=True)
    ex = jnp.exp(t2 - t1)
    w1 = g_prob / (1.0 + ex)
    w2 = g_prob * ex / (1.0 + ex)
    eid = jnp.where(lane == 0, i1 - N_EXPERT_GROUPS, jnp.where(lane == 1, i2 - N_EXPERT_GROUPS, 0.0))
    eid_ref[...] = eid.astype(jnp.int32)
    ew_ref[...] = jnp.where(lane == 0, w1, jnp.where(lane == 1, w2, 0.0))


def _out_proj(att, ssm, x2d, wa, ws, g, b, wr, br, tm):
    T, D = x2d.shape
    const = lambda shape: pl.BlockSpec(shape, lambda i: (0, 0))
    rows = lambda w: pl.BlockSpec((tm, w), lambda i: (i, 0))
    return pl.pallas_call(
        _outproj_kernel,
        grid=(T // tm,),
        in_specs=[rows(ATT_WIDTH), rows(SSM_WIDTH), rows(D), const((ATT_WIDTH, D)), const((SSM_WIDTH, D)),
                  const((1, D)), const((1, D)), const((D, LANES)), const((1, LANES))],
        out_specs=[rows(D), rows(LANES), rows(LANES)],
        out_shape=[jax.ShapeDtypeStruct((T, D), F32),
                   jax.ShapeDtypeStruct((T, LANES), jnp.int32),
                   jax.ShapeDtypeStruct((T, LANES), F32)],
        compiler_params=_cparams(("arbitrary",)),
        name="out_proj_ln1_router",
    )(att, ssm, x2d, wa, ws, g, b, wr, br)


def _pos_kernel(eid_ref, stril_ref, dest_ref, pend_ref, tot_ref, run_ref, pstart_ref):
    ph = pl.program_id(0)
    i = pl.program_id(1)
    tb = eid_ref.shape[0]
    lane = lax.broadcasted_iota(jnp.int32, (tb, LANES), 1)
    lanef = lane.astype(F32)
    ef = eid_ref[...].astype(F32)
    oh1 = (lanef == _lane_col(ef, 0)).astype(F32)
    oh2 = (lanef == _lane_col(ef, 1)).astype(F32)
    cnt = oh1 + oh2

    @pl.when((ph == 0) & (i == 0))
    def _():
        tot_ref[...] = jnp.zeros_like(tot_ref)

    @pl.when(ph == 0)
    def _():
        tot_ref[...] += jnp.sum(cnt, axis=0, keepdims=True)

    @pl.when((ph == 1) & (i == 0))
    def _():
        tot = jnp.broadcast_to(tot_ref[...], (8, LANES))
        padded = jnp.floor((tot + (MOE_BM - 1)) * (1.0 / MOE_BM)) * MOE_BM
        lane8 = lax.broadcasted_iota(jnp.int32, (8, LANES), 1)
        ends = padded
        k = 1
        while k < LANES:
            ends = ends + jnp.where(lane8 >= k, pltpu.roll(ends, k, axis=1), 0.0)
            k *= 2
        row8 = lax.broadcasted_iota(jnp.int32, (8, LANES), 0)
        pend_ref[...] = jnp.where(row8 == 0, ends, jnp.where(row8 == 1, tot, 0.0))
        pstart_ref[...] = (ends - padded)[0:1]
        run_ref[...] = jnp.zeros_like(run_ref)

    @pl.when(ph == 1)
    def _():
        pre = jnp.dot(stril_ref[...], cnt.astype(BF16), preferred_element_type=F32)
        slot = pstart_ref[...] + run_ref[...] + pre
        d1 = jnp.sum(oh1 * slot, axis=-1, keepdims=True)
        d2 = jnp.sum(oh2 * slot, axis=-1, keepdims=True)
        dest_ref[...] = jnp.where(lane == 0, d1, jnp.where(lane == 1, d2, 0.0)).astype(jnp.int32)
        run_ref[...] += jnp.sum(cnt, axis=0, keepdims=True)


def _positions(eid, stril):
    T = eid.shape[0]
    tb = POS_TB
    return pl.pallas_call(
        _pos_kernel,
        grid=(2, T // tb),
        in_specs=[pl.BlockSpec((tb, LANES), lambda p, i: (i, 0)),
                  pl.BlockSpec((tb, tb), lambda p, i: (0, 0))],
        out_specs=[pl.BlockSpec((tb, LANES), lambda p, i: (i * p, 0)),
                   pl.BlockSpec((8, LANES), lambda p, i: (0, 0))],
        out_shape=[jax.ShapeDtypeStruct((T, LANES), jnp.int32),
                   jax.ShapeDtypeStruct((8, LANES), F32)],
        scratch_shapes=[pltpu.VMEM((1, LANES), F32), pltpu.VMEM((1, LANES), F32), pltpu.VMEM((1, LANES), F32)],
        compiler_params=_cparams(("arbitrary", "arbitrary")),
        name="dispatch_positions",
    )(eid, stril)


def _row_copy(src, s, dst, d, sem):
    return pltpu.make_async_copy(src.at[pl.ds(s, 1)], dst.at[pl.ds(d, 1)], sem)


def _scatter_kernel(zstart_ref, zcnt_ref, dest_ref, h_ref, xs_hbm, zrow_ref, sem, zsem):
    tb = h_ref.shape[0]

    @pl.when(pl.program_id(0) == 0)
    def _():
        zrow_ref[...] = jnp.zeros_like(zrow_ref)

        def fill(e, c):
            lax.fori_loop(0, zcnt_ref[e],
                          lambda r, c2: (_row_copy(zrow_ref, 0, xs_hbm, zstart_ref[e] + r, zsem).start(), c2)[1], 0)
            return c

        def fill_wait(e, c):
            lax.fori_loop(0, zcnt_ref[e],
                          lambda r, c2: (_row_copy(zrow_ref, 0, xs_hbm, 0, zsem).wait(), c2)[1], 0)
            return c

        nblk = xs_hbm.shape[0] // MOE_BM
        first_free = zstart_ref[N_EXPERTS]

        def tail_copy(b):
            return pltpu.make_async_copy(zrow_ref, xs_hbm.at[pl.ds(b * MOE_BM, MOE_BM)], zsem)

        lax.fori_loop(0, N_EXPERTS, fill, 0)
        lax.fori_loop(first_free, nblk, lambda b, c: (tail_copy(b).start(), c)[1], 0)
        lax.fori_loop(0, N_EXPERTS, fill_wait, 0)
        lax.fori_loop(first_free, nblk, lambda b, c: (tail_copy(b).wait(), c)[1], 0)

    def issue(r, c):
        _row_copy(h_ref, r, xs_hbm, dest_ref[0, 0, 2 * r], sem).start(priority=0)
        _row_copy(h_ref, r, xs_hbm, dest_ref[0, 0, 2 * r + 1], sem).start(priority=1)
        return c

    def drain(r, c):
        _row_copy(h_ref, 0, xs_hbm, 0, sem).wait()
        _row_copy(h_ref, 0, xs_hbm, 0, sem).wait()
        return c

    lax.fori_loop(0, tb, issue, 0, unroll=8)
    lax.fori_loop(0, tb, drain, 0, unroll=8)


def _dispatch(zstart, zcnt, dest3, h, cap):
    T, D = h.shape
    nb = dest3.shape[0]
    tb = dest3.shape[2] // 2
    grid_spec = pltpu.PrefetchScalarGridSpec(
        num_scalar_prefetch=2,
        grid=(nb,),
        in_specs=[pl.BlockSpec((1, 1, 2 * tb), lambda i, zs, zc: (i, 0, 0), memory_space=pltpu.SMEM),
                  pl.BlockSpec((tb, D), lambda i, zs, zc: (i, 0))],
        out_specs=pl.BlockSpec(memory_space=pl.ANY),
        scratch_shapes=[pltpu.VMEM((MOE_BM, D), h.dtype), pltpu.SemaphoreType.DMA, pltpu.SemaphoreType.DMA],
    )
    return pl.pallas_call(
        _scatter_kernel,
        grid_spec=grid_spec,
        out_shape=jax.ShapeDtypeStruct((cap, D), h.dtype),
        compiler_params=_cparams(("arbitrary",)),
        name="moe_dispatch",
    )(zstart, zcnt, dest3, h)


def _expert_kernel(bexp_ref, nused_ref, x_ref, wg_ref, wu_ref, wd_ref, y_ref):
    i = pl.program_id(0)

    @pl.when(i < nused_ref[0])
    def _():
        x = x_ref[...].astype(BF16)
        gate = jnp.dot(x, wg_ref[0], preferred_element_type=F32)
        up = jnp.dot(x, wu_ref[0], preferred_element_type=F32)
        hid = (gate * _sigmoid(gate) * up).astype(BF16)
        y_ref[...] = jnp.dot(hid, wd_ref[0], preferred_element_type=F32)

    @pl.when(i >= nused_ref[0])
    def _():
        y_ref[...] = jnp.zeros_like(y_ref)


def _expert_mlp(bexp, nused, xs, wg, wu, wd):
    cap, D = xs.shape
    bm = MOE_BM
    H = wg.shape[2]
    grid_spec = pltpu.PrefetchScalarGridSpec(
        num_scalar_prefetch=2,
        grid=(cap // bm,),
        in_specs=[
            pl.BlockSpec((bm, D), lambda i, be, nu: (i, 0)),
            pl.BlockSpec((1, D, H), lambda i, be, nu: (be[i], 0, 0)),
            pl.BlockSpec((1, D, H), lambda i, be, nu: (be[i], 0, 0)),
            pl.BlockSpec((1, H, D), lambda i, be, nu: (be[i], 0, 0)),
        ],
        out_specs=pl.BlockSpec((bm, D), lambda i, be, nu: (i, 0)),
    )
    return pl.pallas_call(
        _expert_kernel,
        grid_spec=grid_spec,
        out_shape=jax.ShapeDtypeStruct((cap, D), F32),
        compiler_params=_cparams(("arbitrary",)),
        name="expert_mlp",
    )(bexp, nused, xs, wg, wu, wd)


def _combine_kernel(dest_ref, dnext_ref, h_ref, ew_ref, g_ref, b_ref, y_hbm, o_ref, ybuf, sems):
    i = pl.program_id(0)
    n = pl.num_programs(0)
    tb = h_ref.shape[0]
    slot = i & 1

    def gather(d_ref, s):
        def issue(r, c):
            _row_copy(y_hbm, d_ref[0, 0, 2 * r], ybuf.at[s, 0], r, sems.at[s]).start(priority=0)
            _row_copy(y_hbm, d_ref[0, 0, 2 * r + 1], ybuf.at[s, 1], r, sems.at[s]).start(priority=1)
            return c
        lax.fori_loop(0, tb, issue, 0, unroll=8)

    @pl.when(i == 0)
    def _():
        gather(dest_ref, slot)

    @pl.when(i + 1 < n)
    def _():
        gather(dnext_ref, 1 - slot)

    def drain(r, c):
        _row_copy(y_hbm, 0, ybuf.at[slot, 0], 0, sems.at[slot]).wait()
        _row_copy(y_hbm, 0, ybuf.at[slot, 1], 0, sems.at[slot]).wait()
        return c

    lax.fori_loop(0, tb, drain, 0, unroll=8)
    ew = ew_ref[...]
    ffn = _lane_col(ew, 0) * ybuf[slot, 0] + _lane_col(ew, 1) * ybuf[slot, 1]
    o_ref[...] = _layer_norm_rows(DN_ALPHA * h_ref[...] + ffn, g_ref[...], b_ref[...])


def _combine(dest3, h, ew, g, b, y):
    T, D = h.shape
    tb = dest3.shape[2] // 2
    nb = T // tb
    return pl.pallas_call(
        _combine_kernel,
        grid=(nb,),
        in_specs=[pl.BlockSpec((1, 1, 2 * tb), lambda i: (i, 0, 0), memory_space=pltpu.SMEM),
                  pl.BlockSpec((1, 1, 2 * tb), lambda i: (jnp.minimum(i + 1, nb - 1), 0, 0),
                               memory_space=pltpu.SMEM),
                  pl.BlockSpec((tb, D), lambda i: (i, 0)),
                  pl.BlockSpec((tb, LANES), lambda i: (i, 0)),
                  pl.BlockSpec((1, D), lambda i: (0, 0)),
                  pl.BlockSpec((1, D), lambda i: (0, 0)),
                  pl.BlockSpec(memory_space=pl.ANY)],
        out_specs=pl.BlockSpec((tb, D), lambda i: (i, 0)),
        out_shape=jax.ShapeDtypeStruct((T, D), F32),
        scratch_shapes=[pltpu.VMEM((2, 2, tb, D), F32), pltpu.SemaphoreType.DMA((2,))],
        compiler_params=_cparams(("arbitrary",)),
        name="moe_combine_ln2",
    )(dest3, dest3, h, ew, g, b, y)


def _pad_lanes(v, n=LANES):
    v = v.reshape(1, -1).astype(F32)
    return jnp.pad(v, ((0, 0), (0, n - v.shape[1])))


def kernel(x, w_in, lambda_q1, lambda_k1, lambda_q2, lambda_k2, attn_norm_w, conv_w, conv_b, dt_bias, a_log, d_skip, ssm_norm_w, w_out, ln1_g, ln1_b, w_router_group, b_router_group, w_router_expert, b_router_expert, w_gate, w_up, w_down, ln2_g, ln2_b):
    B, S, D = x.shape
    T = B * S
    assert w_in.shape[0] == DEPTH == 1
    assert S % ATT_TQ == 0 and S % SSD_L == 0 and T % POS_TB == 0 and T % TOK_TB == 0
    l = 0
    lambda_init = 0.8 - 0.6 * math.exp(-0.3 * l)
    x2d = x.reshape(T, D)

    w_main = w_in[l][:, :N_MAIN].astype(BF16)
    w_dt = jnp.pad(w_in[l][:, N_MAIN:], ((0, 0), (0, LANES - N_DT))).astype(BF16)
    slopes = jnp.exp2(-8.0 * jnp.arange(1, ATT_HEADS + 1, dtype=F32) / ATT_HEADS)
    lamp = jnp.concatenate([_pad_lanes(lambda_q1[l]), _pad_lanes(lambda_k1[l]),
                            _pad_lanes(lambda_q2[l]), _pad_lanes(lambda_k2[l]),
                            jnp.zeros((4, LANES), F32)], axis=0)
    nw_col = attn_norm_w[l].astype(F32).reshape(ATT_V_DIM, 1)
    cw8 = jnp.pad(conv_w[l].astype(F32), ((0, 8 - SSM_CONV), (0, 0)))
    cb = conv_b[l].astype(F32).reshape(1, N_XBC)
    dskip_exp = jnp.repeat(d_skip[l].astype(F32), SSM_HEAD_DIM).reshape(1, SSM_WIDTH)
    ssm_nw = ssm_norm_w[l].astype(F32).reshape(1, SSM_WIDTH)
    head_of_lane = jnp.arange(SSM_WIDTH, dtype=jnp.int32) // SSM_HEAD_DIM
    e_mat = (jnp.arange(LANES, dtype=jnp.int32)[:, None] == head_of_lane[None, :]).astype(BF16)
    wa = w_out[l][:ATT_WIDTH].astype(BF16)
    ws = w_out[l][ATT_WIDTH:].astype(BF16)
    wr = jnp.concatenate(
        [w_router_group[l], jnp.transpose(w_router_expert[l], (1, 0, 2)).reshape(D, N_EXPERTS)], axis=1)
    wr = jnp.pad(wr, ((0, 0), (0, LANES - wr.shape[1]))).astype(BF16)
    br = _pad_lanes(jnp.concatenate([b_router_group[l], b_router_expert[l].reshape(-1)]))
    row = lambda v: v.astype(F32).reshape(1, D)

    tm_in = 1024 if T % 1024 == 0 else 256
    proj, dt_raw = _in_proj(x2d, w_main, w_dt, tm_in, INPROJ_TN)
    att = _diff_attention(proj, slopes, lamp, nw_col, B, S, lambda_init)
    experts_f32 = [w[l].astype(F32).reshape(-1, w.shape[-1]) for w in (w_gate, w_up, w_down)]
    ssm, wg_b, wu_b, wd_b = _ssd_mixer(proj, dt_raw, cw8, cb, _pad_lanes(dt_bias[l]), _pad_lanes(a_log[l]),
                                       dskip_exp, ssm_nw, e_mat, B, S, side=experts_f32)
    wg_b, wu_b, wd_b = (w.reshape(src.shape[1:]) for w, src in zip((wg_b, wu_b, wd_b), (w_gate, w_up, w_down)))
    h1, eid, ew = _out_proj(att, ssm, x2d, wa, ws, row(ln1_g[l]), row(ln1_b[l]), wr, br, 256)

    stril = (jnp.arange(POS_TB)[:, None] > jnp.arange(POS_TB)[None, :]).astype(BF16)
    dest, pend = _positions(eid, stril)
    nblk = (T * 2) // MOE_BM + N_EXPERTS
    cap = nblk * MOE_BM
    pad_ends = pend[0, :N_EXPERTS].astype(jnp.int32)
    blk_start = jnp.arange(nblk, dtype=jnp.int32) * MOE_BM
    nused = (pad_ends[N_EXPERTS - 1] // MOE_BM).astype(jnp.int32)
    last_used = jnp.maximum(nused - 1, 0) * MOE_BM
    bexp = jnp.sum(pad_ends[None, :] <= jnp.minimum(blk_start, last_used)[:, None], axis=1).astype(jnp.int32)
    bexp = jnp.minimum(bexp, N_EXPERTS - 1)
    dest3 = dest[:, :2].reshape(T // TOK_TB, 1, 2 * TOK_TB)
    counts = pend[1, :N_EXPERTS].astype(jnp.int32)
    padded = pad_ends - jnp.concatenate([jnp.zeros((1,), jnp.int32), pad_ends[:-1]])
    zstart = jnp.concatenate([pad_ends - padded + counts, nused.reshape(1)])
    xs_sorted = _dispatch(zstart, padded - counts, dest3, h1, cap)
    y_sorted = _expert_mlp(bexp, nused.reshape(1), xs_sorted, wg_b, wu_b, wd_b)
    out = _combine(dest3, h1, ew, row(ln2_g[l]), row(ln2_b[l]), y_sorted)
    return out.reshape(B, S, D)
```

```python
import functools
import math

import jax
import jax.numpy as jnp
from jax import lax
from jax.experimental import pallas as pl
from jax.experimental.pallas import tpu as pltpu

F32 = jnp.float32
BF16 = jnp.bfloat16

CHUNK = 64
ATT_HEADS = 8
ATT_HEAD_DIM = 64
ATT_V_DIM = 128
ATT_WIDTH = ATT_HEADS * ATT_V_DIM
ATT_VT_ROWS = ATT_V_DIM + 16
LOG2E = 1.4426950408889634
SSM_HEADS = 16
SSM_HEAD_DIM = 64
SSM_WIDTH = SSM_HEADS * SSM_HEAD_DIM
SSM_GROUPS = 2
SSM_STATE = 128
SSM_CONV = 4
N_Q = 1024
N_K = 1024
N_V = 1024
N_Z = 1024
N_XBC = SSM_WIDTH + 2 * SSM_GROUPS * SSM_STATE
N_DT = SSM_HEADS
N_MAIN = N_Q + N_K + N_V + N_Z + N_XBC
N_EXPERT_GROUPS = 4
EXPERTS_PER_GROUP = 8
N_EXPERTS = 32
EXPERT_HIDDEN = 1024
DEPTH = 1
DN_ALPHA = (2 * DEPTH) ** 0.25
LN_EPS = 1e-5
RMS_EPS = 1e-6
LANES = 128

VMEM_LIMIT = 56 * 1024 * 1024

INPROJ_TM = 512
INPROJ_TN = 2816
OUTPROJ_TM = 512
OUTPROJ_CHUNK = 256
ATT_TQ = 512
SSD_L = 256
MOE_BM = 256
TOK_TB = 256
POS_TB = 512


def _cparams(sem, flags=None):
    return pltpu.CompilerParams(dimension_semantics=sem, vmem_limit_bytes=VMEM_LIMIT, flags=flags)


def _sigmoid(x):
    return 1.0 / (1.0 + jnp.exp(-x))


def _lane_col(x, idx):
    lane = lax.broadcasted_iota(jnp.int32, x.shape, 1)
    return jnp.sum(jnp.where(lane == idx, x, 0.0), axis=-1, keepdims=True)


def _inproj_kernel(x_ref, w_ref, wdt_ref, o_ref, dt_ref, xb_ref):
    @pl.when(pl.program_id(1) == 0)
    def _():
        xb = x_ref[...].astype(BF16)
        xb_ref[...] = xb
        dt_ref[...] = jnp.dot(xb, wdt_ref[...], preferred_element_type=F32)

    o_ref[...] = jnp.dot(xb_ref[...], w_ref[...], preferred_element_type=F32).astype(o_ref.dtype)


def _in_proj(x2d, w_main, w_dt, tm, tn):
    T, D = x2d.shape
    N = w_main.shape[1]
    return pl.pallas_call(
        _inproj_kernel,
        grid=(T // tm, N // tn),
        in_specs=[
            pl.BlockSpec((tm, D), lambda i, j: (i, 0)),
            pl.BlockSpec((D, tn), lambda i, j: (0, j)),
            pl.BlockSpec((D, LANES), lambda i, j: (0, 0)),
        ],
        out_specs=[
            pl.BlockSpec((tm, tn), lambda i, j: (i, j)),
            pl.BlockSpec((tm, LANES), lambda i, j: (i, 0)),
        ],
        out_shape=[
            jax.ShapeDtypeStruct((T, N), BF16),
            jax.ShapeDtypeStruct((T, LANES), F32),
        ],
        scratch_shapes=[pltpu.VMEM((tm, D), BF16)],
        compiler_params=_cparams(("arbitrary", "arbitrary")),
        name="in_proj",
    )(x2d, w_main, w_dt)


def _attn_kernel(slopes_ref, q_ref, k_ref, v_ref, lamp_ref, nw_ref, o_ref,
                 tab_ref, acc_ref, sa_ref, sb_ref, pa_ref, pb_ref, mxa_ref, mxb_ref, ala_ref, alb_ref,
                 qm_ref, vt_ref, *, lambda_init):
    h = pl.program_id(1)
    tq = tab_ref.shape[1]
    tk = tq
    nq = q_ref.shape[0] // tq
    slope2 = slopes_ref[h] * LOG2E

    s_rel = lax.broadcasted_iota(jnp.int32, (tk, tq), 0)
    t_rel = lax.broadcasted_iota(jnp.int32, (tk, tq), 1)
    tab_ref[0] = slope2 * s_rel.astype(F32)
    allowed = (s_rel // CHUNK) <= (t_rel // CHUNK)
    val = slope2 * (t_rel - jnp.abs(t_rel - s_rel)).astype(F32)
    tab_ref[1] = jnp.where(allowed, val, -jnp.inf)

    def prep_q(i, c):
        rows = pl.ds(pl.multiple_of(i * tq, tq), tq)
        q = q_ref[rows, :]
        lane = lax.broadcasted_iota(jnp.int32, q.shape, 1)
        qs = (q.astype(F32) * (ATT_HEAD_DIM ** -0.5 * LOG2E)).astype(BF16)
        zero = jnp.zeros_like(qs)
        qm_ref[0, rows, :] = jnp.where(lane < ATT_HEAD_DIM, qs, zero)
        qm_ref[1, rows, :] = jnp.where(lane >= ATT_HEAD_DIM, qs, zero)
        vt_ref[i, :ATT_V_DIM, :] = v_ref[rows, :].astype(F32).T.astype(BF16)
        extra_row = lax.broadcasted_iota(jnp.int32, (ATT_VT_ROWS - ATT_V_DIM, tq), 0)
        vt_ref[i, ATT_V_DIM:, :] = (extra_row == 0).astype(BF16)
        return c

    lax.fori_loop(0, nq, prep_q, 0)
    nt = (((1,), (1,)), ((), ()))
    lamp = lamp_ref[...]
    lam = (jnp.exp(jnp.sum(lamp[0:1] * lamp[1:2], axis=-1, keepdims=True))
           - jnp.exp(jnp.sum(lamp[2:3] * lamp[3:4], axis=-1, keepdims=True)) + lambda_init)

    half = tq // 2
    strip = 16

    def scores_piece(pair, s_out, mx_out, m, c):
        qi, j = pair
        cols = slice(c * half, (c + 1) * half)
        kb = k_ref[pl.ds(pl.multiple_of(j * tk, tk), tk), :]
        qh = qm_ref[m, pl.ds(pl.multiple_of(qi * tq + c * half, half), half), :]
        st = (lax.dot_general(kb, qh, nt, preferred_element_type=F32)
              + tab_ref[(j == qi).astype(jnp.int32), :, cols])
        s_out[m, :, cols] = st
        mx_out[m, :, cols] = jnp.max(st, axis=0, keepdims=True)

    def accum_piece(pair, p_in, al_in, m, c):
        cols = slice(c * half, (c + 1) * half)
        acc_ref[m, :, cols] = (al_in[m][:, cols] * acc_ref[m, :, cols]
                               + jnp.dot(vt_ref[pair[1]], p_in[m, :, cols], preferred_element_type=F32))

    def finalize(qi):
        a0 = acc_ref[0]
        a1 = acc_ref[1]
        dv = ATT_V_DIM
        o = a0[:dv] / a0[dv:dv + 1] - lam * (a1[:dv] / a1[dv:dv + 1])
        ms2 = jnp.mean(o * o, axis=0, keepdims=True)
        o = o * lax.rsqrt(ms2 + RMS_EPS) * nw_ref[...] * (1.0 - lambda_init)
        o_ref[pl.ds(pl.multiple_of(qi * tq, tq), tq), :] = o.T.astype(o_ref.dtype)

    def next_pair(pair):
        qi, j = pair
        wrap = j == qi
        return jnp.where(wrap, qi + 1, qi), jnp.where(wrap, 0, j + 1)

    def trip(cur_set, nxt_set, state):
        s_c, mx_c, p_c, al_c = cur_set
        s_n, mx_n, p_n, al_n = nxt_set
        prv, cur, m_old = state
        nxt = next_pair(cur)
        nxt_c = (jnp.minimum(nxt[0], nq - 1), jnp.where(nxt[0] >= nq, nq - 1, nxt[1]))
        prv_c = (prv[0], jnp.maximum(prv[1], 0))
        cj = -slope2 * ((cur[0] - cur[1]) * tk).astype(F32)
        refs, m_out = [], []
        for m in range(2):
            m_prev = jnp.where(cur[1] == 0, -jnp.inf, m_old[m])
            m_new = jnp.maximum(m_prev, mx_c[m] + cj)
            al_c[m] = jnp.exp2(m_prev - m_new)
            refs.append(m_new - cj)
            m_out.append(m_new)

        acc_p = [functools.partial(accum_piece, prv_c, p_n, al_n, m, c) for m in range(2) for c in range(2)]
        sco_p = [functools.partial(scores_piece, nxt_c, s_n, mx_n, m, c) for m in range(2) for c in range(2)]
        pieces = acc_p[:2] + sco_p + acc_p[2:]
        strips = [(m, r) for m in range(2) for r in range(tk // strip)]
        per = len(strips) // len(pieces)
        for g, piece in enumerate(pieces):
            piece()
            for m, r in strips[g * per:(g + 1) * per]:
                rows = slice(r * strip, (r + 1) * strip)
                p_c[m, rows, :] = jnp.exp2(s_c[m, rows, :] - refs[m]).astype(BF16)

        @pl.when(prv[1] == prv[0])
        def _():
            finalize(prv[0])

        return cur, nxt, tuple(m_out)

    set_a = (sa_ref, mxa_ref, pa_ref, ala_ref)
    set_b = (sb_ref, mxb_ref, pb_ref, alb_ref)
    acc_ref[...] = jnp.zeros_like(acc_ref)
    pb_ref[...] = jnp.zeros_like(pb_ref)
    alb_ref[...] = jnp.ones_like(alb_ref)
    zero_i = jnp.int32(0)
    for m in range(2):
        for c in range(2):
            scores_piece((zero_i, zero_i), sa_ref, mxa_ref, m, c)

    def body(t, state):
        return trip(set_b, set_a, trip(set_a, set_b, state))

    n_pairs = nq * (nq + 1) // 2
    m_init = jnp.full((1, tq), -jnp.inf, F32)
    first = ((zero_i, jnp.int32(-1)), (zero_i, zero_i), (m_init, m_init))
    state = lax.fori_loop(0, n_pairs // 2, body, first)
    if n_pairs % 2:
        trip(set_a, set_b, state)

    last_set = set_a if (n_pairs - 1) % 2 == 0 else set_b
    last_pair = (jnp.int32(nq - 1), jnp.int32(nq - 1))
    for m in range(2):
        for c in range(2):
            accum_piece(last_pair, last_set[2], last_set[3], m, c)
    finalize(last_pair[0])


def _diff_attention(proj, slopes, lamp, nw_col, B, S, lambda_init):
    T = B * S
    tq = ATT_TQ
    nq = S // tq
    kern = functools.partial(_attn_kernel, lambda_init=lambda_init)
    grid_spec = pltpu.PrefetchScalarGridSpec(
        num_scalar_prefetch=1,
        grid=(B, ATT_HEADS),
        in_specs=[
            pl.BlockSpec((S, LANES), lambda b, h, s: (b, h)),
            pl.BlockSpec((S, LANES), lambda b, h, s: (b, N_Q // LANES + h)),
            pl.BlockSpec((S, LANES), lambda b, h, s: (b, (N_Q + N_K) // LANES + h)),
            pl.BlockSpec((8, LANES), lambda b, h, s: (0, 0)),
            pl.BlockSpec((ATT_V_DIM, 1), lambda b, h, s: (0, 0)),
        ],
        out_specs=pl.BlockSpec((S, ATT_V_DIM), lambda b, h, s: (b, h)),
        scratch_shapes=[
            pltpu.VMEM((2, tq, tq), F32),
            pltpu.VMEM((2, ATT_VT_ROWS, tq), F32),
            pltpu.VMEM((2, tq, tq), F32), pltpu.VMEM((2, tq, tq), F32),
            pltpu.VMEM((2, tq, tq), BF16), pltpu.VMEM((2, tq, tq), BF16),
            pltpu.VMEM((2, 1, tq), F32), pltpu.VMEM((2, 1, tq), F32),
            pltpu.VMEM((2, 1, tq), F32), pltpu.VMEM((2, 1, tq), F32),
            pltpu.VMEM((2, S, LANES), BF16),
            pltpu.VMEM((nq, ATT_VT_ROWS, tq), BF16),
        ],
    )
    return pl.pallas_call(
        kern,
        grid_spec=grid_spec,
        out_shape=jax.ShapeDtypeStruct((T, ATT_WIDTH), BF16),
        compiler_params=_cparams(("arbitrary", "arbitrary")),
        name="diff_attention",
    )(slopes, proj, proj, proj, lamp, nw_col)


def _expand_heads(v, e):
    hi = v.astype(BF16)
    lo = (v - hi.astype(F32)).astype(BF16)
    return jnp.dot(hi, e, preferred_element_type=F32) + jnp.dot(lo, e, preferred_element_type=F32)


def _ssd_kernel(z_ref, xs_ref, b_ref, c_ref, dt_ref, cw_ref, cb_ref, dtb_ref, alog_ref,
                dskip_ref, nw_ref, e_ref, *rest, n_side):
    side_in = rest[:n_side]
    o_ref = rest[n_side]
    side_out = rest[n_side + 1:2 * n_side + 1]
    ext_ref, st_ref, y_ref = rest[2 * n_side + 1:]
    for w_in_ref, w_out_ref in zip(side_in, side_out):
        w_out_ref[...] = w_in_ref[...].astype(w_out_ref.dtype)
    blk = pl.program_id(1)
    L = z_ref.shape[0]
    nchunk = L // CHUNK
    gw = SSM_WIDTH // SSM_GROUPS
    hpg = SSM_HEADS // SSM_GROUPS

    @pl.when(blk == 0)
    def _():
        ext_ref[0:8, :] = jnp.zeros((8, N_XBC), F32)
        st_ref[...] = jnp.zeros_like(st_ref)

    cur = jnp.concatenate([xs_ref[...], b_ref[...], c_ref[...]], axis=1).astype(F32)
    ext_ref[8:, :] = cur
    cw = cw_ref[...]
    conv = cb_ref[...] + cw[3:4] * cur
    for j in range(SSM_CONV - 1):
        conv = conv + cw[j:j + 1] * ext_ref[pl.ds(8 - (SSM_CONV - 1) + j, L), :]
    ext_ref[0:8, :] = cur[L - 8:, :]
    xbc = conv * _sigmoid(conv)
    xs = xbc[:, :SSM_WIDTH]
    bmb = xbc[:, SSM_WIDTH:SSM_WIDTH + SSM_GROUPS * SSM_STATE].astype(BF16)
    cmb = xbc[:, SSM_WIDTH + SSM_GROUPS * SSM_STATE:].astype(BF16)
    xsb = xs.astype(BF16)

    lane1 = lax.broadcasted_iota(jnp.int32, (1, LANES), 1)
    dtx = dt_ref[...] + dtb_ref[...]
    dtp = jnp.maximum(dtx, 0.0) + jnp.log1p(jnp.exp(-jnp.abs(dtx)))
    a_head = jnp.where(lane1 < SSM_HEADS, -jnp.exp(alog_ref[...]), 0.0)
    acs = dtp * a_head
    row_in_chunk = lax.broadcasted_iota(jnp.int32, (L, LANES), 0) & (CHUNK - 1)
    k = 1
    while k < CHUNK:
        acs = acs + jnp.where(row_in_chunk >= k, pltpu.roll(acs, k, axis=0), 0.0)
        k *= 2
    acs_t = acs.T
    dt_t = dtp.T

    e = e_ref[...]
    acs_last = jnp.concatenate(
        [jnp.broadcast_to(acs[c * CHUNK + CHUNK - 1:(c + 1) * CHUNK, :], (CHUNK, LANES)) for c in range(nchunk)],
        axis=0)
    w_exp = _expand_heads(dtp * jnp.exp(acs_last - acs), e)
    od_exp = _expand_heads(jnp.exp(acs), e)
    row8 = lax.broadcasted_iota(jnp.int32, (8, LANES), 0)
    cd8 = jnp.zeros((8, LANES), F32)
    for c in range(nchunk):
        cd8 = jnp.where(row8 == c, jnp.exp(acs[c * CHUNK + CHUNK - 1:(c + 1) * CHUNK, :]), cd8)
    cd_exp = _expand_heads(cd8, e)
    xw = (xs * w_exp).astype(BF16)

    tn = (((0,), (0,)), ((), ()))
    for c in range(nchunk):
        r0 = c * CHUNK
        for g in range(SSM_GROUPS):
            st = st_ref[g]
            cg = cmb[r0:r0 + CHUNK, g * SSM_STATE:(g + 1) * SSM_STATE]
            bg = bmb[r0:r0 + CHUNK, g * SSM_STATE:(g + 1) * SSM_STATE]
            y_ref[r0:r0 + CHUNK, g * gw:(g + 1) * gw] = jnp.dot(cg, st.astype(BF16), preferred_element_type=F32)
            snew = lax.dot_general(bg, xw[r0:r0 + CHUNK, g * gw:(g + 1) * gw], tn, preferred_element_type=F32)
            st_ref[g] = st * cd_exp[c:c + 1, g * gw:(g + 1) * gw] + snew
    y = y_ref[...] * od_exp + xs * dskip_ref[...]

    pair = 2 * CHUNK
    li = lax.broadcasted_iota(jnp.int32, (pair, pair), 0)
    si = lax.broadcasted_iota(jnp.int32, (pair, pair), 1)
    mask2 = (li >= si) & ((si >= CHUNK) | (li < CHUNK))
    lanep = lax.broadcasted_iota(jnp.int32, (pair, LANES), 1)
    nt = (((1,), (1,)), ((), ()))
    for pp in range(L // pair):
        r0 = pp * pair
        acs_p = acs[r0:r0 + pair, :]
        for g in range(SSM_GROUPS):
            cb2 = lax.dot_general(cmb[r0:r0 + pair, g * SSM_STATE:(g + 1) * SSM_STATE],
                                  bmb[r0:r0 + pair, g * SSM_STATE:(g + 1) * SSM_STATE],
                                  nt, preferred_element_type=F32)
            for hh in range(hpg // 2):
                hp = g * (hpg // 2) + hh
                mats = []
                for u in range(2):
                    hd = 2 * hp + u
                    seg = _lane_col(acs_p, hd) - acs_t[hd:hd + 1, r0:r0 + pair]
                    decay = jnp.exp(jnp.where(mask2, seg, -jnp.inf))
                    mats.append((cb2 * decay * dt_t[hd:hd + 1, r0:r0 + pair]).astype(BF16))
                lhs = jnp.concatenate(mats, axis=1)
                xp = xsb[r0:r0 + pair, hp * LANES:(hp + 1) * LANES]
                zero = jnp.zeros_like(xp)
                rhs = jnp.concatenate([jnp.where(lanep < SSM_HEAD_DIM, xp, zero),
                                       jnp.where(lanep >= SSM_HEAD_DIM, xp, zero)], axis=0)
                y_ref[r0:r0 + pair, hp * LANES:(hp + 1) * LANES] = jnp.dot(lhs, rhs, preferred_element_type=F32)
    y = y + y_ref[...]

    z = z_ref[...].astype(F32)
    y = y * (z * _sigmoid(z))
    outs = []
    for g in range(SSM_GROUPS):
        yg = y[:, g * gw:(g + 1) * gw]
        outs.append(yg * lax.rsqrt(jnp.mean(yg * yg, axis=-1, keepdims=True) + RMS_EPS))
    o_ref[...] = (jnp.concatenate(outs, axis=1) * nw_ref[...]).astype(o_ref.dtype)


def _ssd_mixer(proj, dt_raw, cw8, cb, dtb, alog, dskip_exp, nw, e_mat, B, S, side=()):
    T = B * S
    L = SSD_L
    nb = S // L
    row = lambda b, i: b * nb + i
    col0 = (N_Q + N_K + N_V + N_Z)
    const = lambda shape: pl.BlockSpec(shape, lambda b, i: (0, 0))
    steps = B * nb
    assert all(w.shape[0] % (8 * steps) == 0 for w in side)
    side_specs = [pl.BlockSpec((w.shape[0] // steps, w.shape[1]), lambda b, i: (row(b, i), 0)) for w in side]
    return pl.pallas_call(
        functools.partial(_ssd_kernel, n_side=len(side)),
        grid=(B, nb),
        in_specs=[
            pl.BlockSpec((L, N_Z), lambda b, i: (row(b, i), (N_Q + N_K + N_V) // N_Z)),
            pl.BlockSpec((L, SSM_WIDTH), lambda b, i: (row(b, i), col0 // SSM_WIDTH)),
            pl.BlockSpec((L, 256), lambda b, i: (row(b, i), (col0 + SSM_WIDTH) // 256)),
            pl.BlockSpec((L, 256), lambda b, i: (row(b, i), (col0 + SSM_WIDTH + 256) // 256)),
            pl.BlockSpec((L, LANES), lambda b, i: (row(b, i), 0)),
            const((8, N_XBC)),
            const((1, N_XBC)),
            const((1, LANES)),
            const((1, LANES)),
            const((1, SSM_WIDTH)),
            const((1, SSM_WIDTH)),
            const((LANES, SSM_WIDTH)),
        ] + side_specs,
        out_specs=[pl.BlockSpec((L, SSM_WIDTH), lambda b, i: (row(b, i), 0))] + side_specs,
        out_shape=[jax.ShapeDtypeStruct((T, SSM_WIDTH), BF16)]
        + [jax.ShapeDtypeStruct(w.shape, BF16) for w in side],
        scratch_shapes=[
            pltpu.VMEM((L + 8, N_XBC), F32),
            pltpu.VMEM((SSM_GROUPS, SSM_STATE, SSM_WIDTH // SSM_GROUPS), F32),
            pltpu.VMEM((L, SSM_WIDTH), F32),
        ],
        compiler_params=_cparams(("arbitrary", "arbitrary")),
        name="ssd_mixer",
    )(proj, proj, proj, proj, dt_raw, cw8, cb, dtb, alog, dskip_exp, nw, e_mat, *side)


def _layer_norm_rows(r, g, b):
    mu = jnp.mean(r, axis=-1, keepdims=True)
    d = r - mu
    var = jnp.mean(d * d, axis=-1, keepdims=True)
    return d * lax.rsqrt(var + LN_EPS) * g + b


def _outproj_kernel(att_ref, ssm_ref, x_ref, wa_ref, ws_ref, g_ref, b_ref, wr_ref, br_ref,
                    h_ref, eid_ref, ew_ref):
    chunk = min(OUTPROJ_CHUNK, h_ref.shape[0])
    for c in range(h_ref.shape[0] // chunk):
        rows = slice(c * chunk, (c + 1) * chunk)
        mix = (jnp.dot(att_ref[rows, :], wa_ref[...], preferred_element_type=F32)
               + jnp.dot(ssm_ref[rows, :], ws_ref[...], preferred_element_type=F32))
        h = _layer_norm_rows(DN_ALPHA * x_ref[rows, :] + mix, g_ref[...], b_ref[...])
        h_ref[rows, :] = h
        eid, ew = _route(h, wr_ref[...], br_ref[...])
        eid_ref[rows, :] = eid
        ew_ref[rows, :] = ew


def _route(h, wr, br):
    logits = jnp.dot(h.astype(BF16), wr, preferred_element_type=F32) + br
    lane = lax.broadcasted_iota(jnp.int32, logits.shape, 1)
    lanef = lane.astype(F32)
    big = float(LANES)
    gl = jnp.where(lane < N_EXPERT_GROUPS, logits, -jnp.inf)
    gmax = jnp.max(gl, axis=-1, keepdims=True)
    g_prob = 1.0 / jnp.sum(jnp.exp(gl - gmax), axis=-1, keepdims=True)
    gidx = jnp.min(jnp.where(gl == gmax, lanef, big), axis=-1, keepdims=True)
    lo = N_EXPERT_GROUPS + EXPERTS_PER_GROUP * gidx
    el = jnp.where((lanef >= lo) & (lanef < lo + EXPERTS_PER_GROUP), logits, -jnp.inf)
    t1 = jnp.max(el, axis=-1, keepdims=True)
    i1 = jnp.min(jnp.where(el == t1, lanef, big), axis=-1, keepdims=True)
    el2 = jnp.where(lanef == i1, -jnp.inf, el)
    t2 = jnp.max(el2, axis=-1, keepdims=True)
    i2 = jnp.min(jnp.where(el2 == t2, lanef, big), axis=-1, keepdims=True)
    ex = jnp.exp(t2 - t1)
    w1 = g_prob / (1.0 + ex)
    w2 = g_prob * ex / (1.0 + ex)
    eid = jnp.where(lane == 0, i1 - N_EXPERT_GROUPS, jnp.where(lane == 1, i2 - N_EXPERT_GROUPS, 0.0))
    return eid.astype(jnp.int32), jnp.where(lane == 0, w1, jnp.where(lane == 1, w2, 0.0))


def _out_proj(att, ssm, x2d, wa, ws, g, b, wr, br, tm):
    T, D = x2d.shape
    const = lambda shape: pl.BlockSpec(shape, lambda i: (0, 0))
    rows = lambda w: pl.BlockSpec((tm, w), lambda i: (i, 0))
    return pl.pallas_call(
        _outproj_kernel,
        grid=(T // tm,),
        in_specs=[rows(ATT_WIDTH), rows(SSM_WIDTH), rows(D), const((ATT_WIDTH, D)), const((SSM_WIDTH, D)),
                  const((1, D)), const((1, D)), const((D, LANES)), const((1, LANES))],
        out_specs=[rows(D), rows(LANES), rows(LANES)],
        out_shape=[jax.ShapeDtypeStruct((T, D), F32),
                   jax.ShapeDtypeStruct((T, LANES), jnp.int32),
                   jax.ShapeDtypeStruct((T, LANES), F32)],
        compiler_params=_cparams(("arbitrary",)),
        name="out_proj_ln1_router",
    )(att, ssm, x2d, wa, ws, g, b, wr, br)


def _pos_kernel(eid_ref, stril_ref, dest_ref, pend_ref, tot_ref, run_ref, pstart_ref):
    ph = pl.program_id(0)
    i = pl.program_id(1)
    tb = eid_ref.shape[0]
    lane = lax.broadcasted_iota(jnp.int32, (tb, LANES), 1)
    lanef = lane.astype(F32)
    ef = eid_ref[...].astype(F32)
    oh1 = (lanef == _lane_col(ef, 0)).astype(F32)
    oh2 = (lanef == _lane_col(ef, 1)).astype(F32)
    cnt = oh1 + oh2

    @pl.when((ph == 0) & (i == 0))
    def _():
        tot_ref[...] = jnp.zeros_like(tot_ref)

    @pl.when(ph == 0)
    def _():
        tot_ref[...] += jnp.sum(cnt, axis=0, keepdims=True)

    @pl.when((ph == 1) & (i == 0))
    def _():
        tot = jnp.broadcast_to(tot_ref[...], (8, LANES))
        padded = jnp.floor((tot + (MOE_BM - 1)) * (1.0 / MOE_BM)) * MOE_BM
        lane8 = lax.broadcasted_iota(jnp.int32, (8, LANES), 1)
        ends = padded
        k = 1
        while k < LANES:
            ends = ends + jnp.where(lane8 >= k, pltpu.roll(ends, k, axis=1), 0.0)
            k *= 2
        row8 = lax.broadcasted_iota(jnp.int32, (8, LANES), 0)
        pend_ref[...] = jnp.where(row8 == 0, ends, jnp.where(row8 == 1, tot, 0.0))
        pstart_ref[...] = (ends - padded)[0:1]
        run_ref[...] = jnp.zeros_like(run_ref)

    @pl.when(ph == 1)
    def _():
        pre = jnp.dot(stril_ref[...], cnt.astype(BF16), preferred_element_type=F32)
        slot = pstart_ref[...] + run_ref[...] + pre
        d1 = jnp.sum(oh1 * slot, axis=-1, keepdims=True)
        d2 = jnp.sum(oh2 * slot, axis=-1, keepdims=True)
        dest_ref[...] = jnp.where(lane == 0, d1, jnp.where(lane == 1, d2, 0.0)).astype(jnp.int32)
        run_ref[...] += jnp.sum(cnt, axis=0, keepdims=True)


def _positions(eid, stril):
    T = eid.shape[0]
    tb = POS_TB
    return pl.pallas_call(
        _pos_kernel,
        grid=(2, T // tb),
        in_specs=[pl.BlockSpec((tb, LANES), lambda p, i: (i, 0)),
                  pl.BlockSpec((tb, tb), lambda p, i: (0, 0))],
        out_specs=[pl.BlockSpec((tb, LANES), lambda p, i: (i * p, 0)),
                   pl.BlockSpec((8, LANES), lambda p, i: (0, 0))],
        out_shape=[jax.ShapeDtypeStruct((T, LANES), jnp.int32),
                   jax.ShapeDtypeStruct((8, LANES), F32)],
        scratch_shapes=[pltpu.VMEM((1, LANES), F32), pltpu.VMEM((1, LANES), F32), pltpu.VMEM((1, LANES), F32)],
        compiler_params=_cparams(("arbitrary", "arbitrary")),
        name="dispatch_positions",
    )(eid, stril)


def _row_copy(src, s, dst, d, sem):
    return pltpu.make_async_copy(src.at[pl.ds(s, 1)], dst.at[pl.ds(d, 1)], sem)


def _scatter_kernel(zstart_ref, zcnt_ref, dest_ref, h_ref, xs_hbm, zrow_ref, sem, zsem):
    tb = h_ref.shape[0]

    @pl.when(pl.program_id(0) == 0)
    def _():
        zrow_ref[...] = jnp.zeros_like(zrow_ref)

        def fill(e, c):
            lax.fori_loop(0, zcnt_ref[e],
                          lambda r, c2: (_row_copy(zrow_ref, 0, xs_hbm, zstart_ref[e] + r, zsem).start(), c2)[1], 0)
            return c

        def fill_wait(e, c):
            lax.fori_loop(0, zcnt_ref[e],
                          lambda r, c2: (_row_copy(zrow_ref, 0, xs_hbm, 0, zsem).wait(), c2)[1], 0)
            return c

        nblk = xs_hbm.shape[0] // MOE_BM
        first_free = zstart_ref[N_EXPERTS]

        def tail_copy(b):
            return pltpu.make_async_copy(zrow_ref, xs_hbm.at[pl.ds(b * MOE_BM, MOE_BM)], zsem)

        lax.fori_loop(0, N_EXPERTS, fill, 0)
        lax.fori_loop(first_free, nblk, lambda b, c: (tail_copy(b).start(), c)[1], 0)
        lax.fori_loop(0, N_EXPERTS, fill_wait, 0)
        lax.fori_loop(first_free, nblk, lambda b, c: (tail_copy(b).wait(), c)[1], 0)

    def issue(r, c):
        _row_copy(h_ref, r, xs_hbm, dest_ref[0, 0, 2 * r], sem).start(priority=0)
        _row_copy(h_ref, r, xs_hbm, dest_ref[0, 0, 2 * r + 1], sem).start(priority=1)
        return c

    def drain(r, c):
        _row_copy(h_ref, 0, xs_hbm, 0, sem).wait()
        _row_copy(h_ref, 0, xs_hbm, 0, sem).wait()
        return c

    lax.fori_loop(0, tb, issue, 0, unroll=8)
    lax.fori_loop(0, tb, drain, 0, unroll=8)


def _dispatch(zstart, zcnt, dest3, h, cap):
    T, D = h.shape
    nb = dest3.shape[0]
    tb = dest3.shape[2] // 2
    grid_spec = pltpu.PrefetchScalarGridSpec(
        num_scalar_prefetch=2,
        grid=(nb,),
        in_specs=[pl.BlockSpec((1, 1, 2 * tb), lambda i, zs, zc: (i, 0, 0), memory_space=pltpu.SMEM),
                  pl.BlockSpec((tb, D), lambda i, zs, zc: (i, 0))],
        out_specs=pl.BlockSpec(memory_space=pl.ANY),
        scratch_shapes=[pltpu.VMEM((MOE_BM, D), h.dtype), pltpu.SemaphoreType.DMA, pltpu.SemaphoreType.DMA],
    )
    return pl.pallas_call(
        _scatter_kernel,
        grid_spec=grid_spec,
        out_shape=jax.ShapeDtypeStruct((cap, D), h.dtype),
        compiler_params=_cparams(("arbitrary",)),
        name="moe_dispatch",
    )(zstart, zcnt, dest3, h)


def _expert_kernel(bexp_ref, nused_ref, x_ref, wg_ref, wu_ref, wd_ref, y_ref):
    i = pl.program_id(0)

    @pl.when(i < nused_ref[0])
    def _():
        x = x_ref[...].astype(BF16)
        gate = jnp.dot(x, wg_ref[0], preferred_element_type=F32)
        up = jnp.dot(x, wu_ref[0], preferred_element_type=F32)
        hid = (gate * _sigmoid(gate) * up).astype(BF16)
        y_ref[...] = jnp.dot(hid, wd_ref[0], preferred_element_type=F32)

    @pl.when(i >= nused_ref[0])
    def _():
        y_ref[...] = jnp.zeros_like(y_ref)


def _expert_mlp(bexp, nused, xs, wg, wu, wd):
    cap, D = xs.shape
    bm = MOE_BM
    H = wg.shape[2]
    grid_spec = pltpu.PrefetchScalarGridSpec(
        num_scalar_prefetch=2,
        grid=(cap // bm,),
        in_specs=[
            pl.BlockSpec((bm, D), lambda i, be, nu: (i, 0)),
            pl.BlockSpec((1, D, H), lambda i, be, nu: (be[i], 0, 0)),
            pl.BlockSpec((1, D, H), lambda i, be, nu: (be[i], 0, 0)),
            pl.BlockSpec((1, H, D), lambda i, be, nu: (be[i], 0, 0)),
        ],
        out_specs=pl.BlockSpec((bm, D), lambda i, be, nu: (i, 0)),
    )
    return pl.pallas_call(
        _expert_kernel,
        grid_spec=grid_spec,
        out_shape=jax.ShapeDtypeStruct((cap, D), F32),
        compiler_params=_cparams(("arbitrary",)),
        name="expert_mlp",
    )(bexp, nused, xs, wg, wu, wd)


def _combine_kernel(dest_ref, dnext_ref, h_ref, ew_ref, g_ref, b_ref, y_hbm, o_ref, ybuf, sems):
    i = pl.program_id(0)
    n = pl.num_programs(0)
    tb = h_ref.shape[0]
    slot = i & 1

    def gather(d_ref, s, unroll):
        def issue(r, c):
            _row_copy(y_hbm, d_ref[0, 0, 2 * r], ybuf.at[s, 0], r, sems.at[s]).start(priority=0)
            _row_copy(y_hbm, d_ref[0, 0, 2 * r + 1], ybuf.at[s, 1], r, sems.at[s]).start(priority=1)
            return c
        lax.fori_loop(0, tb, issue, 0, unroll=unroll)

    def wait_slot(s):
        for k in range(2):
            pltpu.make_async_copy(y_hbm.at[pl.ds(0, tb)], ybuf.at[s, k], sems.at[s]).wait()

    @pl.when(i == 0)
    def _():
        gather(dest_ref, slot, 8)

    wait_slot(slot)
    gather(dnext_ref, 1 - slot, tb)
    ew = ew_ref[...]
    ffn = _lane_col(ew, 0) * ybuf[slot, 0] + _lane_col(ew, 1) * ybuf[slot, 1]
    o_ref[...] = _layer_norm_rows(DN_ALPHA * h_ref[...] + ffn, g_ref[...], b_ref[...])

    @pl.when(i == n - 1)
    def _():
        wait_slot(1 - slot)


def _combine(dest3, h, ew, g, b, y):
    T, D = h.shape
    tb = dest3.shape[2] // 2
    nb = T // tb
    return pl.pallas_call(
        _combine_kernel,
        grid=(nb,),
        in_specs=[pl.BlockSpec((1, 1, 2 * tb), lambda i: (i, 0, 0), memory_space=pltpu.SMEM),
                  pl.BlockSpec((1, 1, 2 * tb), lambda i: (jnp.minimum(i + 1, nb - 1), 0, 0),
                               memory_space=pltpu.SMEM),
                  pl.BlockSpec((tb, D), lambda i: (i, 0)),
                  pl.BlockSpec((tb, LANES), lambda i: (i, 0)),
                  pl.BlockSpec((1, D), lambda i: (0, 0)),
                  pl.BlockSpec((1, D), lambda i: (0, 0)),
                  pl.BlockSpec(memory_space=pl.ANY)],
        out_specs=pl.BlockSpec((tb, D), lambda i: (i, 0)),
        out_shape=jax.ShapeDtypeStruct((T, D), F32),
        scratch_shapes=[pltpu.VMEM((2, 2, tb, D), F32), pltpu.SemaphoreType.DMA((2,))],
        compiler_params=_cparams(("arbitrary",)),
        name="moe_combine_ln2",
    )(dest3, dest3, h, ew, g, b, y)


def _pad_lanes(v, n=LANES):
    v = v.reshape(1, -1).astype(F32)
    return jnp.pad(v, ((0, 0), (0, n - v.shape[1])))


def kernel(x, w_in, lambda_q1, lambda_k1, lambda_q2, lambda_k2, attn_norm_w, conv_w, conv_b, dt_bias, a_log, d_skip, ssm_norm_w, w_out, ln1_g, ln1_b, w_router_group, b_router_group, w_router_expert, b_router_expert, w_gate, w_up, w_down, ln2_g, ln2_b):
    B, S, D = x.shape
    T = B * S
    assert w_in.shape[0] == DEPTH == 1
    assert S % ATT_TQ == 0 and S % SSD_L == 0 and T % POS_TB == 0 and T % TOK_TB == 0
    l = 0
    lambda_init = 0.8 - 0.6 * math.exp(-0.3 * l)
    x2d = x.reshape(T, D)

    w_main = w_in[l][:, :N_MAIN].astype(BF16)
    w_dt = jnp.pad(w_in[l][:, N_MAIN:], ((0, 0), (0, LANES - N_DT))).astype(BF16)
    slopes = jnp.exp2(-8.0 * jnp.arange(1, ATT_HEADS + 1, dtype=F32) / ATT_HEADS)
    lamp = jnp.concatenate([_pad_lanes(lambda_q1[l]), _pad_lanes(lambda_k1[l]),
                            _pad_lanes(lambda_q2[l]), _pad_lanes(lambda_k2[l]),
                            jnp.zeros((4, LANES), F32)], axis=0)
    nw_col = attn_norm_w[l].astype(F32).reshape(ATT_V_DIM, 1)
    cw8 = jnp.pad(conv_w[l].astype(F32), ((0, 8 - SSM_CONV), (0, 0)))
    cb = conv_b[l].astype(F32).reshape(1, N_XBC)
    dskip_exp = jnp.repeat(d_skip[l].astype(F32), SSM_HEAD_DIM).reshape(1, SSM_WIDTH)
    ssm_nw = ssm_norm_w[l].astype(F32).reshape(1, SSM_WIDTH)
    head_of_lane = jnp.arange(SSM_WIDTH, dtype=jnp.int32) // SSM_HEAD_DIM
    e_mat = (jnp.arange(LANES, dtype=jnp.int32)[:, None] == head_of_lane[None, :]).astype(BF16)
    wa = w_out[l][:ATT_WIDTH].astype(BF16)
    ws = w_out[l][ATT_WIDTH:].astype(BF16)
    wr = jnp.concatenate(
        [w_router_group[l], jnp.transpose(w_router_expert[l], (1, 0, 2)).reshape(D, N_EXPERTS)], axis=1)
    wr = jnp.pad(wr, ((0, 0), (0, LANES - wr.shape[1]))).astype(BF16)
    br = _pad_lanes(jnp.concatenate([b_router_group[l], b_router_expert[l].reshape(-1)]))
    row = lambda v: v.astype(F32).reshape(1, D)

    tm_in = INPROJ_TM if T % INPROJ_TM == 0 else 256
    proj, dt_raw = _in_proj(x2d, w_main, w_dt, tm_in, INPROJ_TN)
    att = _diff_attention(proj, slopes, lamp, nw_col, B, S, lambda_init)
    experts_f32 = [w[l].astype(F32).reshape(-1, w.shape[-1]) for w in (w_gate, w_up, w_down)]
    ssm, wg_b, wu_b, wd_b = _ssd_mixer(proj, dt_raw, cw8, cb, _pad_lanes(dt_bias[l]), _pad_lanes(a_log[l]),
                                       dskip_exp, ssm_nw, e_mat, B, S, side=experts_f32)
    wg_b, wu_b, wd_b = (w.reshape(src.shape[1:]) for w, src in zip((wg_b, wu_b, wd_b), (w_gate, w_up, w_down)))
    h1, eid, ew = _out_proj(att, ssm, x2d, wa, ws, row(ln1_g[l]), row(ln1_b[l]), wr, br,
                            OUTPROJ_TM if T % OUTPROJ_TM == 0 else 256)

    stril = (jnp.arange(POS_TB)[:, None] > jnp.arange(POS_TB)[None, :]).astype(BF16)
    dest, pend = _positions(eid, stril)
    nblk = (T * 2) // MOE_BM + N_EXPERTS
    cap = nblk * MOE_BM
    pad_ends = pend[0, :N_EXPERTS].astype(jnp.int32)
    blk_start = jnp.arange(nblk, dtype=jnp.int32) * MOE_BM
    nused = (pad_ends[N_EXPERTS - 1] // MOE_BM).astype(jnp.int32)
    last_used = jnp.maximum(nused - 1, 0) * MOE_BM
    bexp = jnp.sum(pad_ends[None, :] <= jnp.minimum(blk_start, last_used)[:, None], axis=1).astype(jnp.int32)
    bexp = jnp.minimum(bexp, N_EXPERTS - 1)
    dest3 = dest[:, :2].reshape(T // TOK_TB, 1, 2 * TOK_TB)
    counts = pend[1, :N_EXPERTS].astype(jnp.int32)
    padded = pad_ends - jnp.concatenate([jnp.zeros((1,), jnp.int32), pad_ends[:-1]])
    zstart = jnp.concatenate([pad_ends - padded + counts, nused.reshape(1)])
    xs_sorted = _dispatch(zstart, padded - counts, dest3, h1, cap)
    y_sorted = _expert_mlp(bexp, nused.reshape(1), xs_sorted, wg_b, wu_b, wd_b)
    out = _combine(dest3, h1, ew, row(ln2_g[l]), row(ln2_b[l]), y_sorted)
    return out.reshape(B, S, D)
```

```python
import functools
import math

import jax
import jax.numpy as jnp
from jax import lax
from jax.experimental import pallas as pl
from jax.experimental.pallas import tpu as pltpu

F32 = jnp.float32
BF16 = jnp.bfloat16

CHUNK = 64
ATT_HEADS = 8
ATT_HEAD_DIM = 64
ATT_V_DIM = 128
ATT_WIDTH = ATT_HEADS * ATT_V_DIM
ATT_VT_ROWS = ATT_V_DIM + 16
LOG2E = 1.4426950408889634
SSM_HEADS = 16
SSM_HEAD_DIM = 64
SSM_WIDTH = SSM_HEADS * SSM_HEAD_DIM
SSM_GROUPS = 2
SSM_STATE = 128
SSM_CONV = 4
N_Q = 1024
N_K = 1024
N_V = 1024
N_Z = 1024
N_XBC = SSM_WIDTH + 2 * SSM_GROUPS * SSM_STATE
N_DT = SSM_HEADS
N_MAIN = N_Q + N_K + N_V + N_Z + N_XBC
N_EXPERT_GROUPS = 4
EXPERTS_PER_GROUP = 8
N_EXPERTS = 32
EXPERT_HIDDEN = 1024
DEPTH = 1
DN_ALPHA = (2 * DEPTH) ** 0.25
LN_EPS = 1e-5
RMS_EPS = 1e-6
LANES = 128

VMEM_LIMIT = 56 * 1024 * 1024

INPROJ_TM = 512
INPROJ_TN = 2816
OUTPROJ_TM = 512
OUTPROJ_CHUNK = 256
ATT_TAIL_STRIPS = 0
ATT_TQ = 512
ATT_TK = 512
SSD_L = 512
MOE_BM = 256
TOK_TB = 512
POS_TB = 1024


def _cparams(sem, flags=None):
    return pltpu.CompilerParams(dimension_semantics=sem, vmem_limit_bytes=VMEM_LIMIT, flags=flags)


def _sigmoid(x):
    return 1.0 / (1.0 + jnp.exp(-x))


def _lane_col(x, idx):
    lane = lax.broadcasted_iota(jnp.int32, x.shape, 1)
    return jnp.sum(jnp.where(lane == idx, x, 0.0), axis=-1, keepdims=True)


def _inproj_kernel(x_ref, w_ref, wdt_ref, o_ref, dt_ref, xb_ref):
    @pl.when(pl.program_id(1) == 0)
    def _():
        xb = x_ref[...].astype(BF16)
        xb_ref[...] = xb
        dt_ref[...] = jnp.dot(xb, wdt_ref[...], preferred_element_type=F32)

    o_ref[...] = jnp.dot(xb_ref[...], w_ref[...], preferred_element_type=F32).astype(o_ref.dtype)


def _in_proj(x2d, w_main, w_dt, tm, tn):
    T, D = x2d.shape
    N = w_main.shape[1]
    return pl.pallas_call(
        _inproj_kernel,
        grid=(T // tm, N // tn),
        in_specs=[
            pl.BlockSpec((tm, D), lambda i, j: (i, 0)),
            pl.BlockSpec((D, tn), lambda i, j: (0, j)),
            pl.BlockSpec((D, LANES), lambda i, j: (0, 0)),
        ],
        out_specs=[
            pl.BlockSpec((tm, tn), lambda i, j: (i, j)),
            pl.BlockSpec((tm, LANES), lambda i, j: (i, 0)),
        ],
        out_shape=[
            jax.ShapeDtypeStruct((T, N), BF16),
            jax.ShapeDtypeStruct((T, LANES), F32),
        ],
        scratch_shapes=[pltpu.VMEM((tm, D), BF16)],
        compiler_params=_cparams(("arbitrary", "arbitrary")),
        name="in_proj",
    )(x2d, w_main, w_dt)


def _attn_kernel(slopes_ref, q_ref, k_ref, v_ref, lamp_ref, nw_ref, o_ref,
                 tab_ref, acc_ref, sa_ref, sb_ref, pa_ref, pb_ref, mxa_ref, mxb_ref, ala_ref, alb_ref,
                 qm_ref, vt_ref, *, lambda_init):
    h = pl.program_id(1)
    tk, tq = tab_ref.shape[1:]
    ratio = tq // tk
    nk = q_ref.shape[0] // tk
    nq = nk // ratio
    slope2 = slopes_ref[h] * LOG2E

    s_rel = lax.broadcasted_iota(jnp.int32, (tk, tq), 0)
    t_rel = lax.broadcasted_iota(jnp.int32, (tk, tq), 1)
    tab_ref[0] = slope2 * s_rel.astype(F32)
    for d in range(ratio):
        s_q = s_rel + d * tk
        allowed = (s_q // CHUNK) <= (t_rel // CHUNK)
        val = slope2 * (t_rel - jnp.abs(t_rel - s_q) - d * tk).astype(F32)
        tab_ref[1 + d] = jnp.where(allowed, val, -jnp.inf)

    def prep(i, c):
        rows = pl.ds(pl.multiple_of(i * tk, tk), tk)
        q = q_ref[rows, :]
        lane = lax.broadcasted_iota(jnp.int32, q.shape, 1)
        qs = (q.astype(F32) * (ATT_HEAD_DIM ** -0.5 * LOG2E)).astype(BF16)
        zero = jnp.zeros_like(qs)
        qm_ref[0, rows, :] = jnp.where(lane < ATT_HEAD_DIM, qs, zero)
        qm_ref[1, rows, :] = jnp.where(lane >= ATT_HEAD_DIM, qs, zero)
        vt_ref[i, :ATT_V_DIM, :] = v_ref[rows, :].astype(F32).T.astype(BF16)
        extra_row = lax.broadcasted_iota(jnp.int32, (ATT_VT_ROWS - ATT_V_DIM, tk), 0)
        vt_ref[i, ATT_V_DIM:, :] = (extra_row == 0).astype(BF16)
        return c

    lax.fori_loop(0, nk, prep, 0)
    nt = (((1,), (1,)), ((), ()))
    lamp = lamp_ref[...]
    lam = (jnp.exp(jnp.sum(lamp[0:1] * lamp[1:2], axis=-1, keepdims=True))
           - jnp.exp(jnp.sum(lamp[2:3] * lamp[3:4], axis=-1, keepdims=True)) + lambda_init)

    half = 256
    ncol = tq // half
    strip = 16

    def last_j(qi):
        return ratio * (qi + 1) - 1

    def scores_piece(pair, s_out, mx_out, m, c):
        qi, j = pair
        cols = slice(c * half, (c + 1) * half)
        kb = k_ref[pl.ds(pl.multiple_of(j * tk, tk), tk), :]
        qh = qm_ref[m, pl.ds(pl.multiple_of(qi * tq + c * half, half), half), :]
        kind = jnp.maximum(j - ratio * qi + 1, 0)
        st = lax.dot_general(kb, qh, nt, preferred_element_type=F32) + tab_ref[kind, :, cols]
        s_out[m, :, cols] = st
        mx_out[m, :, cols] = jnp.max(st, axis=0, keepdims=True)

    def accum_piece(pair, p_in, al_in, m, c):
        cols = slice(c * half, (c + 1) * half)
        acc_ref[m, :, cols] = (al_in[m][:, cols] * acc_ref[m, :, cols]
                               + jnp.dot(vt_ref[pair[1]], p_in[m, :, cols], preferred_element_type=F32))

    def finalize(qi):
        a0 = acc_ref[0]
        a1 = acc_ref[1]
        dv = ATT_V_DIM
        o = a0[:dv] / a0[dv:dv + 1] - lam * (a1[:dv] / a1[dv:dv + 1])
        ms2 = jnp.mean(o * o, axis=0, keepdims=True)
        o = o * lax.rsqrt(ms2 + RMS_EPS) * nw_ref[...] * (1.0 - lambda_init)
        o_ref[pl.ds(pl.multiple_of(qi * tq, tq), tq), :] = o.T.astype(o_ref.dtype)

    def next_pair(pair):
        qi, j = pair
        wrap = j == last_j(qi)
        return jnp.where(wrap, qi + 1, qi), jnp.where(wrap, 0, j + 1)

    def trip(cur_set, nxt_set, state):
        s_c, mx_c, p_c, al_c = cur_set
        s_n, mx_n, p_n, al_n = nxt_set
        prv, cur, m_old = state
        nxt = next_pair(cur)
        nxt_c = (jnp.minimum(nxt[0], nq - 1), jnp.where(nxt[0] >= nq, nk - 1, nxt[1]))
        prv_c = (prv[0], jnp.maximum(prv[1], 0))
        cj = -slope2 * (cur[0] * tq - cur[1] * tk).astype(F32)
        refs, m_out = [], []
        for m in range(2):
            m_prev = jnp.where(cur[1] == 0, -jnp.inf, m_old[m])
            m_new = jnp.maximum(m_prev, mx_c[m] + cj)
            al_c[m] = jnp.exp2(m_prev - m_new)
            refs.append(m_new - cj)
            m_out.append(m_new)

        acc_p = [functools.partial(accum_piece, prv_c, p_n, al_n, m, c) for m in range(2) for c in range(ncol)]
        sco_p = [functools.partial(scores_piece, nxt_c, s_n, mx_n, m, c) for m in range(2) for c in range(ncol)]
        pieces = acc_p[:ncol] + sco_p + acc_p[ncol:]
        sw = 512
        strips = [(m, r, w) for m in range(2) for r in range(tk // strip) for w in range(tq // sw)]
        def prob_strip(m, r, w):
            rows = slice(r * strip, (r + 1) * strip)
            cols = slice(w * sw, (w + 1) * sw)
            p_c[m, rows, cols] = jnp.exp2(s_c[m, rows, cols] - refs[m][:, cols]).astype(BF16)

        per = (len(strips) - ATT_TAIL_STRIPS) // len(pieces)
        for g, piece in enumerate(pieces):
            piece()
            for s in strips[g * per:(g + 1) * per]:
                prob_strip(*s)
        for s in strips[len(pieces) * per:]:
            prob_strip(*s)

        @pl.when(prv[1] == last_j(prv[0]))
        def _():
            finalize(prv[0])

        return cur, nxt, tuple(m_out)

    set_a = (sa_ref, mxa_ref, pa_ref, ala_ref)
    set_b = (sb_ref, mxb_ref, pb_ref, alb_ref)
    acc_ref[...] = jnp.zeros_like(acc_ref)
    pb_ref[...] = jnp.zeros_like(pb_ref)
    alb_ref[...] = jnp.ones_like(alb_ref)
    zero_i = jnp.int32(0)
    for m in range(2):
        for c in range(ncol):
            scores_piece((zero_i, zero_i), sa_ref, mxa_ref, m, c)

    def body(t, state):
        return trip(set_b, set_a, trip(set_a, set_b, state))

    n_pairs = ratio * nq * (nq + 1) // 2
    m_init = jnp.full((1, tq), -jnp.inf, F32)
    first = ((zero_i, jnp.int32(-1)), (zero_i, zero_i), (m_init, m_init))
    state = lax.fori_loop(0, n_pairs // 2, body, first)
    if n_pairs % 2:
        trip(set_a, set_b, state)

    last_set = set_a if (n_pairs - 1) % 2 == 0 else set_b
    last_pair = (jnp.int32(nq - 1), jnp.int32(nk - 1))
    for m in range(2):
        for c in range(ncol):
            accum_piece(last_pair, last_set[2], last_set[3], m, c)
    finalize(last_pair[0])


def _diff_attention(proj, slopes, lamp, nw_col, B, S, lambda_init):
    T = B * S
    tq, tk = ATT_TQ, ATT_TK
    nk = S // tk
    kern = functools.partial(_attn_kernel, lambda_init=lambda_init)
    grid_spec = pltpu.PrefetchScalarGridSpec(
        num_scalar_prefetch=1,
        grid=(B, ATT_HEADS),
        in_specs=[
            pl.BlockSpec((S, LANES), lambda b, h, s: (b, h)),
            pl.BlockSpec((S, LANES), lambda b, h, s: (b, N_Q // LANES + h)),
            pl.BlockSpec((S, LANES), lambda b, h, s: (b, (N_Q + N_K) // LANES + h)),
            pl.BlockSpec((8, LANES), lambda b, h, s: (0, 0)),
            pl.BlockSpec((ATT_V_DIM, 1), lambda b, h, s: (0, 0)),
        ],
        out_specs=pl.BlockSpec((S, ATT_V_DIM), lambda b, h, s: (b, h)),
        scratch_shapes=[
            pltpu.VMEM((1 + tq // tk, tk, tq), F32),
            pltpu.VMEM((2, ATT_VT_ROWS, tq), F32),
            pltpu.VMEM((2, tk, tq), F32), pltpu.VMEM((2, tk, tq), F32),
            pltpu.VMEM((2, tk, tq), BF16), pltpu.VMEM((2, tk, tq), BF16),
            pltpu.VMEM((2, 1, tq), F32), pltpu.VMEM((2, 1, tq), F32),
            pltpu.VMEM((2, 1, tq), F32), pltpu.VMEM((2, 1, tq), F32),
            pltpu.VMEM((2, S, LANES), BF16),
            pltpu.VMEM((nk, ATT_VT_ROWS, tk), BF16),
        ],
    )
    return pl.pallas_call(
        kern,
        grid_spec=grid_spec,
        out_shape=jax.ShapeDtypeStruct((T, ATT_WIDTH), BF16),
        compiler_params=_cparams(("arbitrary", "arbitrary")),
        name="diff_attention",
    )(slopes, proj, proj, proj, lamp, nw_col)


def _expand_heads(v, e):
    hi = v.astype(BF16)
    lo = (v - hi.astype(F32)).astype(BF16)
    return jnp.dot(hi, e, preferred_element_type=F32) + jnp.dot(lo, e, preferred_element_type=F32)


def _ssd_kernel(z_ref, xs_ref, b_ref, c_ref, dt_ref, cw_ref, cb_ref, dtb_ref, alog_ref,
                dskip_ref, nw_ref, e_ref, *rest, n_side):
    side_in = rest[:n_side]
    o_ref = rest[n_side]
    side_out = rest[n_side + 1:2 * n_side + 1]
    ext_ref, st_ref, y_ref = rest[2 * n_side + 1:]
    for w_in_ref, w_out_ref in zip(side_in, side_out):
        w_out_ref[...] = w_in_ref[...].astype(w_out_ref.dtype)
    blk = pl.program_id(1)
    L = z_ref.shape[0]
    nchunk = L // CHUNK
    gw = SSM_WIDTH // SSM_GROUPS
    hpg = SSM_HEADS // SSM_GROUPS

    @pl.when(blk == 0)
    def _():
        ext_ref[0:8, :] = jnp.zeros((8, N_XBC), F32)
        st_ref[...] = jnp.zeros_like(st_ref)

    cur = jnp.concatenate([xs_ref[...], b_ref[...], c_ref[...]], axis=1).astype(F32)
    ext_ref[8:, :] = cur
    cw = cw_ref[...]
    conv = cb_ref[...] + cw[3:4] * cur
    for j in range(SSM_CONV - 1):
        conv = conv + cw[j:j + 1] * ext_ref[pl.ds(8 - (SSM_CONV - 1) + j, L), :]
    ext_ref[0:8, :] = cur[L - 8:, :]
    xbc = conv * _sigmoid(conv)
    xs = xbc[:, :SSM_WIDTH]
    bmb = xbc[:, SSM_WIDTH:SSM_WIDTH + SSM_GROUPS * SSM_STATE].astype(BF16)
    cmb = xbc[:, SSM_WIDTH + SSM_GROUPS * SSM_STATE:].astype(BF16)
    xsb = xs.astype(BF16)

    lane1 = lax.broadcasted_iota(jnp.int32, (1, LANES), 1)
    dtx = dt_ref[...] + dtb_ref[...]
    dtp = jnp.maximum(dtx, 0.0) + jnp.log1p(jnp.exp(-jnp.abs(dtx)))
    a_head = jnp.where(lane1 < SSM_HEADS, -jnp.exp(alog_ref[...]), 0.0)
    acs = dtp * a_head
    row_in_chunk = lax.broadcasted_iota(jnp.int32, (L, LANES), 0) & (CHUNK - 1)
    k = 1
    while k < CHUNK:
        acs = acs + jnp.where(row_in_chunk >= k, pltpu.roll(acs, k, axis=0), 0.0)
        k *= 2
    acs_t = acs.T
    dt_t = dtp.T

    e = e_ref[...]
    acs_last = jnp.concatenate(
        [jnp.broadcast_to(acs[c * CHUNK + CHUNK - 1:(c + 1) * CHUNK, :], (CHUNK, LANES)) for c in range(nchunk)],
        axis=0)
    w_exp = _expand_heads(dtp * jnp.exp(acs_last - acs), e)
    od_exp = _expand_heads(jnp.exp(acs), e)
    row8 = lax.broadcasted_iota(jnp.int32, (8, LANES), 0)
    cd8 = jnp.zeros((8, LANES), F32)
    for c in range(nchunk):
        cd8 = jnp.where(row8 == c, jnp.exp(acs[c * CHUNK + CHUNK - 1:(c + 1) * CHUNK, :]), cd8)
    cd_exp = _expand_heads(cd8, e)
    xw = (xs * w_exp).astype(BF16)

    tn = (((0,), (0,)), ((), ()))
    for c in range(nchunk):
        r0 = c * CHUNK
        for g in range(SSM_GROUPS):
            st = st_ref[g]
            cg = cmb[r0:r0 + CHUNK, g * SSM_STATE:(g + 1) * SSM_STATE]
            bg = bmb[r0:r0 + CHUNK, g * SSM_STATE:(g + 1) * SSM_STATE]
            y_ref[r0:r0 + CHUNK, g * gw:(g + 1) * gw] = jnp.dot(cg, st.astype(BF16), preferred_element_type=F32)
            snew = lax.dot_general(bg, xw[r0:r0 + CHUNK, g * gw:(g + 1) * gw], tn, preferred_element_type=F32)
            st_ref[g] = st * cd_exp[c:c + 1, g * gw:(g + 1) * gw] + snew
    y = y_ref[...] * od_exp + xs * dskip_ref[...]

    pair = 2 * CHUNK
    li = lax.broadcasted_iota(jnp.int32, (pair, pair), 0)
    si = lax.broadcasted_iota(jnp.int32, (pair, pair), 1)
    mask2 = (li >= si) & ((si >= CHUNK) | (li < CHUNK))
    lanep = lax.broadcasted_iota(jnp.int32, (pair, LANES), 1)
    nt = (((1,), (1,)), ((), ()))
    for pp in range(L // pair):
        r0 = pp * pair
        acs_p = acs[r0:r0 + pair, :]
        for g in range(SSM_GROUPS):
            cb2 = lax.dot_general(cmb[r0:r0 + pair, g * SSM_STATE:(g + 1) * SSM_STATE],
                                  bmb[r0:r0 + pair, g * SSM_STATE:(g + 1) * SSM_STATE],
                                  nt, preferred_element_type=F32)
            for hh in range(hpg // 2):
                hp = g * (hpg // 2) + hh
                mats = []
                for u in range(2):
                    hd = 2 * hp + u
                    seg = _lane_col(acs_p, hd) - acs_t[hd:hd + 1, r0:r0 + pair]
                    decay = jnp.exp(jnp.where(mask2, seg, -jnp.inf))
                    mats.append((cb2 * decay * dt_t[hd:hd + 1, r0:r0 + pair]).astype(BF16))
                lhs = jnp.concatenate(mats, axis=1)
                xp = xsb[r0:r0 + pair, hp * LANES:(hp + 1) * LANES]
                zero = jnp.zeros_like(xp)
                rhs = jnp.concatenate([jnp.where(lanep < SSM_HEAD_DIM, xp, zero),
                                       jnp.where(lanep >= SSM_HEAD_DIM, xp, zero)], axis=0)
                y_ref[r0:r0 + pair, hp * LANES:(hp + 1) * LANES] = jnp.dot(lhs, rhs, preferred_element_type=F32)
    y = y + y_ref[...]

    z = z_ref[...].astype(F32)
    y = y * (z * _sigmoid(z))
    outs = []
    for g in range(SSM_GROUPS):
        yg = y[:, g * gw:(g + 1) * gw]
        outs.append(yg * lax.rsqrt(jnp.mean(yg * yg, axis=-1, keepdims=True) + RMS_EPS))
    o_ref[...] = (jnp.concatenate(outs, axis=1) * nw_ref[...]).astype(o_ref.dtype)


def _ssd_mixer(proj, dt_raw, cw8, cb, dtb, alog, dskip_exp, nw, e_mat, B, S, side=()):
    T = B * S
    L = SSD_L
    nb = S // L
    row = lambda b, i: b * nb + i
    col0 = (N_Q + N_K + N_V + N_Z)
    const = lambda shape: pl.BlockSpec(shape, lambda b, i: (0, 0))
    steps = B * nb
    assert all(w.shape[0] % (8 * steps) == 0 for w in side)
    side_specs = [pl.BlockSpec((w.shape[0] // steps, w.shape[1]), lambda b, i: (row(b, i), 0)) for w in side]
    return pl.pallas_call(
        functools.partial(_ssd_kernel, n_side=len(side)),
        grid=(B, nb),
        in_specs=[
            pl.BlockSpec((L, N_Z), lambda b, i: (row(b, i), (N_Q + N_K + N_V) // N_Z)),
            pl.BlockSpec((L, SSM_WIDTH), lambda b, i: (row(b, i), col0 // SSM_WIDTH)),
            pl.BlockSpec((L, 256), lambda b, i: (row(b, i), (col0 + SSM_WIDTH) // 256)),
            pl.BlockSpec((L, 256), lambda b, i: (row(b, i), (col0 + SSM_WIDTH + 256) // 256)),
            pl.BlockSpec((L, LANES), lambda b, i: (row(b, i), 0)),
            const((8, N_XBC)),
            const((1, N_XBC)),
            const((1, LANES)),
            const((1, LANES)),
            const((1, SSM_WIDTH)),
            const((1, SSM_WIDTH)),
            const((LANES, SSM_WIDTH)),
        ] + side_specs,
        out_specs=[pl.BlockSpec((L, SSM_WIDTH), lambda b, i: (row(b, i), 0))] + side_specs,
        out_shape=[jax.ShapeDtypeStruct((T, SSM_WIDTH), BF16)]
        + [jax.ShapeDtypeStruct(w.shape, BF16) for w in side],
        scratch_shapes=[
            pltpu.VMEM((L + 8, N_XBC), F32),
            pltpu.VMEM((SSM_GROUPS, SSM_STATE, SSM_WIDTH // SSM_GROUPS), F32),
            pltpu.VMEM((L, SSM_WIDTH), F32),
        ],
        compiler_params=_cparams(("arbitrary", "arbitrary")),
        name="ssd_mixer",
    )(proj, proj, proj, proj, dt_raw, cw8, cb, dtb, alog, dskip_exp, nw, e_mat, *side)


def _layer_norm_rows(r, g, b):
    mu = jnp.mean(r, axis=-1, keepdims=True)
    d = r - mu
    var = jnp.mean(d * d, axis=-1, keepdims=True)
    return d * lax.rsqrt(var + LN_EPS) * g + b


def _outproj_kernel(att_ref, ssm_ref, x_ref, wa_ref, ws_ref, g_ref, b_ref, wr_ref, br_ref,
                    h_ref, eid_ref, ew_ref):
    chunk = min(OUTPROJ_CHUNK, h_ref.shape[0])
    for c in range(h_ref.shape[0] // chunk):
        rows = slice(c * chunk, (c + 1) * chunk)
        mix = (jnp.dot(att_ref[rows, :], wa_ref[...], preferred_element_type=F32)
               + jnp.dot(ssm_ref[rows, :], ws_ref[...], preferred_element_type=F32))
        h = _layer_norm_rows(DN_ALPHA * x_ref[rows, :] + mix, g_ref[...], b_ref[...])
        h_ref[rows, :] = h
        eid, ew = _route(h, wr_ref[...], br_ref[...])
        eid_ref[rows, :] = eid
        ew_ref[rows, :] = ew


def _route(h, wr, br):
    logits = jnp.dot(h.astype(BF16), wr, preferred_element_type=F32) + br
    lane = lax.broadcasted_iota(jnp.int32, logits.shape, 1)
    lanef = lane.astype(F32)
    big = float(LANES)
    gl = jnp.where(lane < N_EXPERT_GROUPS, logits, -jnp.inf)
    gmax = jnp.max(gl, axis=-1, keepdims=True)
    g_prob = 1.0 / jnp.sum(jnp.exp(gl - gmax), axis=-1, keepdims=True)
    gidx = jnp.min(jnp.where(gl == gmax, lanef, big), axis=-1, keepdims=True)
    lo = N_EXPERT_GROUPS + EXPERTS_PER_GROUP * gidx
    el = jnp.where((lanef >= lo) & (lanef < lo + EXPERTS_PER_GROUP), logits, -jnp.inf)
    t1 = jnp.max(el, axis=-1, keepdims=True)
    i1 = jnp.min(jnp.where(el == t1, lanef, big), axis=-1, keepdims=True)
    el2 = jnp.where(lanef == i1, -jnp.inf, el)
    t2 = jnp.max(el2, axis=-1, keepdims=True)
    i2 = jnp.min(jnp.where(el2 == t2, lanef, big), axis=-1, keepdims=True)
    ex = jnp.exp(t2 - t1)
    w1 = g_prob / (1.0 + ex)
    w2 = g_prob * ex / (1.0 + ex)
    eid = jnp.where(lane == 0, i1 - N_EXPERT_GROUPS, jnp.where(lane == 1, i2 - N_EXPERT_GROUPS, 0.0))
    return eid.astype(jnp.int32), jnp.where(lane == 0, w1, jnp.where(lane == 1, w2, 0.0))


def _out_proj(att, ssm, x2d, wa, ws, g, b, wr, br, tm):
    T, D = x2d.shape
    const = lambda shape: pl.BlockSpec(shape, lambda i: (0, 0))
    rows = lambda w: pl.BlockSpec((tm, w), lambda i: (i, 0))
    return pl.pallas_call(
        _outproj_kernel,
        grid=(T // tm,),
        in_specs=[rows(ATT_WIDTH), rows(SSM_WIDTH), rows(D), const((ATT_WIDTH, D)), const((SSM_WIDTH, D)),
                  const((1, D)), const((1, D)), const((D, LANES)), const((1, LANES))],
        out_specs=[rows(D), rows(LANES), rows(LANES)],
        out_shape=[jax.ShapeDtypeStruct((T, D), F32),
                   jax.ShapeDtypeStruct((T, LANES), jnp.int32),
                   jax.ShapeDtypeStruct((T, LANES), F32)],
        compiler_params=_cparams(("arbitrary",)),
        name="out_proj_ln1_router",
    )(att, ssm, x2d, wa, ws, g, b, wr, br)


def _pos_kernel(eid_ref, stril_ref, dest_ref, pend_ref, tot_ref, run_ref, pstart_ref):
    ph = pl.program_id(0)
    i = pl.program_id(1)
    tb = eid_ref.shape[0]
    lane = lax.broadcasted_iota(jnp.int32, (tb, LANES), 1)
    lanef = lane.astype(F32)
    ef = eid_ref[...].astype(F32)
    oh1 = (lanef == _lane_col(ef, 0)).astype(F32)
    oh2 = (lanef == _lane_col(ef, 1)).astype(F32)
    cnt = oh1 + oh2

    @pl.when((ph == 0) & (i == 0))
    def _():
        tot_ref[...] = jnp.zeros_like(tot_ref)

    @pl.when(ph == 0)
    def _():
        tot_ref[...] += jnp.sum(cnt, axis=0, keepdims=True)

    @pl.when((ph == 1) & (i == 0))
    def _():
        tot = jnp.broadcast_to(tot_ref[...], (8, LANES))
        padded = jnp.floor((tot + (MOE_BM - 1)) * (1.0 / MOE_BM)) * MOE_BM
        lane8 = lax.broadcasted_iota(jnp.int32, (8, LANES), 1)
        ends = padded
        k = 1
        while k < LANES:
            ends = ends + jnp.where(lane8 >= k, pltpu.roll(ends, k, axis=1), 0.0)
            k *= 2
        row8 = lax.broadcasted_iota(jnp.int32, (8, LANES), 0)
        pend_ref[...] = jnp.where(row8 == 0, ends, jnp.where(row8 == 1, tot, 0.0))
        pstart_ref[...] = (ends - padded)[0:1]
        run_ref[...] = jnp.zeros_like(run_ref)

    @pl.when(ph == 1)
    def _():
        pre = jnp.dot(stril_ref[...], cnt.astype(BF16), preferred_element_type=F32)
        slot = pstart_ref[...] + run_ref[...] + pre
        d1 = jnp.sum(oh1 * slot, axis=-1, keepdims=True)
        d2 = jnp.sum(oh2 * slot, axis=-1, keepdims=True)
        dest_ref[...] = jnp.where(lane == 0, d1, jnp.where(lane == 1, d2, 0.0)).astype(jnp.int32)
        run_ref[...] += jnp.sum(cnt, axis=0, keepdims=True)


def _positions(eid, stril):
    T = eid.shape[0]
    tb = POS_TB
    return pl.pallas_call(
        _pos_kernel,
        grid=(2, T // tb),
        in_specs=[pl.BlockSpec((tb, LANES), lambda p, i: (i, 0)),
                  pl.BlockSpec((tb, tb), lambda p, i: (0, 0))],
        out_specs=[pl.BlockSpec((tb, LANES), lambda p, i: (i * p, 0)),
                   pl.BlockSpec((8, LANES), lambda p, i: (0, 0))],
        out_shape=[jax.ShapeDtypeStruct((T, LANES), jnp.int32),
                   jax.ShapeDtypeStruct((8, LANES), F32)],
        scratch_shapes=[pltpu.VMEM((1, LANES), F32), pltpu.VMEM((1, LANES), F32), pltpu.VMEM((1, LANES), F32)],
        compiler_params=_cparams(("arbitrary", "arbitrary")),
        name="dispatch_positions",
    )(eid, stril)


def _row_copy(src, s, dst, d, sem):
    return pltpu.make_async_copy(src.at[pl.ds(s, 1)], dst.at[pl.ds(d, 1)], sem)


def _scatter_kernel(zstart_ref, zcnt_ref, dest_ref, h_ref, xs_hbm, zrow_ref, sem, zsem):
    tb = h_ref.shape[0]

    @pl.when(pl.program_id(0) == 0)
    def _():
        zrow_ref[...] = jnp.zeros_like(zrow_ref)

        def fill(e, c):
            lax.fori_loop(0, zcnt_ref[e],
                          lambda r, c2: (_row_copy(zrow_ref, 0, xs_hbm, zstart_ref[e] + r, zsem).start(), c2)[1], 0)
            return c

        def fill_wait(e, c):
            lax.fori_loop(0, zcnt_ref[e],
                          lambda r, c2: (_row_copy(zrow_ref, 0, xs_hbm, 0, zsem).wait(), c2)[1], 0)
            return c

        nblk = xs_hbm.shape[0] // MOE_BM
        first_free = zstart_ref[N_EXPERTS]

        def tail_copy(b):
            return pltpu.make_async_copy(zrow_ref, xs_hbm.at[pl.ds(b * MOE_BM, MOE_BM)], zsem)

        lax.fori_loop(0, N_EXPERTS, fill, 0)
        lax.fori_loop(first_free, nblk, lambda b, c: (tail_copy(b).start(), c)[1], 0)
        lax.fori_loop(0, N_EXPERTS, fill_wait, 0)
        lax.fori_loop(first_free, nblk, lambda b, c: (tail_copy(b).wait(), c)[1], 0)

    def issue(r, c):
        _row_copy(h_ref, r, xs_hbm, dest_ref[0, 0, 2 * r], sem).start(priority=0)
        _row_copy(h_ref, r, xs_hbm, dest_ref[0, 0, 2 * r + 1], sem).start(priority=1)
        return c

    def drain(r, c):
        _row_copy(h_ref, 0, xs_hbm, 0, sem).wait()
        _row_copy(h_ref, 0, xs_hbm, 0, sem).wait()
        return c

    lax.fori_loop(0, tb, issue, 0, unroll=8)
    lax.fori_loop(0, tb, drain, 0, unroll=8)


def _dispatch(zstart, zcnt, dest3, h, cap):
    T, D = h.shape
    nb = dest3.shape[0]
    tb = dest3.shape[2] // 2
    grid_spec = pltpu.PrefetchScalarGridSpec(
        num_scalar_prefetch=2,
        grid=(nb,),
        in_specs=[pl.BlockSpec((1, 1, 2 * tb), lambda i, zs, zc: (i, 0, 0), memory_space=pltpu.SMEM),
                  pl.BlockSpec((tb, D), lambda i, zs, zc: (i, 0))],
        out_specs=pl.BlockSpec(memory_space=pl.ANY),
        scratch_shapes=[pltpu.VMEM((MOE_BM, D), h.dtype), pltpu.SemaphoreType.DMA, pltpu.SemaphoreType.DMA],
    )
    return pl.pallas_call(
        _scatter_kernel,
        grid_spec=grid_spec,
        out_shape=jax.ShapeDtypeStruct((cap, D), h.dtype),
        compiler_params=_cparams(("arbitrary",)),
        name="moe_dispatch",
    )(zstart, zcnt, dest3, h)


def _expert_kernel(bexp_ref, nused_ref, x_ref, wg_ref, wu_ref, wd_ref, y_ref):
    i = pl.program_id(0)

    @pl.when(i < nused_ref[0])
    def _():
        x = x_ref[...].astype(BF16)
        gate = jnp.dot(x, wg_ref[0], preferred_element_type=F32)
        up = jnp.dot(x, wu_ref[0], preferred_element_type=F32)
        hid = (gate * _sigmoid(gate) * up).astype(BF16)
        y_ref[...] = jnp.dot(hid, wd_ref[0], preferred_element_type=F32)

    @pl.when(i >= nused_ref[0])
    def _():
        y_ref[...] = jnp.zeros_like(y_ref)


def _expert_mlp(bexp, nused, xs, wg, wu, wd):
    cap, D = xs.shape
    bm = MOE_BM
    H = wg.shape[2]
    grid_spec = pltpu.PrefetchScalarGridSpec(
        num_scalar_prefetch=2,
        grid=(cap // bm,),
        in_specs=[
            pl.BlockSpec((bm, D), lambda i, be, nu: (i, 0)),
            pl.BlockSpec((1, D, H), lambda i, be, nu: (be[i], 0, 0)),
            pl.BlockSpec((1, D, H), lambda i, be, nu: (be[i], 0, 0)),
            pl.BlockSpec((1, H, D), lambda i, be, nu: (be[i], 0, 0)),
        ],
        out_specs=pl.BlockSpec((bm, D), lambda i, be, nu: (i, 0)),
    )
    return pl.pallas_call(
        _expert_kernel,
        grid_spec=grid_spec,
        out_shape=jax.ShapeDtypeStruct((cap, D), F32),
        compiler_params=_cparams(("arbitrary",)),
        name="expert_mlp",
    )(bexp, nused, xs, wg, wu, wd)


def _combine_kernel(dest_ref, dnext_ref, h_ref, ew_ref, g_ref, b_ref, y_hbm, o_ref, ybuf, sems):
    i = pl.program_id(0)
    n = pl.num_programs(0)
    tb = h_ref.shape[0]
    slot = i & 1

    def gather(d_ref, s, unroll):
        def issue(r, c):
            _row_copy(y_hbm, d_ref[0, 0, 2 * r], ybuf.at[s, 0], r, sems.at[s]).start(priority=0)
            _row_copy(y_hbm, d_ref[0, 0, 2 * r + 1], ybuf.at[s, 1], r, sems.at[s]).start(priority=1)
            return c
        lax.fori_loop(0, tb, issue, 0, unroll=unroll)

    def wait_slot(s):
        for k in range(2):
            pltpu.make_async_copy(y_hbm.at[pl.ds(0, tb)], ybuf.at[s, k], sems.at[s]).wait()

    @pl.when(i == 0)
    def _():
        gather(dest_ref, slot, 8)

    wait_slot(slot)
    gather(dnext_ref, 1 - slot, tb)
    ew = ew_ref[...]
    ffn = _lane_col(ew, 0) * ybuf[slot, 0] + _lane_col(ew, 1) * ybuf[slot, 1]
    o_ref[...] = _layer_norm_rows(DN_ALPHA * h_ref[...] + ffn, g_ref[...], b_ref[...])

    @pl.when(i == n - 1)
    def _():
        wait_slot(1 - slot)


def _combine(dest3, h, ew, g, b, y):
    T, D = h.shape
    tb = dest3.shape[2] // 2
    nb = T // tb
    return pl.pallas_call(
        _combine_kernel,
        grid=(nb,),
        in_specs=[pl.BlockSpec((1, 1, 2 * tb), lambda i: (i, 0, 0), memory_space=pltpu.SMEM),
                  pl.BlockSpec((1, 1, 2 * tb), lambda i: (jnp.minimum(i + 1, nb - 1), 0, 0),
                               memory_space=pltpu.SMEM),
                  pl.BlockSpec((tb, D), lambda i: (i, 0)),
                  pl.BlockSpec((tb, LANES), lambda i: (i, 0)),
                  pl.BlockSpec((1, D), lambda i: (0, 0)),
                  pl.BlockSpec((1, D), lambda i: (0, 0)),
                  pl.BlockSpec(memory_space=pl.ANY)],
        out_specs=pl.BlockSpec((tb, D), lambda i: (i, 0)),
        out_shape=jax.ShapeDtypeStruct((T, D), F32),
        scratch_shapes=[pltpu.VMEM((2, 2, tb, D), F32), pltpu.SemaphoreType.DMA((2,))],
        compiler_params=_cparams(("arbitrary",)),
        name="moe_combine_ln2",
    )(dest3, dest3, h, ew, g, b, y)


def _pad_lanes(v, n=LANES):
    v = v.reshape(1, -1).astype(F32)
    return jnp.pad(v, ((0, 0), (0, n - v.shape[1])))


def kernel(x, w_in, lambda_q1, lambda_k1, lambda_q2, lambda_k2, attn_norm_w, conv_w, conv_b, dt_bias, a_log, d_skip, ssm_norm_w, w_out, ln1_g, ln1_b, w_router_group, b_router_group, w_router_expert, b_router_expert, w_gate, w_up, w_down, ln2_g, ln2_b):
    B, S, D = x.shape
    T = B * S
    assert w_in.shape[0] == DEPTH == 1
    assert S % ATT_TQ == 0 and S % SSD_L == 0 and T % POS_TB == 0 and T % TOK_TB == 0
    l = 0
    lambda_init = 0.8 - 0.6 * math.exp(-0.3 * l)
    x2d = x.reshape(T, D)

    w_main = w_in[l][:, :N_MAIN].astype(BF16)
    w_dt = jnp.pad(w_in[l][:, N_MAIN:], ((0, 0), (0, LANES - N_DT))).astype(BF16)
    slopes = jnp.exp2(-8.0 * jnp.arange(1, ATT_HEADS + 1, dtype=F32) / ATT_HEADS)
    lamp = jnp.concatenate([_pad_lanes(lambda_q1[l]), _pad_lanes(lambda_k1[l]),
                            _pad_lanes(lambda_q2[l]), _pad_lanes(lambda_k2[l]),
                            jnp.zeros((4, LANES), F32)], axis=0)
    nw_col = attn_norm_w[l].astype(F32).reshape(ATT_V_DIM, 1)
    cw8 = jnp.pad(conv_w[l].astype(F32), ((0, 8 - SSM_CONV), (0, 0)))
    cb = conv_b[l].astype(F32).reshape(1, N_XBC)
    dskip_exp = jnp.repeat(d_skip[l].astype(F32), SSM_HEAD_DIM).reshape(1, SSM_WIDTH)
    ssm_nw = ssm_norm_w[l].astype(F32).reshape(1, SSM_WIDTH)
    head_of_lane = jnp.arange(SSM_WIDTH, dtype=jnp.int32) // SSM_HEAD_DIM
    e_mat = (jnp.arange(LANES, dtype=jnp.int32)[:, None] == head_of_lane[None, :]).astype(BF16)
    wa = w_out[l][:ATT_WIDTH].astype(BF16)
    ws = w_out[l][ATT_WIDTH:].astype(BF16)
    wr = jnp.concatenate(
        [w_router_group[l], jnp.transpose(w_router_expert[l], (1, 0, 2)).reshape(D, N_EXPERTS)], axis=1)
    wr = jnp.pad(wr, ((0, 0), (0, LANES - wr.shape[1]))).astype(BF16)
    br = _pad_lanes(jnp.concatenate([b_router_group[l], b_router_expert[l].reshape(-1)]))
    row = lambda v: v.astype(F32).reshape(1, D)

    tm_in = INPROJ_TM if T % INPROJ_TM == 0 else 256
    proj, dt_raw = _in_proj(x2d, w_main, w_dt, tm_in, INPROJ_TN)
    att = _diff_attention(proj, slopes, lamp, nw_col, B, S, lambda_init)
    experts_f32 = [w[l].astype(F32).reshape(-1, w.shape[-1]) for w in (w_gate, w_up, w_down)]
    ssm, wg_b, wu_b, wd_b = _ssd_mixer(proj, dt_raw, cw8, cb, _pad_lanes(dt_bias[l]), _pad_lanes(a_log[l]),
                                       dskip_exp, ssm_nw, e_mat, B, S, side=experts_f32)
    wg_b, wu_b, wd_b = (w.reshape(src.shape[1:]) for w, src in zip((wg_b, wu_b, wd_b), (w_gate, w_up, w_down)))
    h1, eid, ew = _out_proj(att, ssm, x2d, wa, ws, row(ln1_g[l]), row(ln1_b[l]), wr, br,
                            OUTPROJ_TM if T % OUTPROJ_TM == 0 else 256)

    stril = (jnp.arange(POS_TB)[:, None] > jnp.arange(POS_TB)[None, :]).astype(BF16)
    dest, pend = _positions(eid, stril)
    nblk = (T * 2) // MOE_BM + N_EXPERTS
    cap = nblk * MOE_BM
    pad_ends = pend[0, :N_EXPERTS].astype(jnp.int32)
    blk_start = jnp.arange(nblk, dtype=jnp.int32) * MOE_BM
    nused = (pad_ends[N_EXPERTS - 1] // MOE_BM).astype(jnp.int32)
    last_used = jnp.maximum(nused - 1, 0) * MOE_BM
    bexp = jnp.sum(pad_ends[None, :] <= jnp.minimum(blk_start, last_used)[:, None], axis=1).astype(jnp.int32)
    bexp = jnp.minimum(bexp, N_EXPERTS - 1)
    dest3 = dest[:, :2].reshape(T // TOK_TB, 1, 2 * TOK_TB)
    counts = pend[1, :N_EXPERTS].astype(jnp.int32)
    padded = pad_ends - jnp.concatenate([jnp.zeros((1,), jnp.int32), pad_ends[:-1]])
    zstart = jnp.concatenate([pad_ends - padded + counts, nused.reshape(1)])
    xs_sorted = _dispatch(zstart, padded - counts, dest3, h1, cap)
    y_sorted = _expert_mlp(bexp, nused.reshape(1), xs_sorted, wg_b, wu_b, wd_b)
    out = _combine(dest3, h1, ew, row(ln2_g[l]), row(ln2_b[l]), y_sorted)
    return out.reshape(B, S, D)
```

```python
import functools
import math

import jax
import jax.numpy as jnp
from jax import lax
from jax.experimental import pallas as pl
from jax.experimental.pallas import tpu as pltpu

F32 = jnp.float32
BF16 = jnp.bfloat16

CHUNK = 64
ATT_HEADS = 8
ATT_HEAD_DIM = 64
ATT_V_DIM = 128
ATT_WIDTH = ATT_HEADS * ATT_V_DIM
ATT_VT_ROWS = ATT_V_DIM + 16
LOG2E = 1.4426950408889634
SSM_HEADS = 16
SSM_HEAD_DIM = 64
SSM_WIDTH = SSM_HEADS * SSM_HEAD_DIM
SSM_GROUPS = 2
SSM_STATE = 128
SSM_CONV = 4
N_Q = 1024
N_K = 1024
N_V = 1024
N_Z = 1024
N_XBC = SSM_WIDTH + 2 * SSM_GROUPS * SSM_STATE
N_DT = SSM_HEADS
N_MAIN = N_Q + N_K + N_V + N_Z + N_XBC
N_EXPERT_GROUPS = 4
EXPERTS_PER_GROUP = 8
N_EXPERTS = 32
EXPERT_HIDDEN = 1024
DEPTH = 1
DN_ALPHA = (2 * DEPTH) ** 0.25
LN_EPS = 1e-5
RMS_EPS = 1e-6
LANES = 128

VMEM_LIMIT = 56 * 1024 * 1024

INPROJ_TM = 512
INPROJ_TN = 2816
OUTPROJ_TM = 512
OUTPROJ_CHUNK = 256
ATT_TAIL_STRIPS = 0
ATT_TQ = 512
ATT_TK = 512
SSD_L = 512
MOE_BM = 256
TOK_TB = 512
POS_TB = 1024


def _cparams(sem, flags=None):
    return pltpu.CompilerParams(dimension_semantics=sem, vmem_limit_bytes=VMEM_LIMIT, flags=flags)


def _sigmoid(x):
    return 1.0 / (1.0 + jnp.exp(-x))


def _lane_col(x, idx):
    lane = lax.broadcasted_iota(jnp.int32, x.shape, 1)
    return jnp.sum(jnp.where(lane == idx, x, 0.0), axis=-1, keepdims=True)


def _inproj_kernel(x_ref, w_ref, wdt_ref, o_ref, dt_ref, xb_ref):
    @pl.when(pl.program_id(1) == 0)
    def _():
        xb = x_ref[...].astype(BF16)
        xb_ref[...] = xb
        dt_ref[...] = jnp.dot(xb, wdt_ref[...], preferred_element_type=F32)

    o_ref[...] = jnp.dot(xb_ref[...], w_ref[...], preferred_element_type=F32).astype(o_ref.dtype)


def _in_proj(x2d, w_main, w_dt, tm, tn):
    T, D = x2d.shape
    N = w_main.shape[1]
    return pl.pallas_call(
        _inproj_kernel,
        grid=(T // tm, N // tn),
        in_specs=[
            pl.BlockSpec((tm, D), lambda i, j: (i, 0)),
            pl.BlockSpec((D, tn), lambda i, j: (0, j)),
            pl.BlockSpec((D, LANES), lambda i, j: (0, 0)),
        ],
        out_specs=[
            pl.BlockSpec((tm, tn), lambda i, j: (i, j)),
            pl.BlockSpec((tm, LANES), lambda i, j: (i, 0)),
        ],
        out_shape=[
            jax.ShapeDtypeStruct((T, N), BF16),
            jax.ShapeDtypeStruct((T, LANES), F32),
        ],
        scratch_shapes=[pltpu.VMEM((tm, D), BF16)],
        compiler_params=_cparams(("arbitrary", "arbitrary")),
        name="in_proj",
    )(x2d, w_main, w_dt)


def _attn_kernel(slopes_ref, q_ref, k_ref, v_ref, lamp_ref, nw_ref, o_ref,
                 tab_ref, acc_ref, sa_ref, sb_ref, pa_ref, pb_ref, mxa_ref, mxb_ref, ala_ref, alb_ref,
                 qm_ref, vt_ref, *, lambda_init):
    h = pl.program_id(1)
    tk, tq = tab_ref.shape[1:]
    ratio = tq // tk
    nk = q_ref.shape[0] // tk
    nq = nk // ratio
    slope2 = slopes_ref[h] * LOG2E

    s_rel = lax.broadcasted_iota(jnp.int32, (tk, tq), 0)
    t_rel = lax.broadcasted_iota(jnp.int32, (tk, tq), 1)
    tab_ref[0] = slope2 * s_rel.astype(F32)
    for d in range(ratio):
        s_q = s_rel + d * tk
        allowed = (s_q // CHUNK) <= (t_rel // CHUNK)
        val = slope2 * (t_rel - jnp.abs(t_rel - s_q) - d * tk).astype(F32)
        tab_ref[1 + d] = jnp.where(allowed, val, -jnp.inf)

    def prep(i, c):
        rows = pl.ds(pl.multiple_of(i * tk, tk), tk)
        q = q_ref[rows, :]
        lane = lax.broadcasted_iota(jnp.int32, q.shape, 1)
        qs = (q.astype(F32) * (ATT_HEAD_DIM ** -0.5 * LOG2E)).astype(BF16)
        zero = jnp.zeros_like(qs)
        qm_ref[0, rows, :] = jnp.where(lane < ATT_HEAD_DIM, qs, zero)
        qm_ref[1, rows, :] = jnp.where(lane >= ATT_HEAD_DIM, qs, zero)
        vt_ref[i, :ATT_V_DIM, :] = v_ref[rows, :].astype(F32).T.astype(BF16)
        extra_row = lax.broadcasted_iota(jnp.int32, (ATT_VT_ROWS - ATT_V_DIM, tk), 0)
        vt_ref[i, ATT_V_DIM:, :] = (extra_row == 0).astype(BF16)
        return c

    lax.fori_loop(0, nk, prep, 0)
    nt = (((1,), (1,)), ((), ()))
    lamp = lamp_ref[...]
    lam = (jnp.exp(jnp.sum(lamp[0:1] * lamp[1:2], axis=-1, keepdims=True))
           - jnp.exp(jnp.sum(lamp[2:3] * lamp[3:4], axis=-1, keepdims=True)) + lambda_init)

    half = 256
    ncol = tq // half
    strip = 16

    def last_j(qi):
        return ratio * (qi + 1) - 1

    def scores_piece(pair, s_out, mx_out, m, c):
        qi, j = pair
        cols = slice(c * half, (c + 1) * half)
        kb = k_ref[pl.ds(pl.multiple_of(j * tk, tk), tk), :]
        qh = qm_ref[m, pl.ds(pl.multiple_of(qi * tq + c * half, half), half), :]
        kind = jnp.maximum(j - ratio * qi + 1, 0)
        st = lax.dot_general(kb, qh, nt, preferred_element_type=F32) + tab_ref[kind, :, cols]
        s_out[m, :, cols] = st
        mx_out[m, :, cols] = jnp.max(st, axis=0, keepdims=True)

    def accum_piece(pair, p_in, al_in, m, c):
        cols = slice(c * half, (c + 1) * half)
        acc_ref[m, :, cols] = (al_in[m][:, cols] * acc_ref[m, :, cols]
                               + jnp.dot(vt_ref[pair[1]], p_in[m, :, cols], preferred_element_type=F32))

    def finalize(qi):
        a0 = acc_ref[0]
        a1 = acc_ref[1]
        dv = ATT_V_DIM
        o = a0[:dv] / a0[dv:dv + 1] - lam * (a1[:dv] / a1[dv:dv + 1])
        ms2 = jnp.mean(o * o, axis=0, keepdims=True)
        o = o * lax.rsqrt(ms2 + RMS_EPS) * nw_ref[...] * (1.0 - lambda_init)
        o_ref[pl.ds(pl.multiple_of(qi * tq, tq), tq), :] = o.T.astype(o_ref.dtype)

    def next_pair(pair):
        qi, j = pair
        wrap = j == last_j(qi)
        return jnp.where(wrap, qi + 1, qi), jnp.where(wrap, 0, j + 1)

    def trip(cur_set, nxt_set, state):
        s_c, mx_c, p_c, al_c = cur_set
        s_n, mx_n, p_n, al_n = nxt_set
        prv, cur, m_old = state
        nxt = next_pair(cur)
        nxt_c = (jnp.minimum(nxt[0], nq - 1), jnp.where(nxt[0] >= nq, nk - 1, nxt[1]))
        prv_c = (prv[0], jnp.maximum(prv[1], 0))
        cj = -slope2 * (cur[0] * tq - cur[1] * tk).astype(F32)
        refs, m_out = [], []
        for m in range(2):
            m_prev = jnp.where(cur[1] == 0, -jnp.inf, m_old[m])
            m_new = jnp.maximum(m_prev, mx_c[m] + cj)
            al_c[m] = jnp.exp2(m_prev - m_new)
            refs.append(m_new - cj)
            m_out.append(m_new)

        acc_p = [functools.partial(accum_piece, prv_c, p_n, al_n, m, c) for m in range(2) for c in range(ncol)]
        sco_p = [functools.partial(scores_piece, nxt_c, s_n, mx_n, m, c) for m in range(2) for c in range(ncol)]
        pieces = acc_p[:ncol] + sco_p + acc_p[ncol:]
        sw = 512
        strips = [(m, r, w) for m in range(2) for r in range(tk // strip) for w in range(tq // sw)]
        def prob_strip(m, r, w):
            rows = slice(r * strip, (r + 1) * strip)
            cols = slice(w * sw, (w + 1) * sw)
            p_c[m, rows, cols] = jnp.exp2(s_c[m, rows, cols] - refs[m][:, cols]).astype(BF16)

        per = (len(strips) - ATT_TAIL_STRIPS) // len(pieces)
        for g, piece in enumerate(pieces):
            piece()
            for s in strips[g * per:(g + 1) * per]:
                prob_strip(*s)
        for s in strips[len(pieces) * per:]:
            prob_strip(*s)

        @pl.when(prv[1] == last_j(prv[0]))
        def _():
            finalize(prv[0])

        return cur, nxt, tuple(m_out)

    set_a = (sa_ref, mxa_ref, pa_ref, ala_ref)
    set_b = (sb_ref, mxb_ref, pb_ref, alb_ref)
    acc_ref[...] = jnp.zeros_like(acc_ref)
    pb_ref[...] = jnp.zeros_like(pb_ref)
    alb_ref[...] = jnp.ones_like(alb_ref)
    zero_i = jnp.int32(0)
    for m in range(2):
        for c in range(ncol):
            scores_piece((zero_i, zero_i), sa_ref, mxa_ref, m, c)

    def body(t, state):
        return trip(set_b, set_a, trip(set_a, set_b, state))

    n_pairs = ratio * nq * (nq + 1) // 2
    m_init = jnp.full((1, tq), -jnp.inf, F32)
    first = ((zero_i, jnp.int32(-1)), (zero_i, zero_i), (m_init, m_init))
    state = lax.fori_loop(0, n_pairs // 2, body, first)
    if n_pairs % 2:
        trip(set_a, set_b, state)

    last_set = set_a if (n_pairs - 1) % 2 == 0 else set_b
    last_pair = (jnp.int32(nq - 1), jnp.int32(nk - 1))
    for m in range(2):
        for c in range(ncol):
            accum_piece(last_pair, last_set[2], last_set[3], m, c)
    finalize(last_pair[0])


def _diff_attention(proj, slopes, lamp, nw_col, B, S, lambda_init):
    T = B * S
    tq, tk = ATT_TQ, ATT_TK
    nk = S // tk
    kern = functools.partial(_attn_kernel, lambda_init=lambda_init)
    grid_spec = pltpu.PrefetchScalarGridSpec(
        num_scalar_prefetch=1,
        grid=(B, ATT_HEADS),
        in_specs=[
            pl.BlockSpec((S, LANES), lambda b, h, s: (b, h)),
            pl.BlockSpec((S, LANES), lambda b, h, s: (b, N_Q // LANES + h)),
            pl.BlockSpec((S, LANES), lambda b, h, s: (b, (N_Q + N_K) // LANES + h)),
            pl.BlockSpec((8, LANES), lambda b, h, s: (0, 0)),
            pl.BlockSpec((ATT_V_DIM, 1), lambda b, h, s: (0, 0)),
        ],
        out_specs=pl.BlockSpec((S, ATT_V_DIM), lambda b, h, s: (b, h)),
        scratch_shapes=[
            pltpu.VMEM((1 + tq // tk, tk, tq), F32),
            pltpu.VMEM((2, ATT_VT_ROWS, tq), F32),
            pltpu.VMEM((2, tk, tq), F32), pltpu.VMEM((2, tk, tq), F32),
            pltpu.VMEM((2, tk, tq), BF16), pltpu.VMEM((2, tk, tq), BF16),
            pltpu.VMEM((2, 1, tq), F32), pltpu.VMEM((2, 1, tq), F32),
            pltpu.VMEM((2, 1, tq), F32), pltpu.VMEM((2, 1, tq), F32),
            pltpu.VMEM((2, S, LANES), BF16),
            pltpu.VMEM((nk, ATT_VT_ROWS, tk), BF16),
        ],
    )
    return pl.pallas_call(
        kern,
        grid_spec=grid_spec,
        out_shape=jax.ShapeDtypeStruct((T, ATT_WIDTH), BF16),
        compiler_params=_cparams(("arbitrary", "arbitrary")),
        name="diff_attention",
    )(slopes, proj, proj, proj, lamp, nw_col)


def _expand_heads(v, e):
    hi = v.astype(BF16)
    lo = (v - hi.astype(F32)).astype(BF16)
    return jnp.dot(hi, e, preferred_element_type=F32) + jnp.dot(lo, e, preferred_element_type=F32)


def _ssd_kernel(z_ref, xs_ref, b_ref, c_ref, dt_ref, cw_ref, cb_ref, dtb_ref, alog_ref,
                dskip_ref, nw_ref, e_ref, *rest, n_side):
    side_in = rest[:n_side]
    o_ref = rest[n_side]
    side_out = rest[n_side + 1:2 * n_side + 1]
    ext_ref, st_ref, y_ref = rest[2 * n_side + 1:]
    for w_in_ref, w_out_ref in zip(side_in, side_out):
        w_out_ref[...] = w_in_ref[...].astype(w_out_ref.dtype)
    blk = pl.program_id(1)
    L = z_ref.shape[0]
    nchunk = L // CHUNK
    gw = SSM_WIDTH // SSM_GROUPS
    hpg = SSM_HEADS // SSM_GROUPS

    @pl.when(blk == 0)
    def _():
        ext_ref[0:8, :] = jnp.zeros((8, N_XBC), F32)
        st_ref[...] = jnp.zeros_like(st_ref)

    cur = jnp.concatenate([xs_ref[...], b_ref[...], c_ref[...]], axis=1).astype(F32)
    ext_ref[8:, :] = cur
    cw = cw_ref[...]
    conv = cb_ref[...] + cw[3:4] * cur
    for j in range(SSM_CONV - 1):
        conv = conv + cw[j:j + 1] * ext_ref[pl.ds(8 - (SSM_CONV - 1) + j, L), :]
    ext_ref[0:8, :] = cur[L - 8:, :]
    xbc = conv * _sigmoid(conv)
    xs = xbc[:, :SSM_WIDTH]
    bmb = xbc[:, SSM_WIDTH:SSM_WIDTH + SSM_GROUPS * SSM_STATE].astype(BF16)
    cmb = xbc[:, SSM_WIDTH + SSM_GROUPS * SSM_STATE:].astype(BF16)
    xsb = xs.astype(BF16)

    lane1 = lax.broadcasted_iota(jnp.int32, (1, LANES), 1)
    dtx = dt_ref[...] + dtb_ref[...]
    dtp = jnp.maximum(dtx, 0.0) + jnp.log1p(jnp.exp(-jnp.abs(dtx)))
    a_head = jnp.where(lane1 < SSM_HEADS, -jnp.exp(alog_ref[...]), 0.0)
    acs = dtp * a_head
    row_in_chunk = lax.broadcasted_iota(jnp.int32, (L, LANES), 0) & (CHUNK - 1)
    k = 1
    while k < CHUNK:
        acs = acs + jnp.where(row_in_chunk >= k, pltpu.roll(acs, k, axis=0), 0.0)
        k *= 2
    acs_t = acs.T
    dt_t = dtp.T

    e = e_ref[...]
    acs_last = jnp.concatenate(
        [jnp.broadcast_to(acs[c * CHUNK + CHUNK - 1:(c + 1) * CHUNK, :], (CHUNK, LANES)) for c in range(nchunk)],
        axis=0)
    w_exp = _expand_heads(dtp * jnp.exp(acs_last - acs), e)
    od_exp = _expand_heads(jnp.exp(acs), e)
    row8 = lax.broadcasted_iota(jnp.int32, (8, LANES), 0)
    cd8 = jnp.zeros((8, LANES), F32)
    for c in range(nchunk):
        cd8 = jnp.where(row8 == c, jnp.exp(acs[c * CHUNK + CHUNK - 1:(c + 1) * CHUNK, :]), cd8)
    cd_exp = _expand_heads(cd8, e)
    xw = (xs * w_exp).astype(BF16)

    tn = (((0,), (0,)), ((), ()))
    for c in range(nchunk):
        r0 = c * CHUNK
        for g in range(SSM_GROUPS):
            st = st_ref[g]
            cg = cmb[r0:r0 + CHUNK, g * SSM_STATE:(g + 1) * SSM_STATE]
            bg = bmb[r0:r0 + CHUNK, g * SSM_STATE:(g + 1) * SSM_STATE]
            y_ref[r0:r0 + CHUNK, g * gw:(g + 1) * gw] = jnp.dot(cg, st.astype(BF16), preferred_element_type=F32)
            snew = lax.dot_general(bg, xw[r0:r0 + CHUNK, g * gw:(g + 1) * gw], tn, preferred_element_type=F32)
            st_ref[g] = st * cd_exp[c:c + 1, g * gw:(g + 1) * gw] + snew
    y = y_ref[...] * od_exp + xs * dskip_ref[...]

    pair = 2 * CHUNK
    li = lax.broadcasted_iota(jnp.int32, (pair, pair), 0)
    si = lax.broadcasted_iota(jnp.int32, (pair, pair), 1)
    mask2 = (li >= si) & ((si >= CHUNK) | (li < CHUNK))
    lanep = lax.broadcasted_iota(jnp.int32, (pair, LANES), 1)
    nt = (((1,), (1,)), ((), ()))
    for pp in range(L // pair):
        r0 = pp * pair
        acs_p = acs[r0:r0 + pair, :]
        for g in range(SSM_GROUPS):
            cb2 = lax.dot_general(cmb[r0:r0 + pair, g * SSM_STATE:(g + 1) * SSM_STATE],
                                  bmb[r0:r0 + pair, g * SSM_STATE:(g + 1) * SSM_STATE],
                                  nt, preferred_element_type=F32)
            for hh in range(hpg // 2):
                hp = g * (hpg // 2) + hh
                mats = []
                for u in range(2):
                    hd = 2 * hp + u
                    seg = _lane_col(acs_p, hd) - acs_t[hd:hd + 1, r0:r0 + pair]
                    decay = jnp.exp(jnp.where(mask2, seg, -jnp.inf))
                    mats.append((cb2 * decay * dt_t[hd:hd + 1, r0:r0 + pair]).astype(BF16))
                lhs = jnp.concatenate(mats, axis=1)
                xp = xsb[r0:r0 + pair, hp * LANES:(hp + 1) * LANES]
                zero = jnp.zeros_like(xp)
                rhs = jnp.concatenate([jnp.where(lanep < SSM_HEAD_DIM, xp, zero),
                                       jnp.where(lanep >= SSM_HEAD_DIM, xp, zero)], axis=0)
                y_ref[r0:r0 + pair, hp * LANES:(hp + 1) * LANES] = jnp.dot(lhs, rhs, preferred_element_type=F32)
    y = y + y_ref[...]

    z = z_ref[...].astype(F32)
    y = y * (z * _sigmoid(z))
    outs = []
    for g in range(SSM_GROUPS):
        yg = y[:, g * gw:(g + 1) * gw]
        outs.append(yg * lax.rsqrt(jnp.mean(yg * yg, axis=-1, keepdims=True) + RMS_EPS))
    o_ref[...] = (jnp.concatenate(outs, axis=1) * nw_ref[...]).astype(o_ref.dtype)


def _ssd_mixer(proj, dt_raw, cw8, cb, dtb, alog, dskip_exp, nw, e_mat, B, S, side=()):
    T = B * S
    L = SSD_L
    nb = S // L
    row = lambda b, i: b * nb + i
    col0 = (N_Q + N_K + N_V + N_Z)
    const = lambda shape: pl.BlockSpec(shape, lambda b, i: (0, 0))
    steps = B * nb
    assert all(w.shape[0] % (8 * steps) == 0 for w in side)
    side_specs = [pl.BlockSpec((w.shape[0] // steps, w.shape[1]), lambda b, i: (row(b, i), 0)) for w in side]
    return pl.pallas_call(
        functools.partial(_ssd_kernel, n_side=len(side)),
        grid=(B, nb),
        in_specs=[
            pl.BlockSpec((L, N_Z), lambda b, i: (row(b, i), (N_Q + N_K + N_V) // N_Z)),
            pl.BlockSpec((L, SSM_WIDTH), lambda b, i: (row(b, i), col0 // SSM_WIDTH)),
            pl.BlockSpec((L, 256), lambda b, i: (row(b, i), (col0 + SSM_WIDTH) // 256)),
            pl.BlockSpec((L, 256), lambda b, i: (row(b, i), (col0 + SSM_WIDTH + 256) // 256)),
            pl.BlockSpec((L, LANES), lambda b, i: (row(b, i), 0)),
            const((8, N_XBC)),
            const((1, N_XBC)),
            const((1, LANES)),
            const((1, LANES)),
            const((1, SSM_WIDTH)),
            const((1, SSM_WIDTH)),
            const((LANES, SSM_WIDTH)),
        ] + side_specs,
        out_specs=[pl.BlockSpec((L, SSM_WIDTH), lambda b, i: (row(b, i), 0))] + side_specs,
        out_shape=[jax.ShapeDtypeStruct((T, SSM_WIDTH), BF16)]
        + [jax.ShapeDtypeStruct(w.shape, BF16) for w in side],
        scratch_shapes=[
            pltpu.VMEM((L + 8, N_XBC), F32),
            pltpu.VMEM((SSM_GROUPS, SSM_STATE, SSM_WIDTH // SSM_GROUPS), F32),
            pltpu.VMEM((L, SSM_WIDTH), F32),
        ],
        compiler_params=_cparams(("arbitrary", "arbitrary")),
        name="ssd_mixer",
    )(proj, proj, proj, proj, dt_raw, cw8, cb, dtb, alog, dskip_exp, nw, e_mat, *side)


def _layer_norm_rows(r, g, b):
    mu = jnp.mean(r, axis=-1, keepdims=True)
    d = r - mu
    var = jnp.mean(d * d, axis=-1, keepdims=True)
    return d * lax.rsqrt(var + LN_EPS) * g + b


def _outproj_kernel(att_ref, ssm_ref, x_ref, wa_ref, ws_ref, g_ref, b_ref, wr_ref, br_ref,
                    h_ref, eid_ref, ew_ref):
    chunk = min(OUTPROJ_CHUNK, h_ref.shape[0])
    for c in range(h_ref.shape[0] // chunk):
        rows = slice(c * chunk, (c + 1) * chunk)
        mix = (jnp.dot(att_ref[rows, :], wa_ref[...], preferred_element_type=F32)
               + jnp.dot(ssm_ref[rows, :], ws_ref[...], preferred_element_type=F32))
        h = _layer_norm_rows(DN_ALPHA * x_ref[rows, :] + mix, g_ref[...], b_ref[...])
        h_ref[rows, :] = h
        eid, ew = _route(h, wr_ref[...], br_ref[...])
        eid_ref[rows, :] = eid
        ew_ref[rows, :] = ew


def _route(h, wr, br):
    logits = jnp.dot(h.astype(BF16), wr, preferred_element_type=F32) + br
    lane = lax.broadcasted_iota(jnp.int32, logits.shape, 1)
    lanef = lane.astype(F32)
    big = float(LANES)
    gl = jnp.where(lane < N_EXPERT_GROUPS, logits, -jnp.inf)
    gmax = jnp.max(gl, axis=-1, keepdims=True)
    g_prob = 1.0 / jnp.sum(jnp.exp(gl - gmax), axis=-1, keepdims=True)
    gidx = jnp.min(jnp.where(gl == gmax, lanef, big), axis=-1, keepdims=True)
    lo = N_EXPERT_GROUPS + EXPERTS_PER_GROUP * gidx
    el = jnp.where((lanef >= lo) & (lanef < lo + EXPERTS_PER_GROUP), logits, -jnp.inf)
    t1 = jnp.max(el, axis=-1, keepdims=True)
    i1 = jnp.min(jnp.where(el == t1, lanef, big), axis=-1, keepdims=True)
    el2 = jnp.where(lanef == i1, -jnp.inf, el)
    t2 = jnp.max(el2, axis=-1, keepdims=True)
    i2 = jnp.min(jnp.where(el2 == t2, lanef, big), axis=-1, keepdims=True)
    ex = jnp.exp(t2 - t1)
    w1 = g_prob / (1.0 + ex)
    w2 = g_prob * ex / (1.0 + ex)
    eid = jnp.where(lane == 0, i1 - N_EXPERT_GROUPS, jnp.where(lane == 1, i2 - N_EXPERT_GROUPS, 0.0))
    return eid.astype(jnp.int32), jnp.where(lane == 0, w1, jnp.where(lane == 1, w2, 0.0))


def _out_proj(att, ssm, x2d, wa, ws, g, b, wr, br, tm):
    T, D = x2d.shape
    const = lambda shape: pl.BlockSpec(shape, lambda i: (0, 0))
    rows = lambda w: pl.BlockSpec((tm, w), lambda i: (i, 0))
    return pl.pallas_call(
        _outproj_kernel,
        grid=(T // tm,),
        in_specs=[rows(ATT_WIDTH), rows(SSM_WIDTH), rows(D), const((ATT_WIDTH, D)), const((SSM_WIDTH, D)),
                  const((1, D)), const((1, D)), const((D, LANES)), const((1, LANES))],
        out_specs=[rows(D), rows(LANES), rows(LANES)],
        out_shape=[jax.ShapeDtypeStruct((T, D), F32),
                   jax.ShapeDtypeStruct((T, LANES), jnp.int32),
                   jax.ShapeDtypeStruct((T, LANES), F32)],
        compiler_params=_cparams(("arbitrary",)),
        name="out_proj_ln1_router",
    )(att, ssm, x2d, wa, ws, g, b, wr, br)


def _pos_kernel(eid_ref, stril_ref, dest_ref, pend_ref, tot_ref, run_ref, pstart_ref):
    ph = pl.program_id(0)
    i = pl.program_id(1)
    tb = eid_ref.shape[0]
    lane = lax.broadcasted_iota(jnp.int32, (tb, LANES), 1)
    lanef = lane.astype(F32)
    ef = eid_ref[...].astype(F32)
    oh1 = (lanef == _lane_col(ef, 0)).astype(F32)
    oh2 = (lanef == _lane_col(ef, 1)).astype(F32)
    cnt = oh1 + oh2

    @pl.when((ph == 0) & (i == 0))
    def _():
        tot_ref[...] = jnp.zeros_like(tot_ref)

    @pl.when(ph == 0)
    def _():
        tot_ref[...] += jnp.sum(cnt, axis=0, keepdims=True)

    @pl.when((ph == 1) & (i == 0))
    def _():
        tot = jnp.broadcast_to(tot_ref[...], (8, LANES))
        padded = jnp.floor((tot + (MOE_BM - 1)) * (1.0 / MOE_BM)) * MOE_BM
        lane8 = lax.broadcasted_iota(jnp.int32, (8, LANES), 1)
        ends = padded
        k = 1
        while k < LANES:
            ends = ends + jnp.where(lane8 >= k, pltpu.roll(ends, k, axis=1), 0.0)
            k *= 2
        row8 = lax.broadcasted_iota(jnp.int32, (8, LANES), 0)
        pend_ref[...] = jnp.where(row8 == 0, ends, jnp.where(row8 == 1, tot, 0.0))
        pstart_ref[...] = (ends - padded)[0:1]
        run_ref[...] = jnp.zeros_like(run_ref)

    @pl.when(ph == 1)
    def _():
        pre = jnp.dot(stril_ref[...], cnt.astype(BF16), preferred_element_type=F32)
        slot = pstart_ref[...] + run_ref[...] + pre
        d1 = jnp.sum(oh1 * slot, axis=-1, keepdims=True)
        d2 = jnp.sum(oh2 * slot, axis=-1, keepdims=True)
        dmat = jnp.where(lane == 0, d1, jnp.where(lane == 1, d2, 0.0))
        dest_ref[0] = dmat.T[:8, :].astype(jnp.int32)
        run_ref[...] += jnp.sum(cnt, axis=0, keepdims=True)


def _positions(eid, stril):
    T = eid.shape[0]
    tb = POS_TB
    return pl.pallas_call(
        _pos_kernel,
        grid=(2, T // tb),
        in_specs=[pl.BlockSpec((tb, LANES), lambda p, i: (i, 0)),
                  pl.BlockSpec((tb, tb), lambda p, i: (0, 0))],
        out_specs=[pl.BlockSpec((1, 8, tb), lambda p, i: (i * p, 0, 0)),
                   pl.BlockSpec((8, LANES), lambda p, i: (0, 0))],
        out_shape=[jax.ShapeDtypeStruct((T // tb, 8, tb), jnp.int32),
                   jax.ShapeDtypeStruct((8, LANES), F32)],
        scratch_shapes=[pltpu.VMEM((1, LANES), F32), pltpu.VMEM((1, LANES), F32), pltpu.VMEM((1, LANES), F32)],
        compiler_params=_cparams(("arbitrary", "arbitrary")),
        name="dispatch_positions",
    )(eid, stril)


def _row_copy(src, s, dst, d, sem):
    return pltpu.make_async_copy(src.at[pl.ds(s, 1)], dst.at[pl.ds(d, 1)], sem)


def _scatter_kernel(zstart_ref, zcnt_ref, dest_ref, h_hbm, xs_hbm, zrow_ref, hbuf, in_sems, out_sems, zsem):
    i = pl.program_id(0)
    n = pl.num_programs(0)
    tb = hbuf.shape[1]

    @pl.when(i == 0)
    def _():
        zrow_ref[...] = jnp.zeros_like(zrow_ref)

        def fill(e, c):
            lax.fori_loop(0, zcnt_ref[e],
                          lambda r, c2: (_row_copy(zrow_ref, 0, xs_hbm, zstart_ref[e] + r, zsem).start(), c2)[1], 0)
            return c

        def fill_wait(e, c):
            lax.fori_loop(0, zcnt_ref[e],
                          lambda r, c2: (_row_copy(zrow_ref, 0, xs_hbm, 0, zsem).wait(), c2)[1], 0)
            return c

        nblk = xs_hbm.shape[0] // MOE_BM
        first_free = zstart_ref[N_EXPERTS]

        def tail_copy(b):
            return pltpu.make_async_copy(zrow_ref, xs_hbm.at[pl.ds(b * MOE_BM, MOE_BM)], zsem)

        lax.fori_loop(0, N_EXPERTS, fill, 0)
        lax.fori_loop(first_free, nblk, lambda b, c: (tail_copy(b).start(), c)[1], 0)
        lax.fori_loop(0, N_EXPERTS, fill_wait, 0)
        lax.fori_loop(first_free, nblk, lambda b, c: (tail_copy(b).wait(), c)[1], 0)

    slot = lax.rem(i, 3)
    nxt = lax.rem(i + 1, 3)
    prv = lax.rem(i + 2, 3)

    def load(b, s):
        return pltpu.make_async_copy(h_hbm.at[pl.ds(pl.multiple_of(b * tb, tb), tb)], hbuf.at[s], in_sems.at[s])

    def wait_rows(s):
        for _ in range(2):
            pltpu.make_async_copy(hbuf.at[s], xs_hbm.at[pl.ds(0, tb)], out_sems.at[s]).wait()

    @pl.when(i == 0)
    def _():
        load(0, 0).start()

    @pl.when(i + 1 < n)
    def _():
        load(i + 1, nxt).start()

    load(i, slot).wait()

    def issue(r, c):
        _row_copy(hbuf.at[slot], r, xs_hbm, dest_ref[0, 0, r], out_sems.at[slot]).start(priority=0)
        _row_copy(hbuf.at[slot], r, xs_hbm, dest_ref[0, 1, r], out_sems.at[slot]).start(priority=1)
        return c

    lax.fori_loop(0, tb, issue, 0, unroll=8)

    @pl.when(i > 0)
    def _():
        wait_rows(prv)

    @pl.when(i == n - 1)
    def _():
        wait_rows(slot)


def _dest_spec(dest, tb, step_of):
    per = dest.shape[2] // tb
    return pl.BlockSpec((1, 8, tb), lambda i, *_: (step_of(i) // per, 0, step_of(i) % per), memory_space=pltpu.SMEM)


def _dispatch(zstart, zcnt, dest, h, cap):
    T, D = h.shape
    tb = TOK_TB
    nb = T // tb
    grid_spec = pltpu.PrefetchScalarGridSpec(
        num_scalar_prefetch=2,
        grid=(nb,),
        in_specs=[_dest_spec(dest, tb, lambda i: i),
                  pl.BlockSpec(memory_space=pl.ANY)],
        out_specs=pl.BlockSpec(memory_space=pl.ANY),
        scratch_shapes=[pltpu.VMEM((MOE_BM, D), h.dtype), pltpu.VMEM((3, tb, D), h.dtype),
                        pltpu.SemaphoreType.DMA((3,)), pltpu.SemaphoreType.DMA((3,)), pltpu.SemaphoreType.DMA],
    )
    return pl.pallas_call(
        _scatter_kernel,
        grid_spec=grid_spec,
        out_shape=jax.ShapeDtypeStruct((cap, D), h.dtype),
        compiler_params=_cparams(("arbitrary",)),
        name="moe_dispatch",
    )(zstart, zcnt, dest, h)


def _expert_kernel(bexp_ref, nused_ref, x_ref, wg_ref, wu_ref, wd_ref, y_ref):
    i = pl.program_id(0)

    @pl.when(i < nused_ref[0])
    def _():
        x = x_ref[...].astype(BF16)
        gate = jnp.dot(x, wg_ref[0], preferred_element_type=F32)
        up = jnp.dot(x, wu_ref[0], preferred_element_type=F32)
        hid = (gate * _sigmoid(gate) * up).astype(BF16)
        y_ref[...] = jnp.dot(hid, wd_ref[0], preferred_element_type=F32)

    @pl.when(i >= nused_ref[0])
    def _():
        y_ref[...] = jnp.zeros_like(y_ref)


def _expert_mlp(bexp, nused, xs, wg, wu, wd):
    cap, D = xs.shape
    bm = MOE_BM
    H = wg.shape[2]
    grid_spec = pltpu.PrefetchScalarGridSpec(
        num_scalar_prefetch=2,
        grid=(cap // bm,),
        in_specs=[
            pl.BlockSpec((bm, D), lambda i, be, nu: (i, 0)),
            pl.BlockSpec((1, D, H), lambda i, be, nu: (be[i], 0, 0)),
            pl.BlockSpec((1, D, H), lambda i, be, nu: (be[i], 0, 0)),
            pl.BlockSpec((1, H, D), lambda i, be, nu: (be[i], 0, 0)),
        ],
        out_specs=pl.BlockSpec((bm, D), lambda i, be, nu: (i, 0)),
    )
    return pl.pallas_call(
        _expert_kernel,
        grid_spec=grid_spec,
        out_shape=jax.ShapeDtypeStruct((cap, D), F32),
        compiler_params=_cparams(("arbitrary",)),
        name="expert_mlp",
    )(bexp, nused, xs, wg, wu, wd)


def _combine_kernel(dest_ref, dnext_ref, h_ref, ew_ref, g_ref, b_ref, y_hbm, o_ref, ybuf, sems):
    i = pl.program_id(0)
    n = pl.num_programs(0)
    tb = h_ref.shape[0]
    slot = i & 1

    def gather(d_ref, s, unroll):
        def issue(r, c):
            _row_copy(y_hbm, d_ref[0, 0, r], ybuf.at[s, 0], r, sems.at[s]).start(priority=0)
            _row_copy(y_hbm, d_ref[0, 1, r], ybuf.at[s, 1], r, sems.at[s]).start(priority=1)
            return c
        lax.fori_loop(0, tb, issue, 0, unroll=unroll)

    def wait_slot(s):
        for k in range(2):
            pltpu.make_async_copy(y_hbm.at[pl.ds(0, tb)], ybuf.at[s, k], sems.at[s]).wait()

    @pl.when(i == 0)
    def _():
        gather(dest_ref, slot, 8)

    wait_slot(slot)
    gather(dnext_ref, 1 - slot, tb)
    ew = ew_ref[...]
    ffn = _lane_col(ew, 0) * ybuf[slot, 0] + _lane_col(ew, 1) * ybuf[slot, 1]
    o_ref[...] = _layer_norm_rows(DN_ALPHA * h_ref[...] + ffn, g_ref[...], b_ref[...])

    @pl.when(i == n - 1)
    def _():
        wait_slot(1 - slot)


def _combine(dest, h, ew, g, b, y):
    T, D = h.shape
    tb = TOK_TB
    nb = T // tb
    return pl.pallas_call(
        _combine_kernel,
        grid=(nb,),
        in_specs=[_dest_spec(dest, tb, lambda i: i),
                  _dest_spec(dest, tb, lambda i: jnp.minimum(i + 1, nb - 1)),
                  pl.BlockSpec((tb, D), lambda i: (i, 0)),
                  pl.BlockSpec((tb, LANES), lambda i: (i, 0)),
                  pl.BlockSpec((1, D), lambda i: (0, 0)),
                  pl.BlockSpec((1, D), lambda i: (0, 0)),
                  pl.BlockSpec(memory_space=pl.ANY)],
        out_specs=pl.BlockSpec((tb, D), lambda i: (i, 0)),
        out_shape=jax.ShapeDtypeStruct((T, D), F32),
        scratch_shapes=[pltpu.VMEM((2, 2, tb, D), F32), pltpu.SemaphoreType.DMA((2,))],
        compiler_params=_cparams(("arbitrary",)),
        name="moe_combine_ln2",
    )(dest, dest, h, ew, g, b, y)


def _pad_lanes(v, n=LANES):
    v = v.reshape(1, -1).astype(F32)
    return jnp.pad(v, ((0, 0), (0, n - v.shape[1])))


def kernel(x, w_in, lambda_q1, lambda_k1, lambda_q2, lambda_k2, attn_norm_w, conv_w, conv_b, dt_bias, a_log, d_skip, ssm_norm_w, w_out, ln1_g, ln1_b, w_router_group, b_router_group, w_router_expert, b_router_expert, w_gate, w_up, w_down, ln2_g, ln2_b):
    B, S, D = x.shape
    T = B * S
    assert w_in.shape[0] == DEPTH == 1
    assert S % ATT_TQ == 0 and S % SSD_L == 0 and T % POS_TB == 0 and POS_TB % TOK_TB == 0
    l = 0
    lambda_init = 0.8 - 0.6 * math.exp(-0.3 * l)
    x2d = x.reshape(T, D)

    w_main = w_in[l][:, :N_MAIN].astype(BF16)
    w_dt = jnp.pad(w_in[l][:, N_MAIN:], ((0, 0), (0, LANES - N_DT))).astype(BF16)
    slopes = jnp.exp2(-8.0 * jnp.arange(1, ATT_HEADS + 1, dtype=F32) / ATT_HEADS)
    lamp = jnp.concatenate([_pad_lanes(lambda_q1[l]), _pad_lanes(lambda_k1[l]),
                            _pad_lanes(lambda_q2[l]), _pad_lanes(lambda_k2[l]),
                            jnp.zeros((4, LANES), F32)], axis=0)
    nw_col = attn_norm_w[l].astype(F32).reshape(ATT_V_DIM, 1)
    cw8 = jnp.pad(conv_w[l].astype(F32), ((0, 8 - SSM_CONV), (0, 0)))
    cb = conv_b[l].astype(F32).reshape(1, N_XBC)
    dskip_exp = jnp.repeat(d_skip[l].astype(F32), SSM_HEAD_DIM).reshape(1, SSM_WIDTH)
    ssm_nw = ssm_norm_w[l].astype(F32).reshape(1, SSM_WIDTH)
    head_of_lane = jnp.arange(SSM_WIDTH, dtype=jnp.int32) // SSM_HEAD_DIM
    e_mat = (jnp.arange(LANES, dtype=jnp.int32)[:, None] == head_of_lane[None, :]).astype(BF16)
    wa = w_out[l][:ATT_WIDTH].astype(BF16)
    ws = w_out[l][ATT_WIDTH:].astype(BF16)
    wr = jnp.concatenate(
        [w_router_group[l], jnp.transpose(w_router_expert[l], (1, 0, 2)).reshape(D, N_EXPERTS)], axis=1)
    wr = jnp.pad(wr, ((0, 0), (0, LANES - wr.shape[1]))).astype(BF16)
    br = _pad_lanes(jnp.concatenate([b_router_group[l], b_router_expert[l].reshape(-1)]))
    row = lambda v: v.astype(F32).reshape(1, D)

    tm_in = INPROJ_TM if T % INPROJ_TM == 0 else 256
    proj, dt_raw = _in_proj(x2d, w_main, w_dt, tm_in, INPROJ_TN)
    att = _diff_attention(proj, slopes, lamp, nw_col, B, S, lambda_init)
    experts_f32 = [w[l].astype(F32).reshape(-1, w.shape[-1]) for w in (w_gate, w_up, w_down)]
    ssm, wg_b, wu_b, wd_b = _ssd_mixer(proj, dt_raw, cw8, cb, _pad_lanes(dt_bias[l]), _pad_lanes(a_log[l]),
                                       dskip_exp, ssm_nw, e_mat, B, S, side=experts_f32)
    wg_b, wu_b, wd_b = (w.reshape(src.shape[1:]) for w, src in zip((wg_b, wu_b, wd_b), (w_gate, w_up, w_down)))
    h1, eid, ew = _out_proj(att, ssm, x2d, wa, ws, row(ln1_g[l]), row(ln1_b[l]), wr, br,
                            OUTPROJ_TM if T % OUTPROJ_TM == 0 else 256)

    stril = (jnp.arange(POS_TB)[:, None] > jnp.arange(POS_TB)[None, :]).astype(BF16)
    dest, pend = _positions(eid, stril)
    nblk = (T * 2) // MOE_BM + N_EXPERTS
    cap = nblk * MOE_BM
    pad_ends = pend[0, :N_EXPERTS].astype(jnp.int32)
    blk_start = jnp.arange(nblk, dtype=jnp.int32) * MOE_BM
    nused = (pad_ends[N_EXPERTS - 1] // MOE_BM).astype(jnp.int32)
    last_used = jnp.maximum(nused - 1, 0) * MOE_BM
    bexp = jnp.sum(pad_ends[None, :] <= jnp.minimum(blk_start, last_used)[:, None], axis=1).astype(jnp.int32)
    bexp = jnp.minimum(bexp, N_EXPERTS - 1)
    counts = pend[1, :N_EXPERTS].astype(jnp.int32)
    padded = pad_ends - jnp.concatenate([jnp.zeros((1,), jnp.int32), pad_ends[:-1]])
    zstart = jnp.concatenate([pad_ends - padded + counts, nused.reshape(1)])
    xs_sorted = _dispatch(zstart, padded - counts, dest, h1, cap)
    y_sorted = _expert_mlp(bexp, nused.reshape(1), xs_sorted, wg_b, wu_b, wd_b)
    out = _combine(dest, h1, ew, row(ln2_g[l]), row(ln2_b[l]), y_sorted)
    return out.reshape(B, S, D)
```

```python
import functools
import math

import jax
import jax.numpy as jnp
from jax import lax
from jax.experimental import pallas as pl
from jax.experimental.pallas import tpu as pltpu

F32 = jnp.float32
BF16 = jnp.bfloat16

CHUNK = 64
ATT_HEADS = 8
ATT_HEAD_DIM = 64
ATT_V_DIM = 128
ATT_WIDTH = ATT_HEADS * ATT_V_DIM
ATT_VT_ROWS = ATT_V_DIM + 16
LOG2E = 1.4426950408889634
SSM_HEADS = 16
SSM_HEAD_DIM = 64
SSM_WIDTH = SSM_HEADS * SSM_HEAD_DIM
SSM_GROUPS = 2
SSM_STATE = 128
SSM_CONV = 4
N_Q = 1024
N_K = 1024
N_V = 1024
N_Z = 1024
N_XBC = SSM_WIDTH + 2 * SSM_GROUPS * SSM_STATE
N_DT = SSM_HEADS
N_MAIN = N_Q + N_K + N_V + N_Z + N_XBC
N_EXPERT_GROUPS = 4
EXPERTS_PER_GROUP = 8
N_EXPERTS = 32
EXPERT_HIDDEN = 1024
DEPTH = 1
DN_ALPHA = (2 * DEPTH) ** 0.25
LN_EPS = 1e-5
RMS_EPS = 1e-6
LANES = 128

VMEM_LIMIT = 56 * 1024 * 1024

INPROJ_TM = 512
INPROJ_TN = 2816
OUTPROJ_TM = 512
OUTPROJ_CHUNK = 256
ATT_TAIL_STRIPS = 0
ATT_TQ = 512
ATT_TK = 512
SSD_L = 512
MOE_BM = 512
TOK_TB = 512
POS_TB = 1024


def _cparams(sem, flags=None):
    return pltpu.CompilerParams(dimension_semantics=sem, vmem_limit_bytes=VMEM_LIMIT, flags=flags)


def _sigmoid(x):
    return 1.0 / (1.0 + jnp.exp(-x))


def _lane_col(x, idx):
    lane = lax.broadcasted_iota(jnp.int32, x.shape, 1)
    return jnp.sum(jnp.where(lane == idx, x, 0.0), axis=-1, keepdims=True)


def _inproj_kernel(x_ref, w_ref, wdt_ref, o_ref, dt_ref, xb_ref):
    @pl.when(pl.program_id(1) == 0)
    def _():
        xb = x_ref[...].astype(BF16)
        xb_ref[...] = xb
        dt_ref[...] = jnp.dot(xb, wdt_ref[...], preferred_element_type=F32)

    o_ref[...] = jnp.dot(xb_ref[...], w_ref[...], preferred_element_type=F32).astype(o_ref.dtype)


def _in_proj(x2d, w_main, w_dt, tm, tn):
    T, D = x2d.shape
    N = w_main.shape[1]
    return pl.pallas_call(
        _inproj_kernel,
        grid=(T // tm, N // tn),
        in_specs=[
            pl.BlockSpec((tm, D), lambda i, j: (i, 0)),
            pl.BlockSpec((D, tn), lambda i, j: (0, j)),
            pl.BlockSpec((D, LANES), lambda i, j: (0, 0)),
        ],
        out_specs=[
            pl.BlockSpec((tm, tn), lambda i, j: (i, j)),
            pl.BlockSpec((tm, LANES), lambda i, j: (i, 0)),
        ],
        out_shape=[
            jax.ShapeDtypeStruct((T, N), BF16),
            jax.ShapeDtypeStruct((T, LANES), F32),
        ],
        scratch_shapes=[pltpu.VMEM((tm, D), BF16)],
        compiler_params=_cparams(("arbitrary", "arbitrary")),
        name="in_proj",
    )(x2d, w_main, w_dt)


def _attn_kernel(slopes_ref, q_ref, k_ref, v_ref, lamp_ref, nw_ref, o_ref,
                 tab_ref, acc_ref, sa_ref, sb_ref, pa_ref, pb_ref, mxa_ref, mxb_ref, ala_ref, alb_ref,
                 qm_ref, vt_ref, *, lambda_init):
    h = pl.program_id(1)
    tk, tq = tab_ref.shape[1:]
    ratio = tq // tk
    nk = q_ref.shape[0] // tk
    nq = nk // ratio
    slope2 = slopes_ref[h] * LOG2E

    s_rel = lax.broadcasted_iota(jnp.int32, (tk, tq), 0)
    t_rel = lax.broadcasted_iota(jnp.int32, (tk, tq), 1)
    tab_ref[0] = slope2 * s_rel.astype(F32)
    for d in range(ratio):
        s_q = s_rel + d * tk
        allowed = (s_q // CHUNK) <= (t_rel // CHUNK)
        val = slope2 * (t_rel - jnp.abs(t_rel - s_q) - d * tk).astype(F32)
        tab_ref[1 + d] = jnp.where(allowed, val, -jnp.inf)

    def prep(i, c):
        rows = pl.ds(pl.multiple_of(i * tk, tk), tk)
        q = q_ref[rows, :]
        lane = lax.broadcasted_iota(jnp.int32, q.shape, 1)
        qs = (q.astype(F32) * (ATT_HEAD_DIM ** -0.5 * LOG2E)).astype(BF16)
        zero = jnp.zeros_like(qs)
        qm_ref[0, rows, :] = jnp.where(lane < ATT_HEAD_DIM, qs, zero)
        qm_ref[1, rows, :] = jnp.where(lane >= ATT_HEAD_DIM, qs, zero)
        vt_ref[i, :ATT_V_DIM, :] = v_ref[rows, :].astype(F32).T.astype(BF16)
        extra_row = lax.broadcasted_iota(jnp.int32, (ATT_VT_ROWS - ATT_V_DIM, tk), 0)
        vt_ref[i, ATT_V_DIM:, :] = (extra_row == 0).astype(BF16)
        return c

    lax.fori_loop(0, nk, prep, 0)
    nt = (((1,), (1,)), ((), ()))
    lamp = lamp_ref[...]
    lam = (jnp.exp(jnp.sum(lamp[0:1] * lamp[1:2], axis=-1, keepdims=True))
           - jnp.exp(jnp.sum(lamp[2:3] * lamp[3:4], axis=-1, keepdims=True)) + lambda_init)

    half = 256
    ncol = tq // half
    strip = 16

    def last_j(qi):
        return ratio * (qi + 1) - 1

    def scores_piece(pair, s_out, mx_out, m, c):
        qi, j = pair
        cols = slice(c * half, (c + 1) * half)
        kb = k_ref[pl.ds(pl.multiple_of(j * tk, tk), tk), :]
        qh = qm_ref[m, pl.ds(pl.multiple_of(qi * tq + c * half, half), half), :]
        kind = jnp.maximum(j - ratio * qi + 1, 0)
        st = lax.dot_general(kb, qh, nt, preferred_element_type=F32) + tab_ref[kind, :, cols]
        s_out[m, :, cols] = st
        mx_out[m, :, cols] = jnp.max(st, axis=0, keepdims=True)

    def accum_piece(pair, p_in, al_in, m, c):
        cols = slice(c * half, (c + 1) * half)
        acc_ref[m, :, cols] = (al_in[m][:, cols] * acc_ref[m, :, cols]
                               + jnp.dot(vt_ref[pair[1]], p_in[m, :, cols], preferred_element_type=F32))

    def finalize(qi):
        a0 = acc_ref[0]
        a1 = acc_ref[1]
        dv = ATT_V_DIM
        o = a0[:dv] / a0[dv:dv + 1] - lam * (a1[:dv] / a1[dv:dv + 1])
        ms2 = jnp.mean(o * o, axis=0, keepdims=True)
        o = o * lax.rsqrt(ms2 + RMS_EPS) * nw_ref[...] * (1.0 - lambda_init)
        o_ref[pl.ds(pl.multiple_of(qi * tq, tq), tq), :] = o.T.astype(o_ref.dtype)

    def next_pair(pair):
        qi, j = pair
        wrap = j == last_j(qi)
        return jnp.where(wrap, qi + 1, qi), jnp.where(wrap, 0, j + 1)

    def trip(cur_set, nxt_set, state):
        s_c, mx_c, p_c, al_c = cur_set
        s_n, mx_n, p_n, al_n = nxt_set
        prv, cur, m_old = state
        nxt = next_pair(cur)
        nxt_c = (jnp.minimum(nxt[0], nq - 1), jnp.where(nxt[0] >= nq, nk - 1, nxt[1]))
        prv_c = (prv[0], jnp.maximum(prv[1], 0))
        cj = -slope2 * (cur[0] * tq - cur[1] * tk).astype(F32)
        refs, m_out = [], []
        for m in range(2):
            m_prev = jnp.where(cur[1] == 0, -jnp.inf, m_old[m])
            m_new = jnp.maximum(m_prev, mx_c[m] + cj)
            al_c[m] = jnp.exp2(m_prev - m_new)
            refs.append(m_new - cj)
            m_out.append(m_new)

        acc_p = [functools.partial(accum_piece, prv_c, p_n, al_n, m, c) for m in range(2) for c in range(ncol)]
        sco_p = [functools.partial(scores_piece, nxt_c, s_n, mx_n, m, c) for m in range(2) for c in range(ncol)]
        pieces = acc_p[:ncol] + sco_p + acc_p[ncol:]
        sw = 512
        strips = [(m, r, w) for m in range(2) for r in range(tk // strip) for w in range(tq // sw)]
        def prob_strip(m, r, w):
            rows = slice(r * strip, (r + 1) * strip)
            cols = slice(w * sw, (w + 1) * sw)
            p_c[m, rows, cols] = jnp.exp2(s_c[m, rows, cols] - refs[m][:, cols]).astype(BF16)

        per = (len(strips) - ATT_TAIL_STRIPS) // len(pieces)
        for g, piece in enumerate(pieces):
            piece()
            for s in strips[g * per:(g + 1) * per]:
                prob_strip(*s)
        for s in strips[len(pieces) * per:]:
            prob_strip(*s)

        @pl.when(prv[1] == last_j(prv[0]))
        def _():
            finalize(prv[0])

        return cur, nxt, tuple(m_out)

    set_a = (sa_ref, mxa_ref, pa_ref, ala_ref)
    set_b = (sb_ref, mxb_ref, pb_ref, alb_ref)
    acc_ref[...] = jnp.zeros_like(acc_ref)
    pb_ref[...] = jnp.zeros_like(pb_ref)
    alb_ref[...] = jnp.ones_like(alb_ref)
    zero_i = jnp.int32(0)
    for m in range(2):
        for c in range(ncol):
            scores_piece((zero_i, zero_i), sa_ref, mxa_ref, m, c)

    def body(t, state):
        return trip(set_b, set_a, trip(set_a, set_b, state))

    n_pairs = ratio * nq * (nq + 1) // 2
    m_init = jnp.full((1, tq), -jnp.inf, F32)
    first = ((zero_i, jnp.int32(-1)), (zero_i, zero_i), (m_init, m_init))
    state = lax.fori_loop(0, n_pairs // 2, body, first)
    if n_pairs % 2:
        trip(set_a, set_b, state)

    last_set = set_a if (n_pairs - 1) % 2 == 0 else set_b
    last_pair = (jnp.int32(nq - 1), jnp.int32(nk - 1))
    for m in range(2):
        for c in range(ncol):
            accum_piece(last_pair, last_set[2], last_set[3], m, c)
    finalize(last_pair[0])


def _diff_attention(proj, slopes, lamp, nw_col, B, S, lambda_init):
    T = B * S
    tq, tk = ATT_TQ, ATT_TK
    nk = S // tk
    kern = functools.partial(_attn_kernel, lambda_init=lambda_init)
    grid_spec = pltpu.PrefetchScalarGridSpec(
        num_scalar_prefetch=1,
        grid=(B, ATT_HEADS),
        in_specs=[
            pl.BlockSpec((S, LANES), lambda b, h, s: (b, h)),
            pl.BlockSpec((S, LANES), lambda b, h, s: (b, N_Q // LANES + h)),
            pl.BlockSpec((S, LANES), lambda b, h, s: (b, (N_Q + N_K) // LANES + h)),
            pl.BlockSpec((8, LANES), lambda b, h, s: (0, 0)),
            pl.BlockSpec((ATT_V_DIM, 1), lambda b, h, s: (0, 0)),
        ],
        out_specs=pl.BlockSpec((S, ATT_V_DIM), lambda b, h, s: (b, h)),
        scratch_shapes=[
            pltpu.VMEM((1 + tq // tk, tk, tq), F32),
            pltpu.VMEM((2, ATT_VT_ROWS, tq), F32),
            pltpu.VMEM((2, tk, tq), F32), pltpu.VMEM((2, tk, tq), F32),
            pltpu.VMEM((2, tk, tq), BF16), pltpu.VMEM((2, tk, tq), BF16),
            pltpu.VMEM((2, 1, tq), F32), pltpu.VMEM((2, 1, tq), F32),
            pltpu.VMEM((2, 1, tq), F32), pltpu.VMEM((2, 1, tq), F32),
            pltpu.VMEM((2, S, LANES), BF16),
            pltpu.VMEM((nk, ATT_VT_ROWS, tk), BF16),
        ],
    )
    return pl.pallas_call(
        kern,
        grid_spec=grid_spec,
        out_shape=jax.ShapeDtypeStruct((T, ATT_WIDTH), BF16),
        compiler_params=_cparams(("arbitrary", "arbitrary")),
        name="diff_attention",
    )(slopes, proj, proj, proj, lamp, nw_col)


def _expand_heads(v, e):
    hi = v.astype(BF16)
    lo = (v - hi.astype(F32)).astype(BF16)
    return jnp.dot(hi, e, preferred_element_type=F32) + jnp.dot(lo, e, preferred_element_type=F32)


def _ssd_kernel(z_ref, xs_ref, b_ref, c_ref, dt_ref, cw_ref, cb_ref, dtb_ref, alog_ref,
                dskip_ref, nw_ref, e_ref, *rest, n_side):
    side_in = rest[:n_side]
    o_ref = rest[n_side]
    side_out = rest[n_side + 1:2 * n_side + 1]
    ext_ref, st_ref, y_ref = rest[2 * n_side + 1:]
    for w_in_ref, w_out_ref in zip(side_in, side_out):
        w_out_ref[...] = w_in_ref[...].astype(w_out_ref.dtype)
    blk = pl.program_id(1)
    L = z_ref.shape[0]
    nchunk = L // CHUNK
    gw = SSM_WIDTH // SSM_GROUPS
    hpg = SSM_HEADS // SSM_GROUPS

    @pl.when(blk == 0)
    def _():
        ext_ref[0:8, :] = jnp.zeros((8, N_XBC), F32)
        st_ref[...] = jnp.zeros_like(st_ref)

    cur = jnp.concatenate([xs_ref[...], b_ref[...], c_ref[...]], axis=1).astype(F32)
    ext_ref[8:, :] = cur
    cw = cw_ref[...]
    conv = cb_ref[...] + cw[3:4] * cur
    for j in range(SSM_CONV - 1):
        conv = conv + cw[j:j + 1] * ext_ref[pl.ds(8 - (SSM_CONV - 1) + j, L), :]
    ext_ref[0:8, :] = cur[L - 8:, :]
    xbc = conv * _sigmoid(conv)
    xs = xbc[:, :SSM_WIDTH]
    bmb = xbc[:, SSM_WIDTH:SSM_WIDTH + SSM_GROUPS * SSM_STATE].astype(BF16)
    cmb = xbc[:, SSM_WIDTH + SSM_GROUPS * SSM_STATE:].astype(BF16)
    xsb = xs.astype(BF16)

    lane1 = lax.broadcasted_iota(jnp.int32, (1, LANES), 1)
    dtx = dt_ref[...] + dtb_ref[...]
    dtp = jnp.maximum(dtx, 0.0) + jnp.log1p(jnp.exp(-jnp.abs(dtx)))
    a_head = jnp.where(lane1 < SSM_HEADS, -jnp.exp(alog_ref[...]), 0.0)
    acs = dtp * a_head
    row_in_chunk = lax.broadcasted_iota(jnp.int32, (L, LANES), 0) & (CHUNK - 1)
    k = 1
    while k < CHUNK:
        acs = acs + jnp.where(row_in_chunk >= k, pltpu.roll(acs, k, axis=0), 0.0)
        k *= 2
    acs_t = acs.T
    dt_t = dtp.T

    e = e_ref[...]
    acs_last = jnp.concatenate(
        [jnp.broadcast_to(acs[c * CHUNK + CHUNK - 1:(c + 1) * CHUNK, :], (CHUNK, LANES)) for c in range(nchunk)],
        axis=0)
    w_exp = _expand_heads(dtp * jnp.exp(acs_last - acs), e)
    od_exp = _expand_heads(jnp.exp(acs), e)
    row8 = lax.broadcasted_iota(jnp.int32, (8, LANES), 0)
    cd8 = jnp.zeros((8, LANES), F32)
    for c in range(nchunk):
        cd8 = jnp.where(row8 == c, jnp.exp(acs[c * CHUNK + CHUNK - 1:(c + 1) * CHUNK, :]), cd8)
    cd_exp = _expand_heads(cd8, e)
    xw = (xs * w_exp).astype(BF16)

    tn = (((0,), (0,)), ((), ()))
    for c in range(nchunk):
        r0 = c * CHUNK
        for g in range(SSM_GROUPS):
            st = st_ref[g]
            cg = cmb[r0:r0 + CHUNK, g * SSM_STATE:(g + 1) * SSM_STATE]
            bg = bmb[r0:r0 + CHUNK, g * SSM_STATE:(g + 1) * SSM_STATE]
            y_ref[r0:r0 + CHUNK, g * gw:(g + 1) * gw] = jnp.dot(cg, st.astype(BF16), preferred_element_type=F32)
            snew = lax.dot_general(bg, xw[r0:r0 + CHUNK, g * gw:(g + 1) * gw], tn, preferred_element_type=F32)
            st_ref[g] = st * cd_exp[c:c + 1, g * gw:(g + 1) * gw] + snew
    y = y_ref[...] * od_exp + xs * dskip_ref[...]

    pair = 2 * CHUNK
    li = lax.broadcasted_iota(jnp.int32, (pair, pair), 0)
    si = lax.broadcasted_iota(jnp.int32, (pair, pair), 1)
    mask2 = (li >= si) & ((si >= CHUNK) | (li < CHUNK))
    lanep = lax.broadcasted_iota(jnp.int32, (pair, LANES), 1)
    nt = (((1,), (1,)), ((), ()))
    for pp in range(L // pair):
        r0 = pp * pair
        acs_p = acs[r0:r0 + pair, :]
        for g in range(SSM_GROUPS):
            cb2 = lax.dot_general(cmb[r0:r0 + pair, g * SSM_STATE:(g + 1) * SSM_STATE],
                                  bmb[r0:r0 + pair, g * SSM_STATE:(g + 1) * SSM_STATE],
                                  nt, preferred_element_type=F32)
            for hh in range(hpg // 2):
                hp = g * (hpg // 2) + hh
                mats = []
                for u in range(2):
                    hd = 2 * hp + u
                    seg = _lane_col(acs_p, hd) - acs_t[hd:hd + 1, r0:r0 + pair]
                    decay = jnp.exp(jnp.where(mask2, seg, -jnp.inf))
                    mats.append((cb2 * decay * dt_t[hd:hd + 1, r0:r0 + pair]).astype(BF16))
                lhs = jnp.concatenate(mats, axis=1)
                xp = xsb[r0:r0 + pair, hp * LANES:(hp + 1) * LANES]
                zero = jnp.zeros_like(xp)
                rhs = jnp.concatenate([jnp.where(lanep < SSM_HEAD_DIM, xp, zero),
                                       jnp.where(lanep >= SSM_HEAD_DIM, xp, zero)], axis=0)
                y_ref[r0:r0 + pair, hp * LANES:(hp + 1) * LANES] = jnp.dot(lhs, rhs, preferred_element_type=F32)
    y = y + y_ref[...]

    z = z_ref[...].astype(F32)
    y = y * (z * _sigmoid(z))
    outs = []
    for g in range(SSM_GROUPS):
        yg = y[:, g * gw:(g + 1) * gw]
        outs.append(yg * lax.rsqrt(jnp.mean(yg * yg, axis=-1, keepdims=True) + RMS_EPS))
    o_ref[...] = (jnp.concatenate(outs, axis=1) * nw_ref[...]).astype(o_ref.dtype)


def _ssd_mixer(proj, dt_raw, cw8, cb, dtb, alog, dskip_exp, nw, e_mat, B, S, side=()):
    T = B * S
    L = SSD_L
    nb = S // L
    row = lambda b, i: b * nb + i
    col0 = (N_Q + N_K + N_V + N_Z)
    const = lambda shape: pl.BlockSpec(shape, lambda b, i: (0, 0))
    steps = B * nb
    assert all(w.shape[0] % (8 * steps) == 0 for w in side)
    side_specs = [pl.BlockSpec((w.shape[0] // steps, w.shape[1]), lambda b, i: (row(b, i), 0)) for w in side]
    return pl.pallas_call(
        functools.partial(_ssd_kernel, n_side=len(side)),
        grid=(B, nb),
        in_specs=[
            pl.BlockSpec((L, N_Z), lambda b, i: (row(b, i), (N_Q + N_K + N_V) // N_Z)),
            pl.BlockSpec((L, SSM_WIDTH), lambda b, i: (row(b, i), col0 // SSM_WIDTH)),
            pl.BlockSpec((L, 256), lambda b, i: (row(b, i), (col0 + SSM_WIDTH) // 256)),
            pl.BlockSpec((L, 256), lambda b, i: (row(b, i), (col0 + SSM_WIDTH + 256) // 256)),
            pl.BlockSpec((L, LANES), lambda b, i: (row(b, i), 0)),
            const((8, N_XBC)),
            const((1, N_XBC)),
            const((1, LANES)),
            const((1, LANES)),
            const((1, SSM_WIDTH)),
            const((1, SSM_WIDTH)),
            const((LANES, SSM_WIDTH)),
        ] + side_specs,
        out_specs=[pl.BlockSpec((L, SSM_WIDTH), lambda b, i: (row(b, i), 0))] + side_specs,
        out_shape=[jax.ShapeDtypeStruct((T, SSM_WIDTH), BF16)]
        + [jax.ShapeDtypeStruct(w.shape, BF16) for w in side],
        scratch_shapes=[
            pltpu.VMEM((L + 8, N_XBC), F32),
            pltpu.VMEM((SSM_GROUPS, SSM_STATE, SSM_WIDTH // SSM_GROUPS), F32),
            pltpu.VMEM((L, SSM_WIDTH), F32),
        ],
        compiler_params=_cparams(("arbitrary", "arbitrary")),
        name="ssd_mixer",
    )(proj, proj, proj, proj, dt_raw, cw8, cb, dtb, alog, dskip_exp, nw, e_mat, *side)


def _layer_norm_rows(r, g, b):
    mu = jnp.mean(r, axis=-1, keepdims=True)
    d = r - mu
    var = jnp.mean(d * d, axis=-1, keepdims=True)
    return d * lax.rsqrt(var + LN_EPS) * g + b


def _outproj_kernel(att_ref, ssm_ref, x_ref, wa_ref, ws_ref, g_ref, b_ref, wr_ref, br_ref,
                    h_ref, eid_ref, ew_ref):
    chunk = min(OUTPROJ_CHUNK, h_ref.shape[0])
    for c in range(h_ref.shape[0] // chunk):
        rows = slice(c * chunk, (c + 1) * chunk)
        mix = (jnp.dot(att_ref[rows, :], wa_ref[...], preferred_element_type=F32)
               + jnp.dot(ssm_ref[rows, :], ws_ref[...], preferred_element_type=F32))
        h = _layer_norm_rows(DN_ALPHA * x_ref[rows, :] + mix, g_ref[...], b_ref[...])
        h_ref[rows, :] = h
        eid, ew = _route(h, wr_ref[...], br_ref[...])
        eid_ref[rows, :] = eid
        ew_ref[rows, :] = ew


def _route(h, wr, br):
    logits = jnp.dot(h.astype(BF16), wr, preferred_element_type=F32) + br
    lane = lax.broadcasted_iota(jnp.int32, logits.shape, 1)
    lanef = lane.astype(F32)
    big = float(LANES)
    gl = jnp.where(lane < N_EXPERT_GROUPS, logits, -jnp.inf)
    gmax = jnp.max(gl, axis=-1, keepdims=True)
    g_prob = 1.0 / jnp.sum(jnp.exp(gl - gmax), axis=-1, keepdims=True)
    gidx = jnp.min(jnp.where(gl == gmax, lanef, big), axis=-1, keepdims=True)
    lo = N_EXPERT_GROUPS + EXPERTS_PER_GROUP * gidx
    el = jnp.where((lanef >= lo) & (lanef < lo + EXPERTS_PER_GROUP), logits, -jnp.inf)
    t1 = jnp.max(el, axis=-1, keepdims=True)
    i1 = jnp.min(jnp.where(el == t1, lanef, big), axis=-1, keepdims=True)
    el2 = jnp.where(lanef == i1, -jnp.inf, el)
    t2 = jnp.max(el2, axis=-1, keepdims=True)
    i2 = jnp.min(jnp.where(el2 == t2, lanef, big), axis=-1, keepdims=True)
    ex = jnp.exp(t2 - t1)
    w1 = g_prob / (1.0 + ex)
    w2 = g_prob * ex / (1.0 + ex)
    eid = jnp.where(lane == 0, i1 - N_EXPERT_GROUPS, jnp.where(lane == 1, i2 - N_EXPERT_GROUPS, 0.0))
    return eid.astype(jnp.int32), jnp.where(lane == 0, w1, jnp.where(lane == 1, w2, 0.0))


def _out_proj(att, ssm, x2d, wa, ws, g, b, wr, br, tm):
    T, D = x2d.shape
    const = lambda shape: pl.BlockSpec(shape, lambda i: (0, 0))
    rows = lambda w: pl.BlockSpec((tm, w), lambda i: (i, 0))
    return pl.pallas_call(
        _outproj_kernel,
        grid=(T // tm,),
        in_specs=[rows(ATT_WIDTH), rows(SSM_WIDTH), rows(D), const((ATT_WIDTH, D)), const((SSM_WIDTH, D)),
                  const((1, D)), const((1, D)), const((D, LANES)), const((1, LANES))],
        out_specs=[rows(D), rows(LANES), rows(LANES)],
        out_shape=[jax.ShapeDtypeStruct((T, D), F32),
                   jax.ShapeDtypeStruct((T, LANES), jnp.int32),
                   jax.ShapeDtypeStruct((T, LANES), F32)],
        compiler_params=_cparams(("arbitrary",)),
        name="out_proj_ln1_router",
    )(att, ssm, x2d, wa, ws, g, b, wr, br)


def _pos_kernel(eid_ref, stril_ref, dest_ref, pend_ref, tot_ref, run_ref, pstart_ref):
    ph = pl.program_id(0)
    i = pl.program_id(1)
    tb = eid_ref.shape[0]
    lane = lax.broadcasted_iota(jnp.int32, (tb, LANES), 1)
    lanef = lane.astype(F32)
    ef = eid_ref[...].astype(F32)
    oh1 = (lanef == _lane_col(ef, 0)).astype(F32)
    oh2 = (lanef == _lane_col(ef, 1)).astype(F32)
    cnt = oh1 + oh2

    @pl.when((ph == 0) & (i == 0))
    def _():
        tot_ref[...] = jnp.zeros_like(tot_ref)

    @pl.when(ph == 0)
    def _():
        tot_ref[...] += jnp.sum(cnt, axis=0, keepdims=True)

    @pl.when((ph == 1) & (i == 0))
    def _():
        tot = jnp.broadcast_to(tot_ref[...], (8, LANES))
        padded = jnp.floor((tot + (MOE_BM - 1)) * (1.0 / MOE_BM)) * MOE_BM
        lane8 = lax.broadcasted_iota(jnp.int32, (8, LANES), 1)
        ends = padded
        k = 1
        while k < LANES:
            ends = ends + jnp.where(lane8 >= k, pltpu.roll(ends, k, axis=1), 0.0)
            k *= 2
        row8 = lax.broadcasted_iota(jnp.int32, (8, LANES), 0)
        pend_ref[...] = jnp.where(row8 == 0, ends, jnp.where(row8 == 1, tot, 0.0))
        pstart_ref[...] = (ends - padded)[0:1]
        run_ref[...] = jnp.zeros_like(run_ref)

    @pl.when(ph == 1)
    def _():
        pre = jnp.dot(stril_ref[...], cnt.astype(BF16), preferred_element_type=F32)
        slot = pstart_ref[...] + run_ref[...] + pre
        d1 = jnp.sum(oh1 * slot, axis=-1, keepdims=True)
        d2 = jnp.sum(oh2 * slot, axis=-1, keepdims=True)
        dmat = jnp.where(lane == 0, d1, jnp.where(lane == 1, d2, 0.0))
        dest_ref[0] = dmat.T[:8, :].astype(jnp.int32)
        run_ref[...] += jnp.sum(cnt, axis=0, keepdims=True)


def _positions(eid, stril):
    T = eid.shape[0]
    tb = POS_TB
    return pl.pallas_call(
        _pos_kernel,
        grid=(2, T // tb),
        in_specs=[pl.BlockSpec((tb, LANES), lambda p, i: (i, 0)),
                  pl.BlockSpec((tb, tb), lambda p, i: (0, 0))],
        out_specs=[pl.BlockSpec((1, 8, tb), lambda p, i: (i * p, 0, 0)),
                   pl.BlockSpec((8, LANES), lambda p, i: (0, 0))],
        out_shape=[jax.ShapeDtypeStruct((T // tb, 8, tb), jnp.int32),
                   jax.ShapeDtypeStruct((8, LANES), F32)],
        scratch_shapes=[pltpu.VMEM((1, LANES), F32), pltpu.VMEM((1, LANES), F32), pltpu.VMEM((1, LANES), F32)],
        compiler_params=_cparams(("arbitrary", "arbitrary")),
        name="dispatch_positions",
    )(eid, stril)


def _row_copy(src, s, dst, d, sem):
    return pltpu.make_async_copy(src.at[pl.ds(s, 1)], dst.at[pl.ds(d, 1)], sem)


def _scatter_kernel(zstart_ref, zcnt_ref, dest_ref, h_hbm, xs_hbm, zrow_ref, hbuf, in_sems, out_sems, zsem):
    i = pl.program_id(0)
    n = pl.num_programs(0)
    tb = hbuf.shape[1]

    @pl.when(i == 0)
    def _():
        zrow_ref[...] = jnp.zeros_like(zrow_ref)

        def fill_ops(e, act):
            cnt = zcnt_ref[e]
            start = zstart_ref[e]
            head = jnp.minimum((-start) & 7, cnt)
            lax.fori_loop(0, head, lambda r, c: (act(_row_copy(zrow_ref, 0, xs_hbm, start + r, zsem)), c)[1], 0)
            rest = cnt - head
            off = start + head
            size = 8
            while size < MOE_BM:
                @pl.when((rest & size) != 0)
                def _(off=off, size=size):
                    act(pltpu.make_async_copy(zrow_ref.at[pl.ds(0, size)],
                                              xs_hbm.at[pl.ds(pl.multiple_of(off, 8), size)], zsem))
                off = off + (rest & size)
                size *= 2

        def fill(e, c):
            fill_ops(e, lambda cp: cp.start())
            return c

        def fill_wait(e, c):
            fill_ops(e, lambda cp: cp.wait())
            return c

        nblk = xs_hbm.shape[0] // MOE_BM
        first_free = zstart_ref[N_EXPERTS]

        def tail_copy(b):
            return pltpu.make_async_copy(zrow_ref, xs_hbm.at[pl.ds(b * MOE_BM, MOE_BM)], zsem)

        lax.fori_loop(0, N_EXPERTS, fill, 0)
        lax.fori_loop(first_free, nblk, lambda b, c: (tail_copy(b).start(), c)[1], 0)
        lax.fori_loop(0, N_EXPERTS, fill_wait, 0)
        lax.fori_loop(first_free, nblk, lambda b, c: (tail_copy(b).wait(), c)[1], 0)

    slot = lax.rem(i, 3)
    nxt = lax.rem(i + 1, 3)
    prv = lax.rem(i + 2, 3)

    def load(b, s):
        return pltpu.make_async_copy(h_hbm.at[pl.ds(pl.multiple_of(b * tb, tb), tb)], hbuf.at[s], in_sems.at[s])

    def wait_rows(s):
        for _ in range(2):
            pltpu.make_async_copy(hbuf.at[s], xs_hbm.at[pl.ds(0, tb)], out_sems.at[s]).wait()

    @pl.when(i == 0)
    def _():
        load(0, 0).start()

    @pl.when(i + 1 < n)
    def _():
        load(i + 1, nxt).start()

    load(i, slot).wait()

    def issue(r, c):
        _row_copy(hbuf.at[slot], r, xs_hbm, dest_ref[0, 0, r], out_sems.at[slot]).start(priority=0)
        _row_copy(hbuf.at[slot], r, xs_hbm, dest_ref[0, 1, r], out_sems.at[slot]).start(priority=1)
        return c

    lax.fori_loop(0, tb, issue, 0, unroll=8)

    @pl.when(i > 0)
    def _():
        wait_rows(prv)

    @pl.when(i == n - 1)
    def _():
        wait_rows(slot)


def _dest_spec(dest, tb, step_of):
    per = dest.shape[2] // tb
    return pl.BlockSpec((1, 8, tb), lambda i, *_: (step_of(i) // per, 0, step_of(i) % per), memory_space=pltpu.SMEM)


def _dispatch(zstart, zcnt, dest, h, cap):
    T, D = h.shape
    tb = TOK_TB
    nb = T // tb
    grid_spec = pltpu.PrefetchScalarGridSpec(
        num_scalar_prefetch=2,
        grid=(nb,),
        in_specs=[_dest_spec(dest, tb, lambda i: i),
                  pl.BlockSpec(memory_space=pl.ANY)],
        out_specs=pl.BlockSpec(memory_space=pl.ANY),
        scratch_shapes=[pltpu.VMEM((MOE_BM, D), h.dtype), pltpu.VMEM((3, tb, D), h.dtype),
                        pltpu.SemaphoreType.DMA((3,)), pltpu.SemaphoreType.DMA((3,)), pltpu.SemaphoreType.DMA],
    )
    return pl.pallas_call(
        _scatter_kernel,
        grid_spec=grid_spec,
        out_shape=jax.ShapeDtypeStruct((cap, D), h.dtype),
        compiler_params=_cparams(("arbitrary",)),
        name="moe_dispatch",
    )(zstart, zcnt, dest, h)


def _expert_kernel(bexp_ref, nused_ref, x_ref, wg_ref, wu_ref, wd_ref, y_ref):
    i = pl.program_id(0)

    @pl.when(i < nused_ref[0])
    def _():
        x = x_ref[...].astype(BF16)
        gate = jnp.dot(x, wg_ref[0], preferred_element_type=F32)
        up = jnp.dot(x, wu_ref[0], preferred_element_type=F32)
        hid = (gate * _sigmoid(gate) * up).astype(BF16)
        y_ref[...] = jnp.dot(hid, wd_ref[0], preferred_element_type=F32)

    @pl.when(i >= nused_ref[0])
    def _():
        y_ref[...] = jnp.zeros_like(y_ref)


def _expert_mlp(bexp, nused, xs, wg, wu, wd):
    cap, D = xs.shape
    bm = MOE_BM
    H = wg.shape[2]
    grid_spec = pltpu.PrefetchScalarGridSpec(
        num_scalar_prefetch=2,
        grid=(cap // bm,),
        in_specs=[
            pl.BlockSpec((bm, D), lambda i, be, nu: (i, 0)),
            pl.BlockSpec((1, D, H), lambda i, be, nu: (be[i], 0, 0)),
            pl.BlockSpec((1, D, H), lambda i, be, nu: (be[i], 0, 0)),
            pl.BlockSpec((1, H, D), lambda i, be, nu: (be[i], 0, 0)),
        ],
        out_specs=pl.BlockSpec((bm, D), lambda i, be, nu: (i, 0)),
    )
    return pl.pallas_call(
        _expert_kernel,
        grid_spec=grid_spec,
        out_shape=jax.ShapeDtypeStruct((cap, D), F32),
        compiler_params=_cparams(("arbitrary",)),
        name="expert_mlp",
    )(bexp, nused, xs, wg, wu, wd)


def _combine_kernel(dest_ref, dnext_ref, h_ref, ew_ref, g_ref, b_ref, y_hbm, o_ref, ybuf, sems):
    i = pl.program_id(0)
    n = pl.num_programs(0)
    tb = h_ref.shape[0]
    slot = i & 1

    def gather(d_ref, s, unroll):
        def issue(r, c):
            _row_copy(y_hbm, d_ref[0, 0, r], ybuf.at[s, 0], r, sems.at[s]).start(priority=0)
            _row_copy(y_hbm, d_ref[0, 1, r], ybuf.at[s, 1], r, sems.at[s]).start(priority=1)
            return c
        lax.fori_loop(0, tb, issue, 0, unroll=unroll)

    def wait_slot(s):
        for k in range(2):
            pltpu.make_async_copy(y_hbm.at[pl.ds(0, tb)], ybuf.at[s, k], sems.at[s]).wait()

    @pl.when(i == 0)
    def _():
        gather(dest_ref, slot, 8)

    wait_slot(slot)
    gather(dnext_ref, 1 - slot, tb)
    ew = ew_ref[...]
    ffn = _lane_col(ew, 0) * ybuf[slot, 0] + _lane_col(ew, 1) * ybuf[slot, 1]
    o_ref[...] = _layer_norm_rows(DN_ALPHA * h_ref[...] + ffn, g_ref[...], b_ref[...])

    @pl.when(i == n - 1)
    def _():
        wait_slot(1 - slot)


def _combine(dest, h, ew, g, b, y):
    T, D = h.shape
    tb = TOK_TB
    nb = T // tb
    return pl.pallas_call(
        _combine_kernel,
        grid=(nb,),
        in_specs=[_dest_spec(dest, tb, lambda i: i),
                  _dest_spec(dest, tb, lambda i: jnp.minimum(i + 1, nb - 1)),
                  pl.BlockSpec((tb, D), lambda i: (i, 0)),
                  pl.BlockSpec((tb, LANES), lambda i: (i, 0)),
                  pl.BlockSpec((1, D), lambda i: (0, 0)),
                  pl.BlockSpec((1, D), lambda i: (0, 0)),
                  pl.BlockSpec(memory_space=pl.ANY)],
        out_specs=pl.BlockSpec((tb, D), lambda i: (i, 0)),
        out_shape=jax.ShapeDtypeStruct((T, D), F32),
        scratch_shapes=[pltpu.VMEM((2, 2, tb, D), F32), pltpu.SemaphoreType.DMA((2,))],
        compiler_params=_cparams(("arbitrary",)),
        name="moe_combine_ln2",
    )(dest, dest, h, ew, g, b, y)


def _pad_lanes(v, n=LANES):
    v = v.reshape(1, -1).astype(F32)
    return jnp.pad(v, ((0, 0), (0, n - v.shape[1])))


def kernel(x, w_in, lambda_q1, lambda_k1, lambda_q2, lambda_k2, attn_norm_w, conv_w, conv_b, dt_bias, a_log, d_skip, ssm_norm_w, w_out, ln1_g, ln1_b, w_router_group, b_router_group, w_router_expert, b_router_expert, w_gate, w_up, w_down, ln2_g, ln2_b):
    B, S, D = x.shape
    T = B * S
    assert w_in.shape[0] == DEPTH == 1
    assert S % ATT_TQ == 0 and S % SSD_L == 0 and T % POS_TB == 0 and POS_TB % TOK_TB == 0
    l = 0
    lambda_init = 0.8 - 0.6 * math.exp(-0.3 * l)
    x2d = x.reshape(T, D)

    w_main = w_in[l][:, :N_MAIN].astype(BF16)
    w_dt = jnp.pad(w_in[l][:, N_MAIN:], ((0, 0), (0, LANES - N_DT))).astype(BF16)
    slopes = jnp.exp2(-8.0 * jnp.arange(1, ATT_HEADS + 1, dtype=F32) / ATT_HEADS)
    lamp = jnp.concatenate([_pad_lanes(lambda_q1[l]), _pad_lanes(lambda_k1[l]),
                            _pad_lanes(lambda_q2[l]), _pad_lanes(lambda_k2[l]),
                            jnp.zeros((4, LANES), F32)], axis=0)
    nw_col = attn_norm_w[l].astype(F32).reshape(ATT_V_DIM, 1)
    cw8 = jnp.pad(conv_w[l].astype(F32), ((0, 8 - SSM_CONV), (0, 0)))
    cb = conv_b[l].astype(F32).reshape(1, N_XBC)
    dskip_exp = jnp.repeat(d_skip[l].astype(F32), SSM_HEAD_DIM).reshape(1, SSM_WIDTH)
    ssm_nw = ssm_norm_w[l].astype(F32).reshape(1, SSM_WIDTH)
    head_of_lane = jnp.arange(SSM_WIDTH, dtype=jnp.int32) // SSM_HEAD_DIM
    e_mat = (jnp.arange(LANES, dtype=jnp.int32)[:, None] == head_of_lane[None, :]).astype(BF16)
    wa = w_out[l][:ATT_WIDTH].astype(BF16)
    ws = w_out[l][ATT_WIDTH:].astype(BF16)
    wr = jnp.concatenate(
        [w_router_group[l], jnp.transpose(w_router_expert[l], (1, 0, 2)).reshape(D, N_EXPERTS)], axis=1)
    wr = jnp.pad(wr, ((0, 0), (0, LANES - wr.shape[1]))).astype(BF16)
    br = _pad_lanes(jnp.concatenate([b_router_group[l], b_router_expert[l].reshape(-1)]))
    row = lambda v: v.astype(F32).reshape(1, D)

    tm_in = INPROJ_TM if T % INPROJ_TM == 0 else 256
    proj, dt_raw = _in_proj(x2d, w_main, w_dt, tm_in, INPROJ_TN)
    att = _diff_attention(proj, slopes, lamp, nw_col, B, S, lambda_init)
    experts_f32 = [w[l].astype(F32).reshape(-1, w.shape[-1]) for w in (w_gate, w_up, w_down)]
    ssm, wg_b, wu_b, wd_b = _ssd_mixer(proj, dt_raw, cw8, cb, _pad_lanes(dt_bias[l]), _pad_lanes(a_log[l]),
                                       dskip_exp, ssm_nw, e_mat, B, S, side=experts_f32)
    wg_b, wu_b, wd_b = (w.reshape(src.shape[1:]) for w, src in zip((wg_b, wu_b, wd_b), (w_gate, w_up, w_down)))
    h1, eid, ew = _out_proj(att, ssm, x2d, wa, ws, row(ln1_g[l]), row(ln1_b[l]), wr, br,
                            OUTPROJ_TM if T % OUTPROJ_TM == 0 else 256)

    stril = (jnp.arange(POS_TB)[:, None] > jnp.arange(POS_TB)[None, :]).astype(BF16)
    dest, pend = _positions(eid, stril)
    nblk = (T * 2) // MOE_BM + N_EXPERTS
    cap = nblk * MOE_BM
    pad_ends = pend[0, :N_EXPERTS].astype(jnp.int32)
    blk_start = jnp.arange(nblk, dtype=jnp.int32) * MOE_BM
    nused = (pad_ends[N_EXPERTS - 1] // MOE_BM).astype(jnp.int32)
    last_used = jnp.maximum(nused - 1, 0) * MOE_BM
    bexp = jnp.sum(pad_ends[None, :] <= jnp.minimum(blk_start, last_used)[:, None], axis=1).astype(jnp.int32)
    bexp = jnp.minimum(bexp, N_EXPERTS - 1)
    counts = pend[1, :N_EXPERTS].astype(jnp.int32)
    padded = pad_ends - jnp.concatenate([jnp.zeros((1,), jnp.int32), pad_ends[:-1]])
    zstart = jnp.concatenate([pad_ends - padded + counts, nused.reshape(1)])
    xs_sorted = _dispatch(zstart, padded - counts, dest, h1, cap)
    y_sorted = _expert_mlp(bexp, nused.reshape(1), xs_sorted, wg_b, wu_b, wd_b)
    out = _combine(dest, h1, ew, row(ln2_g[l]), row(ln2_b[l]), y_sorted)
    return out.reshape(B, S, D)
```

```python
import functools
import math

import jax
import jax.numpy as jnp
from jax import lax
from jax.experimental import pallas as pl
from jax.experimental.pallas import tpu as pltpu

F32 = jnp.float32
BF16 = jnp.bfloat16

CHUNK = 64
ATT_HEADS = 8
ATT_HEAD_DIM = 64
ATT_V_DIM = 128
ATT_WIDTH = ATT_HEADS * ATT_V_DIM
ATT_VT_ROWS = ATT_V_DIM + 16
LOG2E = 1.4426950408889634
SSM_HEADS = 16
SSM_HEAD_DIM = 64
SSM_WIDTH = SSM_HEADS * SSM_HEAD_DIM
SSM_GROUPS = 2
SSM_STATE = 128
SSM_CONV = 4
N_Q = 1024
N_K = 1024
N_V = 1024
N_Z = 1024
N_XBC = SSM_WIDTH + 2 * SSM_GROUPS * SSM_STATE
N_DT = SSM_HEADS
N_MAIN = N_Q + N_K + N_V + N_Z + N_XBC
N_EXPERT_GROUPS = 4
EXPERTS_PER_GROUP = 8
N_EXPERTS = 32
EXPERT_HIDDEN = 1024
DEPTH = 1
DN_ALPHA = (2 * DEPTH) ** 0.25
LN_EPS = 1e-5
RMS_EPS = 1e-6
LANES = 128

VMEM_LIMIT = 56 * 1024 * 1024

INPROJ_TM = 512
INPROJ_TN = 2816
OUTPROJ_TM = 512
OUTPROJ_CHUNK = 256
ATT_TAIL_STRIPS = 0
ATT_TQ = 512
ATT_TK = 512
SSD_L = 512
MOE_BM = 512
TOK_TB = 512
POS_TB = 1024


def _cparams(sem, flags=None):
    return pltpu.CompilerParams(dimension_semantics=sem, vmem_limit_bytes=VMEM_LIMIT, flags=flags)


def _sigmoid(x):
    return 1.0 / (1.0 + jnp.exp(-x))


def _lane_col(x, idx):
    lane = lax.broadcasted_iota(jnp.int32, x.shape, 1)
    return jnp.sum(jnp.where(lane == idx, x, 0.0), axis=-1, keepdims=True)


def _inproj_kernel(x_ref, w_ref, wdt_ref, o_ref, dt_ref, xb_ref):
    @pl.when(pl.program_id(1) == 0)
    def _():
        xb = x_ref[...].astype(BF16)
        xb_ref[...] = xb
        dt_ref[...] = jnp.dot(xb, wdt_ref[...], preferred_element_type=F32)

    o_ref[...] = jnp.dot(xb_ref[...], w_ref[...], preferred_element_type=F32).astype(o_ref.dtype)


def _in_proj(x2d, w_main, w_dt, tm, tn):
    T, D = x2d.shape
    N = N_MAIN
    return pl.pallas_call(
        _inproj_kernel,
        grid=(T // tm, N // tn),
        in_specs=[
            pl.BlockSpec((tm, D), lambda i, j: (i, 0)),
            pl.BlockSpec((D, tn), lambda i, j: (0, j)),
            pl.BlockSpec((D, LANES), lambda i, j: (0, 0)),
        ],
        out_specs=[
            pl.BlockSpec((tm, tn), lambda i, j: (i, j)),
            pl.BlockSpec((tm, LANES), lambda i, j: (i, 0)),
        ],
        out_shape=[
            jax.ShapeDtypeStruct((T, N), BF16),
            jax.ShapeDtypeStruct((T, LANES), F32),
        ],
        scratch_shapes=[pltpu.VMEM((tm, D), BF16)],
        compiler_params=_cparams(("arbitrary", "arbitrary")),
        name="in_proj",
    )(x2d, w_main, w_dt)


def _attn_kernel(slopes_ref, q_ref, k_ref, v_ref, lamp_ref, nw_ref, o_ref,
                 tab_ref, acc_ref, sa_ref, sb_ref, pa_ref, pb_ref, mxa_ref, mxb_ref, ala_ref, alb_ref,
                 qm_ref, vt_ref, *, lambda_init):
    h = pl.program_id(1)
    tk, tq = tab_ref.shape[1:]
    ratio = tq // tk
    nk = q_ref.shape[0] // tk
    nq = nk // ratio
    slope2 = slopes_ref[h] * LOG2E

    s_rel = lax.broadcasted_iota(jnp.int32, (tk, tq), 0)
    t_rel = lax.broadcasted_iota(jnp.int32, (tk, tq), 1)
    tab_ref[0] = slope2 * s_rel.astype(F32)
    for d in range(ratio):
        s_q = s_rel + d * tk
        allowed = (s_q // CHUNK) <= (t_rel // CHUNK)
        val = slope2 * (t_rel - jnp.abs(t_rel - s_q) - d * tk).astype(F32)
        tab_ref[1 + d] = jnp.where(allowed, val, -jnp.inf)

    def prep(i, c):
        rows = pl.ds(pl.multiple_of(i * tk, tk), tk)
        q = q_ref[rows, :]
        lane = lax.broadcasted_iota(jnp.int32, q.shape, 1)
        qs = (q.astype(F32) * (ATT_HEAD_DIM ** -0.5 * LOG2E)).astype(BF16)
        zero = jnp.zeros_like(qs)
        qm_ref[0, rows, :] = jnp.where(lane < ATT_HEAD_DIM, qs, zero)
        qm_ref[1, rows, :] = jnp.where(lane >= ATT_HEAD_DIM, qs, zero)
        vt_ref[i, :ATT_V_DIM, :] = v_ref[rows, :].astype(F32).T.astype(BF16)
        extra_row = lax.broadcasted_iota(jnp.int32, (ATT_VT_ROWS - ATT_V_DIM, tk), 0)
        vt_ref[i, ATT_V_DIM:, :] = (extra_row == 0).astype(BF16)
        return c

    lax.fori_loop(0, nk, prep, 0)
    nt = (((1,), (1,)), ((), ()))
    lamp = lamp_ref[...]
    lam = (jnp.exp(jnp.sum(lamp[0:1] * lamp[1:2], axis=-1, keepdims=True))
           - jnp.exp(jnp.sum(lamp[2:3] * lamp[3:4], axis=-1, keepdims=True)) + lambda_init)

    half = 256
    ncol = tq // half
    strip = 16

    def last_j(qi):
        return ratio * (qi + 1) - 1

    def scores_piece(pair, s_out, mx_out, m, c):
        qi, j = pair
        cols = slice(c * half, (c + 1) * half)
        kb = k_ref[pl.ds(pl.multiple_of(j * tk, tk), tk), :]
        qh = qm_ref[m, pl.ds(pl.multiple_of(qi * tq + c * half, half), half), :]
        kind = jnp.maximum(j - ratio * qi + 1, 0)
        st = lax.dot_general(kb, qh, nt, preferred_element_type=F32) + tab_ref[kind, :, cols]
        s_out[m, :, cols] = st
        mx_out[m, :, cols] = jnp.max(st, axis=0, keepdims=True)

    def accum_piece(pair, p_in, al_in, m, c):
        cols = slice(c * half, (c + 1) * half)
        acc_ref[m, :, cols] = (al_in[m][:, cols] * acc_ref[m, :, cols]
                               + jnp.dot(vt_ref[pair[1]], p_in[m, :, cols], preferred_element_type=F32))

    def finalize(qi):
        a0 = acc_ref[0]
        a1 = acc_ref[1]
        dv = ATT_V_DIM
        o = a0[:dv] / a0[dv:dv + 1] - lam * (a1[:dv] / a1[dv:dv + 1])
        ms2 = jnp.mean(o * o, axis=0, keepdims=True)
        o = o * lax.rsqrt(ms2 + RMS_EPS) * nw_ref[...] * (1.0 - lambda_init)
        o_ref[pl.ds(pl.multiple_of(qi * tq, tq), tq), :] = o.T.astype(o_ref.dtype)

    def next_pair(pair):
        qi, j = pair
        wrap = j == last_j(qi)
        return jnp.where(wrap, qi + 1, qi), jnp.where(wrap, 0, j + 1)

    def trip(cur_set, nxt_set, state):
        s_c, mx_c, p_c, al_c = cur_set
        s_n, mx_n, p_n, al_n = nxt_set
        prv, cur, m_old = state
        nxt = next_pair(cur)
        nxt_c = (jnp.minimum(nxt[0], nq - 1), jnp.where(nxt[0] >= nq, nk - 1, nxt[1]))
        prv_c = (prv[0], jnp.maximum(prv[1], 0))
        cj = -slope2 * (cur[0] * tq - cur[1] * tk).astype(F32)
        refs, m_out = [], []
        for m in range(2):
            m_prev = jnp.where(cur[1] == 0, -jnp.inf, m_old[m])
            m_new = jnp.maximum(m_prev, mx_c[m] + cj)
            al_c[m] = jnp.exp2(m_prev - m_new)
            refs.append(m_new - cj)
            m_out.append(m_new)

        acc_p = [functools.partial(accum_piece, prv_c, p_n, al_n, m, c) for m in range(2) for c in range(ncol)]
        sco_p = [functools.partial(scores_piece, nxt_c, s_n, mx_n, m, c) for m in range(2) for c in range(ncol)]
        pieces = acc_p[:ncol] + sco_p + acc_p[ncol:]
        sw = 512
        strips = [(m, r, w) for m in range(2) for r in range(tk // strip) for w in range(tq // sw)]
        def prob_strip(m, r, w):
            rows = slice(r * strip, (r + 1) * strip)
            cols = slice(w * sw, (w + 1) * sw)
            p_c[m, rows, cols] = jnp.exp2(s_c[m, rows, cols] - refs[m][:, cols]).astype(BF16)

        per = (len(strips) - ATT_TAIL_STRIPS) // len(pieces)
        for g, piece in enumerate(pieces):
            piece()
            for s in strips[g * per:(g + 1) * per]:
                prob_strip(*s)
        for s in strips[len(pieces) * per:]:
            prob_strip(*s)

        @pl.when(prv[1] == last_j(prv[0]))
        def _():
            finalize(prv[0])

        return cur, nxt, tuple(m_out)

    set_a = (sa_ref, mxa_ref, pa_ref, ala_ref)
    set_b = (sb_ref, mxb_ref, pb_ref, alb_ref)
    acc_ref[...] = jnp.zeros_like(acc_ref)
    pb_ref[...] = jnp.zeros_like(pb_ref)
    alb_ref[...] = jnp.ones_like(alb_ref)
    zero_i = jnp.int32(0)
    for m in range(2):
        for c in range(ncol):
            scores_piece((zero_i, zero_i), sa_ref, mxa_ref, m, c)

    def body(t, state):
        return trip(set_b, set_a, trip(set_a, set_b, state))

    n_pairs = ratio * nq * (nq + 1) // 2
    m_init = jnp.full((1, tq), -jnp.inf, F32)
    first = ((zero_i, jnp.int32(-1)), (zero_i, zero_i), (m_init, m_init))
    state = lax.fori_loop(0, n_pairs // 2, body, first)
    if n_pairs % 2:
        trip(set_a, set_b, state)

    last_set = set_a if (n_pairs - 1) % 2 == 0 else set_b
    last_pair = (jnp.int32(nq - 1), jnp.int32(nk - 1))
    for m in range(2):
        for c in range(ncol):
            accum_piece(last_pair, last_set[2], last_set[3], m, c)
    finalize(last_pair[0])


def _diff_attention(proj, slopes, lamp, nw_col, B, S, lambda_init):
    T = B * S
    tq, tk = ATT_TQ, ATT_TK
    nk = S // tk
    kern = functools.partial(_attn_kernel, lambda_init=lambda_init)
    grid_spec = pltpu.PrefetchScalarGridSpec(
        num_scalar_prefetch=1,
        grid=(B, ATT_HEADS),
        in_specs=[
            pl.BlockSpec((S, LANES), lambda b, h, s: (b, h)),
            pl.BlockSpec((S, LANES), lambda b, h, s: (b, N_Q // LANES + h)),
            pl.BlockSpec((S, LANES), lambda b, h, s: (b, (N_Q + N_K) // LANES + h)),
            pl.BlockSpec((8, LANES), lambda b, h, s: (0, 0)),
            pl.BlockSpec((ATT_V_DIM, 1), lambda b, h, s: (0, 0)),
        ],
        out_specs=pl.BlockSpec((S, ATT_V_DIM), lambda b, h, s: (b, h)),
        scratch_shapes=[
            pltpu.VMEM((1 + tq // tk, tk, tq), F32),
            pltpu.VMEM((2, ATT_VT_ROWS, tq), F32),
            pltpu.VMEM((2, tk, tq), F32), pltpu.VMEM((2, tk, tq), F32),
            pltpu.VMEM((2, tk, tq), BF16), pltpu.VMEM((2, tk, tq), BF16),
            pltpu.VMEM((2, 1, tq), F32), pltpu.VMEM((2, 1, tq), F32),
            pltpu.VMEM((2, 1, tq), F32), pltpu.VMEM((2, 1, tq), F32),
            pltpu.VMEM((2, S, LANES), BF16),
            pltpu.VMEM((nk, ATT_VT_ROWS, tk), BF16),
        ],
    )
    return pl.pallas_call(
        kern,
        grid_spec=grid_spec,
        out_shape=jax.ShapeDtypeStruct((T, ATT_WIDTH), BF16),
        compiler_params=_cparams(("arbitrary", "arbitrary")),
        name="diff_attention",
    )(slopes, proj, proj, proj, lamp, nw_col)


def _expand_heads(v, e):
    hi = v.astype(BF16)
    lo = (v - hi.astype(F32)).astype(BF16)
    return jnp.dot(hi, e, preferred_element_type=F32) + jnp.dot(lo, e, preferred_element_type=F32)


def _ssd_kernel(z_ref, xs_ref, b_ref, c_ref, dt_ref, cw_ref, cb_ref, dtb_ref, alog_ref,
                dskip_ref, nw_ref, e_ref, *rest, n_side):
    side_in = rest[:n_side]
    o_ref = rest[n_side]
    side_out = rest[n_side + 1:2 * n_side + 1]
    ext_ref, st_ref, y_ref = rest[2 * n_side + 1:]
    for w_in_ref, w_out_ref in zip(side_in, side_out):
        w_out_ref[...] = w_in_ref[...].astype(w_out_ref.dtype)
    blk = pl.program_id(1)
    L = z_ref.shape[0]
    nchunk = L // CHUNK
    gw = SSM_WIDTH // SSM_GROUPS
    hpg = SSM_HEADS // SSM_GROUPS

    @pl.when(blk == 0)
    def _():
        ext_ref[0:8, :] = jnp.zeros((8, N_XBC), F32)
        st_ref[...] = jnp.zeros_like(st_ref)

    cur = jnp.concatenate([xs_ref[...], b_ref[...], c_ref[...]], axis=1).astype(F32)
    ext_ref[8:, :] = cur
    cw = cw_ref[...]
    conv = cb_ref[...] + cw[3:4] * cur
    for j in range(SSM_CONV - 1):
        conv = conv + cw[j:j + 1] * ext_ref[pl.ds(8 - (SSM_CONV - 1) + j, L), :]
    ext_ref[0:8, :] = cur[L - 8:, :]
    xbc = conv * _sigmoid(conv)
    xs = xbc[:, :SSM_WIDTH]
    bmb = xbc[:, SSM_WIDTH:SSM_WIDTH + SSM_GROUPS * SSM_STATE].astype(BF16)
    cmb = xbc[:, SSM_WIDTH + SSM_GROUPS * SSM_STATE:].astype(BF16)
    xsb = xs.astype(BF16)

    lane1 = lax.broadcasted_iota(jnp.int32, (1, LANES), 1)
    dtx = dt_ref[...] + dtb_ref[...]
    dtp = jnp.maximum(dtx, 0.0) + jnp.log1p(jnp.exp(-jnp.abs(dtx)))
    a_head = jnp.where(lane1 < SSM_HEADS, -jnp.exp(alog_ref[...]), 0.0)
    acs = dtp * a_head
    row_in_chunk = lax.broadcasted_iota(jnp.int32, (L, LANES), 0) & (CHUNK - 1)
    k = 1
    while k < CHUNK:
        acs = acs + jnp.where(row_in_chunk >= k, pltpu.roll(acs, k, axis=0), 0.0)
        k *= 2
    acs_t = acs.T
    dt_t = dtp.T

    e = e_ref[...]
    acs_last = jnp.concatenate(
        [jnp.broadcast_to(acs[c * CHUNK + CHUNK - 1:(c + 1) * CHUNK, :], (CHUNK, LANES)) for c in range(nchunk)],
        axis=0)
    w_exp = _expand_heads(dtp * jnp.exp(acs_last - acs), e)
    od_exp = _expand_heads(jnp.exp(acs), e)
    row8 = lax.broadcasted_iota(jnp.int32, (8, LANES), 0)
    cd8 = jnp.zeros((8, LANES), F32)
    for c in range(nchunk):
        cd8 = jnp.where(row8 == c, jnp.exp(acs[c * CHUNK + CHUNK - 1:(c + 1) * CHUNK, :]), cd8)
    cd_exp = _expand_heads(cd8, e)
    xw = (xs * w_exp).astype(BF16)

    tn = (((0,), (0,)), ((), ()))
    for c in range(nchunk):
        r0 = c * CHUNK
        for g in range(SSM_GROUPS):
            st = st_ref[g]
            cg = cmb[r0:r0 + CHUNK, g * SSM_STATE:(g + 1) * SSM_STATE]
            bg = bmb[r0:r0 + CHUNK, g * SSM_STATE:(g + 1) * SSM_STATE]
            y_ref[r0:r0 + CHUNK, g * gw:(g + 1) * gw] = jnp.dot(cg, st.astype(BF16), preferred_element_type=F32)
            snew = lax.dot_general(bg, xw[r0:r0 + CHUNK, g * gw:(g + 1) * gw], tn, preferred_element_type=F32)
            st_ref[g] = st * cd_exp[c:c + 1, g * gw:(g + 1) * gw] + snew
    y = y_ref[...] * od_exp + xs * dskip_ref[...]

    pair = 2 * CHUNK
    li = lax.broadcasted_iota(jnp.int32, (pair, pair), 0)
    si = lax.broadcasted_iota(jnp.int32, (pair, pair), 1)
    mask2 = (li >= si) & ((si >= CHUNK) | (li < CHUNK))
    lanep = lax.broadcasted_iota(jnp.int32, (pair, LANES), 1)
    nt = (((1,), (1,)), ((), ()))
    for pp in range(L // pair):
        r0 = pp * pair
        acs_p = acs[r0:r0 + pair, :]
        for g in range(SSM_GROUPS):
            cb2 = lax.dot_general(cmb[r0:r0 + pair, g * SSM_STATE:(g + 1) * SSM_STATE],
                                  bmb[r0:r0 + pair, g * SSM_STATE:(g + 1) * SSM_STATE],
                                  nt, preferred_element_type=F32)
            for hh in range(hpg // 2):
                hp = g * (hpg // 2) + hh
                mats = []
                for u in range(2):
                    hd = 2 * hp + u
                    seg = _lane_col(acs_p, hd) - acs_t[hd:hd + 1, r0:r0 + pair]
                    decay = jnp.exp(jnp.where(mask2, seg, -jnp.inf))
                    mats.append((cb2 * decay * dt_t[hd:hd + 1, r0:r0 + pair]).astype(BF16))
                lhs = jnp.concatenate(mats, axis=1)
                xp = xsb[r0:r0 + pair, hp * LANES:(hp + 1) * LANES]
                zero = jnp.zeros_like(xp)
                rhs = jnp.concatenate([jnp.where(lanep < SSM_HEAD_DIM, xp, zero),
                                       jnp.where(lanep >= SSM_HEAD_DIM, xp, zero)], axis=0)
                y_ref[r0:r0 + pair, hp * LANES:(hp + 1) * LANES] = jnp.dot(lhs, rhs, preferred_element_type=F32)
    y = y + y_ref[...]

    z = z_ref[...].astype(F32)
    y = y * (z * _sigmoid(z))
    outs = []
    for g in range(SSM_GROUPS):
        yg = y[:, g * gw:(g + 1) * gw]
        outs.append(yg * lax.rsqrt(jnp.mean(yg * yg, axis=-1, keepdims=True) + RMS_EPS))
    o_ref[...] = (jnp.concatenate(outs, axis=1) * nw_ref[...]).astype(o_ref.dtype)


def _ssd_mixer(proj, dt_raw, cw8, cb, dtb, alog, dskip_exp, nw, e_mat, B, S, side=()):
    T = B * S
    L = SSD_L
    nb = S // L
    row = lambda b, i: b * nb + i
    col0 = (N_Q + N_K + N_V + N_Z)
    const = lambda shape: pl.BlockSpec(shape, lambda b, i: (0, 0))
    steps = B * nb
    assert all(w.shape[0] % (8 * steps) == 0 for w in side)
    side_specs = [pl.BlockSpec((w.shape[0] // steps, w.shape[1]), lambda b, i: (row(b, i), 0)) for w in side]
    return pl.pallas_call(
        functools.partial(_ssd_kernel, n_side=len(side)),
        grid=(B, nb),
        in_specs=[
            pl.BlockSpec((L, N_Z), lambda b, i: (row(b, i), (N_Q + N_K + N_V) // N_Z)),
            pl.BlockSpec((L, SSM_WIDTH), lambda b, i: (row(b, i), col0 // SSM_WIDTH)),
            pl.BlockSpec((L, 256), lambda b, i: (row(b, i), (col0 + SSM_WIDTH) // 256)),
            pl.BlockSpec((L, 256), lambda b, i: (row(b, i), (col0 + SSM_WIDTH + 256) // 256)),
            pl.BlockSpec((L, LANES), lambda b, i: (row(b, i), 0)),
            const((8, N_XBC)),
            const((1, N_XBC)),
            const((1, LANES)),
            const((1, LANES)),
            const((1, SSM_WIDTH)),
            const((1, SSM_WIDTH)),
            const((LANES, SSM_WIDTH)),
        ] + side_specs,
        out_specs=[pl.BlockSpec((L, SSM_WIDTH), lambda b, i: (row(b, i), 0))] + side_specs,
        out_shape=[jax.ShapeDtypeStruct((T, SSM_WIDTH), BF16)]
        + [jax.ShapeDtypeStruct(w.shape, BF16) for w in side],
        scratch_shapes=[
            pltpu.VMEM((L + 8, N_XBC), F32),
            pltpu.VMEM((SSM_GROUPS, SSM_STATE, SSM_WIDTH // SSM_GROUPS), F32),
            pltpu.VMEM((L, SSM_WIDTH), F32),
        ],
        compiler_params=_cparams(("arbitrary", "arbitrary")),
        name="ssd_mixer",
    )(proj, proj, proj, proj, dt_raw, cw8, cb, dtb, alog, dskip_exp, nw, e_mat, *side)


def _layer_norm_rows(r, g, b):
    mu = jnp.mean(r, axis=-1, keepdims=True)
    d = r - mu
    var = jnp.mean(d * d, axis=-1, keepdims=True)
    return d * lax.rsqrt(var + LN_EPS) * g + b


def _outproj_kernel(att_ref, ssm_ref, x_ref, wa_ref, ws_ref, g_ref, b_ref, wr_ref, br_ref,
                    h_ref, eid_ref, ew_ref, cnt_ref):
    @pl.when(pl.program_id(0) == 0)
    def _():
        cnt_ref[...] = jnp.zeros_like(cnt_ref)

    chunk = min(OUTPROJ_CHUNK, h_ref.shape[0])
    for c in range(h_ref.shape[0] // chunk):
        rows = slice(c * chunk, (c + 1) * chunk)
        mix = (jnp.dot(att_ref[rows, :], wa_ref[...], preferred_element_type=F32)
               + jnp.dot(ssm_ref[rows, :], ws_ref[...], preferred_element_type=F32))
        h = _layer_norm_rows(DN_ALPHA * x_ref[rows, :] + mix, g_ref[...], b_ref[...])
        h_ref[rows, :] = h
        eid, ew, counts = _route(h, wr_ref[...], br_ref[...])
        eid_ref[rows, :] = eid
        ew_ref[rows, :] = ew
        cnt_ref[...] += counts


def _route(h, wr, br):
    logits = jnp.dot(h.astype(BF16), wr, preferred_element_type=F32) + br
    lane = lax.broadcasted_iota(jnp.int32, logits.shape, 1)
    lanef = lane.astype(F32)
    big = float(LANES)
    gl = jnp.where(lane < N_EXPERT_GROUPS, logits, -jnp.inf)
    gmax = jnp.max(gl, axis=-1, keepdims=True)
    g_prob = 1.0 / jnp.sum(jnp.exp(gl - gmax), axis=-1, keepdims=True)
    gidx = jnp.min(jnp.where(gl == gmax, lanef, big), axis=-1, keepdims=True)
    lo = N_EXPERT_GROUPS + EXPERTS_PER_GROUP * gidx
    el = jnp.where((lanef >= lo) & (lanef < lo + EXPERTS_PER_GROUP), logits, -jnp.inf)
    t1 = jnp.max(el, axis=-1, keepdims=True)
    i1 = jnp.min(jnp.where(el == t1, lanef, big), axis=-1, keepdims=True)
    el2 = jnp.where(lanef == i1, -jnp.inf, el)
    t2 = jnp.max(el2, axis=-1, keepdims=True)
    i2 = jnp.min(jnp.where(el2 == t2, lanef, big), axis=-1, keepdims=True)
    ex = jnp.exp(t2 - t1)
    w1 = g_prob / (1.0 + ex)
    w2 = g_prob * ex / (1.0 + ex)
    eid = jnp.where(lane == 0, i1 - N_EXPERT_GROUPS, jnp.where(lane == 1, i2 - N_EXPERT_GROUPS, 0.0))
    chosen = (lanef == i1 - N_EXPERT_GROUPS) | (lanef == i2 - N_EXPERT_GROUPS)
    counts = jnp.sum(chosen.astype(F32), axis=0, keepdims=True)
    return eid.astype(jnp.int32), jnp.where(lane == 0, w1, jnp.where(lane == 1, w2, 0.0)), counts


def _out_proj(att, ssm, x2d, wa, ws, g, b, wr, br, tm):
    T, D = x2d.shape
    const = lambda shape: pl.BlockSpec(shape, lambda i: (0, 0))
    rows = lambda w: pl.BlockSpec((tm, w), lambda i: (i, 0))
    return pl.pallas_call(
        _outproj_kernel,
        grid=(T // tm,),
        in_specs=[rows(ATT_WIDTH), rows(SSM_WIDTH), rows(D), const((ATT_WIDTH, D)), const((SSM_WIDTH, D)),
                  const((1, D)), const((1, D)), const((D, LANES)), const((1, LANES))],
        out_specs=[rows(D), rows(LANES), rows(LANES), const((8, LANES))],
        out_shape=[jax.ShapeDtypeStruct((T, D), F32),
                   jax.ShapeDtypeStruct((T, LANES), jnp.int32),
                   jax.ShapeDtypeStruct((T, LANES), F32),
                   jax.ShapeDtypeStruct((8, LANES), F32)],
        compiler_params=_cparams(("arbitrary",)),
        name="out_proj_ln1_router",
    )(att, ssm, x2d, wa, ws, g, b, wr, br)


def _pos_kernel(eid_ref, stril_ref, tot_ref, dest_ref, pend_ref, run_ref, pstart_ref):
    i = pl.program_id(0)
    tb = eid_ref.shape[0]
    lane = lax.broadcasted_iota(jnp.int32, (tb, LANES), 1)
    lanef = lane.astype(F32)
    ef = eid_ref[...].astype(F32)
    oh1 = (lanef == _lane_col(ef, 0)).astype(F32)
    oh2 = (lanef == _lane_col(ef, 1)).astype(F32)
    cnt = oh1 + oh2

    @pl.when(i == 0)
    def _():
        tot = tot_ref[...]
        padded = jnp.floor((tot + (MOE_BM - 1)) * (1.0 / MOE_BM)) * MOE_BM
        lane8 = lax.broadcasted_iota(jnp.int32, (8, LANES), 1)
        ends = padded
        k = 1
        while k < LANES:
            ends = ends + jnp.where(lane8 >= k, pltpu.roll(ends, k, axis=1), 0.0)
            k *= 2
        row8 = lax.broadcasted_iota(jnp.int32, (8, LANES), 0)
        pend_ref[...] = jnp.where(row8 == 0, ends, jnp.where(row8 == 1, tot, 0.0))
        pstart_ref[...] = (ends - padded)[0:1]
        run_ref[...] = jnp.zeros_like(run_ref)

    pre = jnp.dot(stril_ref[...], cnt.astype(BF16), preferred_element_type=F32)
    slot = pstart_ref[...] + run_ref[...] + pre
    d1 = jnp.sum(oh1 * slot, axis=-1, keepdims=True)
    d2 = jnp.sum(oh2 * slot, axis=-1, keepdims=True)
    dmat = jnp.where(lane == 0, d1, jnp.where(lane == 1, d2, 0.0))
    dest_ref[0] = dmat.T[:8, :].astype(jnp.int32)
    run_ref[...] += jnp.sum(cnt, axis=0, keepdims=True)


def _positions(eid, stril, tot):
    T = eid.shape[0]
    tb = POS_TB
    return pl.pallas_call(
        _pos_kernel,
        grid=(T // tb,),
        in_specs=[pl.BlockSpec((tb, LANES), lambda i: (i, 0)),
                  pl.BlockSpec((tb, tb), lambda i: (0, 0)),
                  pl.BlockSpec((8, LANES), lambda i: (0, 0))],
        out_specs=[pl.BlockSpec((1, 8, tb), lambda i: (i, 0, 0)),
                   pl.BlockSpec((8, LANES), lambda i: (0, 0))],
        out_shape=[jax.ShapeDtypeStruct((T // tb, 8, tb), jnp.int32),
                   jax.ShapeDtypeStruct((8, LANES), F32)],
        scratch_shapes=[pltpu.VMEM((1, LANES), F32), pltpu.VMEM((1, LANES), F32)],
        compiler_params=_cparams(("arbitrary",)),
        name="dispatch_positions",
    )(eid, stril, tot)


def _row_copy(src, s, dst, d, sem):
    return pltpu.make_async_copy(src.at[pl.ds(s, 1)], dst.at[pl.ds(d, 1)], sem)


def _scatter_kernel(zstart_ref, zcnt_ref, dest_ref, h_hbm, xs_hbm, zrow_ref, hbuf, in_sems, out_sems, zsem):
    i = pl.program_id(0)
    n = pl.num_programs(0)
    tb = hbuf.shape[1]

    @pl.when(i == 0)
    def _():
        zrow_ref[...] = jnp.zeros_like(zrow_ref)

        def fill_ops(e, act):
            cnt = zcnt_ref[e]
            start = zstart_ref[e]
            head = jnp.minimum((-start) & 7, cnt)
            lax.fori_loop(0, head, lambda r, c: (act(_row_copy(zrow_ref, 0, xs_hbm, start + r, zsem)), c)[1], 0)
            rest = cnt - head
            off = start + head
            size = 8
            while size < MOE_BM:
                @pl.when((rest & size) != 0)
                def _(off=off, size=size):
                    act(pltpu.make_async_copy(zrow_ref.at[pl.ds(0, size)],
                                              xs_hbm.at[pl.ds(pl.multiple_of(off, 8), size)], zsem))
                off = off + (rest & size)
                size *= 2

        def fill(e, c):
            fill_ops(e, lambda cp: cp.start())
            return c

        def fill_wait(e, c):
            fill_ops(e, lambda cp: cp.wait())
            return c

        nblk = xs_hbm.shape[0] // MOE_BM
        first_free = zstart_ref[N_EXPERTS]

        def tail_copy(b):
            return pltpu.make_async_copy(zrow_ref, xs_hbm.at[pl.ds(b * MOE_BM, MOE_BM)], zsem)

        lax.fori_loop(0, N_EXPERTS, fill, 0)
        lax.fori_loop(first_free, nblk, lambda b, c: (tail_copy(b).start(), c)[1], 0)
        lax.fori_loop(0, N_EXPERTS, fill_wait, 0)
        lax.fori_loop(first_free, nblk, lambda b, c: (tail_copy(b).wait(), c)[1], 0)

    slot = lax.rem(i, 3)
    nxt = lax.rem(i + 1, 3)
    prv = lax.rem(i + 2, 3)

    def load(b, s):
        return pltpu.make_async_copy(h_hbm.at[pl.ds(pl.multiple_of(b * tb, tb), tb)], hbuf.at[s], in_sems.at[s])

    def wait_rows(s):
        for _ in range(2):
            pltpu.make_async_copy(hbuf.at[s], xs_hbm.at[pl.ds(0, tb)], out_sems.at[s]).wait()

    @pl.when(i == 0)
    def _():
        load(0, 0).start()

    @pl.when(i + 1 < n)
    def _():
        load(i + 1, nxt).start()

    load(i, slot).wait()

    def issue(r, c):
        _row_copy(hbuf.at[slot], r, xs_hbm, dest_ref[0, 0, r], out_sems.at[slot]).start(priority=0)
        _row_copy(hbuf.at[slot], r, xs_hbm, dest_ref[0, 1, r], out_sems.at[slot]).start(priority=1)
        return c

    lax.fori_loop(0, tb, issue, 0, unroll=8)

    @pl.when(i > 0)
    def _():
        wait_rows(prv)

    @pl.when(i == n - 1)
    def _():
        wait_rows(slot)


def _dest_spec(dest, tb, step_of):
    per = dest.shape[2] // tb
    return pl.BlockSpec((1, 8, tb), lambda i, *_: (step_of(i) // per, 0, step_of(i) % per), memory_space=pltpu.SMEM)


def _dispatch(zstart, zcnt, dest, h, cap):
    T, D = h.shape
    tb = TOK_TB
    nb = T // tb
    grid_spec = pltpu.PrefetchScalarGridSpec(
        num_scalar_prefetch=2,
        grid=(nb,),
        in_specs=[_dest_spec(dest, tb, lambda i: i),
                  pl.BlockSpec(memory_space=pl.ANY)],
        out_specs=pl.BlockSpec(memory_space=pl.ANY),
        scratch_shapes=[pltpu.VMEM((MOE_BM, D), h.dtype), pltpu.VMEM((3, tb, D), h.dtype),
                        pltpu.SemaphoreType.DMA((3,)), pltpu.SemaphoreType.DMA((3,)), pltpu.SemaphoreType.DMA],
    )
    return pl.pallas_call(
        _scatter_kernel,
        grid_spec=grid_spec,
        out_shape=jax.ShapeDtypeStruct((cap, D), h.dtype),
        compiler_params=_cparams(("arbitrary",)),
        name="moe_dispatch",
    )(zstart, zcnt, dest, h)


def _expert_kernel(bexp_ref, nused_ref, x_ref, wg_ref, wu_ref, wd_ref, y_ref):
    i = pl.program_id(0)

    @pl.when(i < nused_ref[0])
    def _():
        x = x_ref[...].astype(BF16)
        gate = jnp.dot(x, wg_ref[0], preferred_element_type=F32)
        up = jnp.dot(x, wu_ref[0], preferred_element_type=F32)
        hid = (gate * _sigmoid(gate) * up).astype(BF16)
        y_ref[...] = jnp.dot(hid, wd_ref[0], preferred_element_type=F32)

    @pl.when(i >= nused_ref[0])
    def _():
        y_ref[...] = jnp.zeros_like(y_ref)


def _expert_mlp(bexp, nused, xs, wg, wu, wd):
    cap, D = xs.shape
    bm = MOE_BM
    H = wg.shape[2]
    grid_spec = pltpu.PrefetchScalarGridSpec(
        num_scalar_prefetch=2,
        grid=(cap // bm,),
        in_specs=[
            pl.BlockSpec((bm, D), lambda i, be, nu: (i, 0)),
            pl.BlockSpec((1, D, H), lambda i, be, nu: (be[i], 0, 0)),
            pl.BlockSpec((1, D, H), lambda i, be, nu: (be[i], 0, 0)),
            pl.BlockSpec((1, H, D), lambda i, be, nu: (be[i], 0, 0)),
        ],
        out_specs=pl.BlockSpec((bm, D), lambda i, be, nu: (i, 0)),
    )
    return pl.pallas_call(
        _expert_kernel,
        grid_spec=grid_spec,
        out_shape=jax.ShapeDtypeStruct((cap, D), F32),
        compiler_params=_cparams(("arbitrary",)),
        name="expert_mlp",
    )(bexp, nused, xs, wg, wu, wd)


def _combine_kernel(dest_ref, dnext_ref, h_ref, ew_ref, g_ref, b_ref, y_hbm, o_ref, ybuf, sems):
    i = pl.program_id(0)
    n = pl.num_programs(0)
    tb = h_ref.shape[0]
    slot = i & 1

    def gather(d_ref, s, unroll):
        def issue(r, c):
            _row_copy(y_hbm, d_ref[0, 0, r], ybuf.at[s, 0], r, sems.at[s]).start(priority=0)
            _row_copy(y_hbm, d_ref[0, 1, r], ybuf.at[s, 1], r, sems.at[s]).start(priority=1)
            return c
        lax.fori_loop(0, tb, issue, 0, unroll=unroll)

    def wait_slot(s):
        for k in range(2):
            pltpu.make_async_copy(y_hbm.at[pl.ds(0, tb)], ybuf.at[s, k], sems.at[s]).wait()

    @pl.when(i == 0)
    def _():
        gather(dest_ref, slot, 8)

    wait_slot(slot)
    gather(dnext_ref, 1 - slot, tb)
    ew = ew_ref[...]
    ffn = _lane_col(ew, 0) * ybuf[slot, 0] + _lane_col(ew, 1) * ybuf[slot, 1]
    o_ref[...] = _layer_norm_rows(DN_ALPHA * h_ref[...] + ffn, g_ref[...], b_ref[...])

    @pl.when(i == n - 1)
    def _():
        wait_slot(1 - slot)


def _combine(dest, h, ew, g, b, y):
    T, D = h.shape
    tb = TOK_TB
    nb = T // tb
    return pl.pallas_call(
        _combine_kernel,
        grid=(nb,),
        in_specs=[_dest_spec(dest, tb, lambda i: i),
                  _dest_spec(dest, tb, lambda i: jnp.minimum(i + 1, nb - 1)),
                  pl.BlockSpec((tb, D), lambda i: (i, 0)),
                  pl.BlockSpec((tb, LANES), lambda i: (i, 0)),
                  pl.BlockSpec((1, D), lambda i: (0, 0)),
                  pl.BlockSpec((1, D), lambda i: (0, 0)),
                  pl.BlockSpec(memory_space=pl.ANY)],
        out_specs=pl.BlockSpec((tb, D), lambda i: (i, 0)),
        out_shape=jax.ShapeDtypeStruct((T, D), F32),
        scratch_shapes=[pltpu.VMEM((2, 2, tb, D), F32), pltpu.SemaphoreType.DMA((2,))],
        compiler_params=_cparams(("arbitrary",)),
        name="moe_combine_ln2",
    )(dest, dest, h, ew, g, b, y)


def _pad_lanes(v, n=LANES):
    v = v.reshape(1, -1).astype(F32)
    return jnp.pad(v, ((0, 0), (0, n - v.shape[1])))


def kernel(x, w_in, lambda_q1, lambda_k1, lambda_q2, lambda_k2, attn_norm_w, conv_w, conv_b, dt_bias, a_log, d_skip, ssm_norm_w, w_out, ln1_g, ln1_b, w_router_group, b_router_group, w_router_expert, b_router_expert, w_gate, w_up, w_down, ln2_g, ln2_b):
    B, S, D = x.shape
    T = B * S
    assert w_in.shape[0] == DEPTH == 1
    assert S % ATT_TQ == 0 and S % SSD_L == 0 and T % POS_TB == 0 and POS_TB % TOK_TB == 0
    l = 0
    lambda_init = 0.8 - 0.6 * math.exp(-0.3 * l)
    x2d = x.reshape(T, D)

    w_main = w_in[l].astype(BF16)
    w_dt = jnp.pad(w_in[l][:, N_MAIN:], ((0, 0), (0, LANES - N_DT))).astype(BF16)
    slopes = jnp.exp2(-8.0 * jnp.arange(1, ATT_HEADS + 1, dtype=F32) / ATT_HEADS)
    lamp = jnp.concatenate([_pad_lanes(lambda_q1[l]), _pad_lanes(lambda_k1[l]),
                            _pad_lanes(lambda_q2[l]), _pad_lanes(lambda_k2[l]),
                            jnp.zeros((4, LANES), F32)], axis=0)
    nw_col = attn_norm_w[l].astype(F32).reshape(ATT_V_DIM, 1)
    cw8 = jnp.pad(conv_w[l].astype(F32), ((0, 8 - SSM_CONV), (0, 0)))
    cb = conv_b[l].astype(F32).reshape(1, N_XBC)
    dskip_exp = jnp.repeat(d_skip[l].astype(F32), SSM_HEAD_DIM).reshape(1, SSM_WIDTH)
    ssm_nw = ssm_norm_w[l].astype(F32).reshape(1, SSM_WIDTH)
    head_of_lane = jnp.arange(SSM_WIDTH, dtype=jnp.int32) // SSM_HEAD_DIM
    e_mat = (jnp.arange(LANES, dtype=jnp.int32)[:, None] == head_of_lane[None, :]).astype(BF16)
    wa = w_out[l][:ATT_WIDTH].astype(BF16)
    ws = w_out[l][ATT_WIDTH:].astype(BF16)
    wr = jnp.concatenate(
        [w_router_group[l], jnp.transpose(w_router_expert[l], (1, 0, 2)).reshape(D, N_EXPERTS)], axis=1)
    wr = jnp.pad(wr, ((0, 0), (0, LANES - wr.shape[1]))).astype(BF16)
    br = _pad_lanes(jnp.concatenate([b_router_group[l], b_router_expert[l].reshape(-1)]))
    row = lambda v: v.astype(F32).reshape(1, D)

    tm_in = INPROJ_TM if T % INPROJ_TM == 0 else 256
    proj, dt_raw = _in_proj(x2d, w_main, w_dt, tm_in, INPROJ_TN)
    att = _diff_attention(proj, slopes, lamp, nw_col, B, S, lambda_init)
    experts_f32 = [w[l].astype(F32).reshape(-1, w.shape[-1]) for w in (w_gate, w_up, w_down)]
    ssm, wg_b, wu_b, wd_b = _ssd_mixer(proj, dt_raw, cw8, cb, _pad_lanes(dt_bias[l]), _pad_lanes(a_log[l]),
                                       dskip_exp, ssm_nw, e_mat, B, S, side=experts_f32)
    wg_b, wu_b, wd_b = (w.reshape(src.shape[1:]) for w, src in zip((wg_b, wu_b, wd_b), (w_gate, w_up, w_down)))
    h1, eid, ew, expert_counts = _out_proj(att, ssm, x2d, wa, ws, row(ln1_g[l]), row(ln1_b[l]), wr, br,
                                           OUTPROJ_TM if T % OUTPROJ_TM == 0 else 256)

    stril = (jnp.arange(POS_TB)[:, None] > jnp.arange(POS_TB)[None, :]).astype(BF16)
    dest, pend = _positions(eid, stril, expert_counts)
    nblk = (T * 2) // MOE_BM + N_EXPERTS
    cap = nblk * MOE_BM
    pad_ends = pend[0, :N_EXPERTS].astype(jnp.int32)
    blk_start = jnp.arange(nblk, dtype=jnp.int32) * MOE_BM
    nused = (pad_ends[N_EXPERTS - 1] // MOE_BM).astype(jnp.int32)
    last_used = jnp.maximum(nused - 1, 0) * MOE_BM
    bexp = jnp.sum(pad_ends[None, :] <= jnp.minimum(blk_start, last_used)[:, None], axis=1).astype(jnp.int32)
    bexp = jnp.minimum(bexp, N_EXPERTS - 1)
    counts = pend[1, :N_EXPERTS].astype(jnp.int32)
    padded = pad_ends - jnp.concatenate([jnp.zeros((1,), jnp.int32), pad_ends[:-1]])
    zstart = jnp.concatenate([pad_ends - padded + counts, nused.reshape(1)])
    xs_sorted = _dispatch(zstart, padded - counts, dest, h1, cap)
    y_sorted = _expert_mlp(bexp, nused.reshape(1), xs_sorted, wg_b, wu_b, wd_b)
    out = _combine(dest, h1, ew, row(ln2_g[l]), row(ln2_b[l]), y_sorted)
    return out.reshape(B, S, D)
```

```python
import functools
import math

import jax
import jax.numpy as jnp
from jax import lax
from jax.experimental import pallas as pl
from jax.experimental.pallas import tpu as pltpu

F32 = jnp.float32
BF16 = jnp.bfloat16

CHUNK = 64
ATT_HEADS = 8
ATT_HEAD_DIM = 64
ATT_V_DIM = 128
ATT_WIDTH = ATT_HEADS * ATT_V_DIM
ATT_VT_ROWS = ATT_V_DIM + 16
LOG2E = 1.4426950408889634
SSM_HEADS = 16
SSM_HEAD_DIM = 64
SSM_WIDTH = SSM_HEADS * SSM_HEAD_DIM
SSM_GROUPS = 2
SSM_STATE = 128
SSM_CONV = 4
N_Q = 1024
N_K = 1024
N_V = 1024
N_Z = 1024
N_XBC = SSM_WIDTH + 2 * SSM_GROUPS * SSM_STATE
N_DT = SSM_HEADS
N_MAIN = N_Q + N_K + N_V + N_Z + N_XBC
N_EXPERT_GROUPS = 4
EXPERTS_PER_GROUP = 8
N_EXPERTS = 32
EXPERT_HIDDEN = 1024
DEPTH = 1
DN_ALPHA = (2 * DEPTH) ** 0.25
LN_EPS = 1e-5
RMS_EPS = 1e-6
LANES = 128

VMEM_LIMIT = 56 * 1024 * 1024

INPROJ_TM = 512
INPROJ_TN = 2816
OUTPROJ_TM = 512
OUTPROJ_CHUNK = 256
ATT_TAIL_STRIPS = 0
ATT_TQ = 512
ATT_TK = 512
SSD_L = 512
MOE_BM = 512
TOK_TB = 512
POS_TB = 1024


def _cparams(sem, flags=None):
    return pltpu.CompilerParams(dimension_semantics=sem, vmem_limit_bytes=VMEM_LIMIT, flags=flags)


def _sigmoid(x):
    return 1.0 / (1.0 + jnp.exp(-x))


def _lane_col(x, idx):
    lane = lax.broadcasted_iota(jnp.int32, x.shape, 1)
    return jnp.sum(jnp.where(lane == idx, x, 0.0), axis=-1, keepdims=True)


def _inproj_kernel(x_ref, w_ref, wdt_ref, o_ref, dt_ref, *, tn):
    xb = x_ref[...].astype(BF16)
    dt_ref[...] = jnp.dot(xb, wdt_ref[...], preferred_element_type=F32)
    for c in range(o_ref.shape[1] // tn):
        cols = slice(c * tn, (c + 1) * tn)
        o_ref[:, cols] = jnp.dot(xb, w_ref[:, cols], preferred_element_type=F32).astype(o_ref.dtype)


def _in_proj(x2d, w_main, w_dt, tm, tn):
    T, D = x2d.shape
    N = N_MAIN
    return pl.pallas_call(
        functools.partial(_inproj_kernel, tn=tn),
        grid=(T // tm,),
        in_specs=[
            pl.BlockSpec((tm, D), lambda i: (i, 0)),
            pl.BlockSpec((D, N), lambda i: (0, 0), pipeline_mode=pl.Buffered(1)),
            pl.BlockSpec((D, LANES), lambda i: (0, 0)),
        ],
        out_specs=[
            pl.BlockSpec((tm, N), lambda i: (i, 0)),
            pl.BlockSpec((tm, LANES), lambda i: (i, 0)),
        ],
        out_shape=[
            jax.ShapeDtypeStruct((T, N), BF16),
            jax.ShapeDtypeStruct((T, LANES), F32),
        ],
        compiler_params=_cparams(("arbitrary",)),
        name="in_proj",
    )(x2d, w_main, w_dt)


def _attn_kernel(slopes_ref, q_ref, k_ref, v_ref, lamp_ref, nw_ref, o_ref,
                 tab_ref, acc_ref, sa_ref, sb_ref, pa_ref, pb_ref, mxa_ref, mxb_ref, ala_ref, alb_ref,
                 qm_ref, vt_ref, *, lambda_init):
    h = pl.program_id(1)
    tk, tq = tab_ref.shape[1:]
    ratio = tq // tk
    nk = q_ref.shape[0] // tk
    nq = nk // ratio
    slope2 = slopes_ref[h] * LOG2E

    s_rel = lax.broadcasted_iota(jnp.int32, (tk, tq), 0)
    t_rel = lax.broadcasted_iota(jnp.int32, (tk, tq), 1)
    tab_ref[0] = slope2 * s_rel.astype(F32)
    for d in range(ratio):
        s_q = s_rel + d * tk
        allowed = (s_q // CHUNK) <= (t_rel // CHUNK)
        val = slope2 * (t_rel - jnp.abs(t_rel - s_q) - d * tk).astype(F32)
        tab_ref[1 + d] = jnp.where(allowed, val, -jnp.inf)

    def prep(i, c):
        rows = pl.ds(pl.multiple_of(i * tk, tk), tk)
        q = q_ref[rows, :]
        lane = lax.broadcasted_iota(jnp.int32, q.shape, 1)
        qs = (q.astype(F32) * (ATT_HEAD_DIM ** -0.5 * LOG2E)).astype(BF16)
        zero = jnp.zeros_like(qs)
        qm_ref[0, rows, :] = jnp.where(lane < ATT_HEAD_DIM, qs, zero)
        qm_ref[1, rows, :] = jnp.where(lane >= ATT_HEAD_DIM, qs, zero)
        vt_ref[i, :ATT_V_DIM, :] = v_ref[rows, :].astype(F32).T.astype(BF16)
        extra_row = lax.broadcasted_iota(jnp.int32, (ATT_VT_ROWS - ATT_V_DIM, tk), 0)
        vt_ref[i, ATT_V_DIM:, :] = (extra_row == 0).astype(BF16)
        return c

    lax.fori_loop(0, nk, prep, 0)
    nt = (((1,), (1,)), ((), ()))
    lamp = lamp_ref[...]
    lam = (jnp.exp(jnp.sum(lamp[0:1] * lamp[1:2], axis=-1, keepdims=True))
           - jnp.exp(jnp.sum(lamp[2:3] * lamp[3:4], axis=-1, keepdims=True)) + lambda_init)

    half = 256
    ncol = tq // half
    strip = 16

    def last_j(qi):
        return ratio * (qi + 1) - 1

    def scores_piece(pair, s_out, mx_out, m, c):
        qi, j = pair
        cols = slice(c * half, (c + 1) * half)
        kb = k_ref[pl.ds(pl.multiple_of(j * tk, tk), tk), :]
        qh = qm_ref[m, pl.ds(pl.multiple_of(qi * tq + c * half, half), half), :]
        kind = jnp.maximum(j - ratio * qi + 1, 0)
        st = lax.dot_general(kb, qh, nt, preferred_element_type=F32) + tab_ref[kind, :, cols]
        s_out[m, :, cols] = st
        mx_out[m, :, cols] = jnp.max(st, axis=0, keepdims=True)

    def accum_piece(pair, p_in, al_in, m, c):
        cols = slice(c * half, (c + 1) * half)
        acc_ref[m, :, cols] = (al_in[m][:, cols] * acc_ref[m, :, cols]
                               + jnp.dot(vt_ref[pair[1]], p_in[m, :, cols], preferred_element_type=F32))

    def finalize(qi):
        a0 = acc_ref[0]
        a1 = acc_ref[1]
        dv = ATT_V_DIM
        o = a0[:dv] / a0[dv:dv + 1] - lam * (a1[:dv] / a1[dv:dv + 1])
        ms2 = jnp.mean(o * o, axis=0, keepdims=True)
        o = o * lax.rsqrt(ms2 + RMS_EPS) * nw_ref[...] * (1.0 - lambda_init)
        o_ref[pl.ds(pl.multiple_of(qi * tq, tq), tq), :] = o.T.astype(o_ref.dtype)

    def next_pair(pair):
        qi, j = pair
        wrap = j == last_j(qi)
        return jnp.where(wrap, qi + 1, qi), jnp.where(wrap, 0, j + 1)

    def trip(cur_set, nxt_set, state):
        s_c, mx_c, p_c, al_c = cur_set
        s_n, mx_n, p_n, al_n = nxt_set
        prv, cur, m_old = state
        nxt = next_pair(cur)
        nxt_c = (jnp.minimum(nxt[0], nq - 1), jnp.where(nxt[0] >= nq, nk - 1, nxt[1]))
        prv_c = (prv[0], jnp.maximum(prv[1], 0))
        cj = -slope2 * (cur[0] * tq - cur[1] * tk).astype(F32)
        refs, m_out = [], []
        for m in range(2):
            m_prev = jnp.where(cur[1] == 0, -jnp.inf, m_old[m])
            m_new = jnp.maximum(m_prev, mx_c[m] + cj)
            al_c[m] = jnp.exp2(m_prev - m_new)
            refs.append(m_new - cj)
            m_out.append(m_new)

        acc_p = [functools.partial(accum_piece, prv_c, p_n, al_n, m, c) for m in range(2) for c in range(ncol)]
        sco_p = [functools.partial(scores_piece, nxt_c, s_n, mx_n, m, c) for m in range(2) for c in range(ncol)]
        pieces = acc_p[:ncol] + sco_p + acc_p[ncol:]
        sw = 512
        strips = [(m, r, w) for m in range(2) for r in range(tk // strip) for w in range(tq // sw)]
        def prob_strip(m, r, w):
            rows = slice(r * strip, (r + 1) * strip)
            cols = slice(w * sw, (w + 1) * sw)
            p_c[m, rows, cols] = jnp.exp2(s_c[m, rows, cols] - refs[m][:, cols]).astype(BF16)

        per = (len(strips) - ATT_TAIL_STRIPS) // len(pieces)
        for g, piece in enumerate(pieces):
            piece()
            for s in strips[g * per:(g + 1) * per]:
                prob_strip(*s)
        for s in strips[len(pieces) * per:]:
            prob_strip(*s)

        @pl.when(prv[1] == last_j(prv[0]))
        def _():
            finalize(prv[0])

        return cur, nxt, tuple(m_out)

    set_a = (sa_ref, mxa_ref, pa_ref, ala_ref)
    set_b = (sb_ref, mxb_ref, pb_ref, alb_ref)
    acc_ref[...] = jnp.zeros_like(acc_ref)
    pb_ref[...] = jnp.zeros_like(pb_ref)
    alb_ref[...] = jnp.ones_like(alb_ref)
    zero_i = jnp.int32(0)
    for m in range(2):
        for c in range(ncol):
            scores_piece((zero_i, zero_i), sa_ref, mxa_ref, m, c)

    def body(t, state):
        return trip(set_b, set_a, trip(set_a, set_b, state))

    n_pairs = ratio * nq * (nq + 1) // 2
    m_init = jnp.full((1, tq), -jnp.inf, F32)
    first = ((zero_i, jnp.int32(-1)), (zero_i, zero_i), (m_init, m_init))
    state = lax.fori_loop(0, n_pairs // 2, body, first)
    if n_pairs % 2:
        trip(set_a, set_b, state)

    last_set = set_a if (n_pairs - 1) % 2 == 0 else set_b
    last_pair = (jnp.int32(nq - 1), jnp.int32(nk - 1))
    for m in range(2):
        for c in range(ncol):
            accum_piece(last_pair, last_set[2], last_set[3], m, c)
    finalize(last_pair[0])


def _diff_attention(proj, slopes, lamp, nw_col, B, S, lambda_init):
    T = B * S
    tq, tk = ATT_TQ, ATT_TK
    nk = S // tk
    kern = functools.partial(_attn_kernel, lambda_init=lambda_init)
    grid_spec = pltpu.PrefetchScalarGridSpec(
        num_scalar_prefetch=1,
        grid=(B, ATT_HEADS),
        in_specs=[
            pl.BlockSpec((S, LANES), lambda b, h, s: (b, h)),
            pl.BlockSpec((S, LANES), lambda b, h, s: (b, N_Q // LANES + h)),
            pl.BlockSpec((S, LANES), lambda b, h, s: (b, (N_Q + N_K) // LANES + h)),
            pl.BlockSpec((8, LANES), lambda b, h, s: (0, 0)),
            pl.BlockSpec((ATT_V_DIM, 1), lambda b, h, s: (0, 0)),
        ],
        out_specs=pl.BlockSpec((S, ATT_V_DIM), lambda b, h, s: (b, h)),
        scratch_shapes=[
            pltpu.VMEM((1 + tq // tk, tk, tq), F32),
            pltpu.VMEM((2, ATT_VT_ROWS, tq), F32),
            pltpu.VMEM((2, tk, tq), F32), pltpu.VMEM((2, tk, tq), F32),
            pltpu.VMEM((2, tk, tq), BF16), pltpu.VMEM((2, tk, tq), BF16),
            pltpu.VMEM((2, 1, tq), F32), pltpu.VMEM((2, 1, tq), F32),
            pltpu.VMEM((2, 1, tq), F32), pltpu.VMEM((2, 1, tq), F32),
            pltpu.VMEM((2, S, LANES), BF16),
            pltpu.VMEM((nk, ATT_VT_ROWS, tk), BF16),
        ],
    )
    return pl.pallas_call(
        kern,
        grid_spec=grid_spec,
        out_shape=jax.ShapeDtypeStruct((T, ATT_WIDTH), BF16),
        compiler_params=_cparams(("arbitrary", "arbitrary")),
        name="diff_attention",
    )(slopes, proj, proj, proj, lamp, nw_col)


def _expand_heads(v, e):
    hi = v.astype(BF16)
    lo = (v - hi.astype(F32)).astype(BF16)
    return jnp.dot(hi, e, preferred_element_type=F32) + jnp.dot(lo, e, preferred_element_type=F32)


def _ssd_kernel(z_ref, xs_ref, b_ref, c_ref, dt_ref, cw_ref, cb_ref, dtb_ref, alog_ref,
                dskip_ref, nw_ref, e_ref, *rest, n_side):
    side_in = rest[:n_side]
    o_ref = rest[n_side]
    side_out = rest[n_side + 1:2 * n_side + 1]
    ext_ref, st_ref, y_ref = rest[2 * n_side + 1:]
    for w_in_ref, w_out_ref in zip(side_in, side_out):
        w_out_ref[...] = w_in_ref[...].astype(w_out_ref.dtype)
    blk = pl.program_id(1)
    L = z_ref.shape[0]
    nchunk = L // CHUNK
    gw = SSM_WIDTH // SSM_GROUPS
    hpg = SSM_HEADS // SSM_GROUPS

    @pl.when(blk == 0)
    def _():
        ext_ref[0:8, :] = jnp.zeros((8, N_XBC), F32)
        st_ref[...] = jnp.zeros_like(st_ref)

    cur = jnp.concatenate([xs_ref[...], b_ref[...], c_ref[...]], axis=1).astype(F32)
    ext_ref[8:, :] = cur
    cw = cw_ref[...]
    conv = cb_ref[...] + cw[3:4] * cur
    for j in range(SSM_CONV - 1):
        conv = conv + cw[j:j + 1] * ext_ref[pl.ds(8 - (SSM_CONV - 1) + j, L), :]
    ext_ref[0:8, :] = cur[L - 8:, :]
    xbc = conv * _sigmoid(conv)
    xs = xbc[:, :SSM_WIDTH]
    bmb = xbc[:, SSM_WIDTH:SSM_WIDTH + SSM_GROUPS * SSM_STATE].astype(BF16)
    cmb = xbc[:, SSM_WIDTH + SSM_GROUPS * SSM_STATE:].astype(BF16)
    xsb = xs.astype(BF16)

    lane1 = lax.broadcasted_iota(jnp.int32, (1, LANES), 1)
    dtx = dt_ref[...] + dtb_ref[...]
    dtp = jnp.maximum(dtx, 0.0) + jnp.log1p(jnp.exp(-jnp.abs(dtx)))
    a_head = jnp.where(lane1 < SSM_HEADS, -jnp.exp(alog_ref[...]), 0.0)
    acs = dtp * a_head
    row_in_chunk = lax.broadcasted_iota(jnp.int32, (L, LANES), 0) & (CHUNK - 1)
    k = 1
    while k < CHUNK:
        acs = acs + jnp.where(row_in_chunk >= k, pltpu.roll(acs, k, axis=0), 0.0)
        k *= 2
    acs_t = acs.T
    dt_t = dtp.T

    e = e_ref[...]
    acs_last = jnp.concatenate(
        [jnp.broadcast_to(acs[c * CHUNK + CHUNK - 1:(c + 1) * CHUNK, :], (CHUNK, LANES)) for c in range(nchunk)],
        axis=0)
    w_exp = _expand_heads(dtp * jnp.exp(acs_last - acs), e)
    od_exp = _expand_heads(jnp.exp(acs), e)
    row8 = lax.broadcasted_iota(jnp.int32, (8, LANES), 0)
    cd8 = jnp.zeros((8, LANES), F32)
    for c in range(nchunk):
        cd8 = jnp.where(row8 == c, jnp.exp(acs[c * CHUNK + CHUNK - 1:(c + 1) * CHUNK, :]), cd8)
    cd_exp = _expand_heads(cd8, e)
    xw = (xs * w_exp).astype(BF16)

    tn = (((0,), (0,)), ((), ()))
    for c in range(nchunk):
        r0 = c * CHUNK
        for g in range(SSM_GROUPS):
            st = st_ref[g]
            cg = cmb[r0:r0 + CHUNK, g * SSM_STATE:(g + 1) * SSM_STATE]
            bg = bmb[r0:r0 + CHUNK, g * SSM_STATE:(g + 1) * SSM_STATE]
            y_ref[r0:r0 + CHUNK, g * gw:(g + 1) * gw] = jnp.dot(cg, st.astype(BF16), preferred_element_type=F32)
            snew = lax.dot_general(bg, xw[r0:r0 + CHUNK, g * gw:(g + 1) * gw], tn, preferred_element_type=F32)
            st_ref[g] = st * cd_exp[c:c + 1, g * gw:(g + 1) * gw] + snew
    y = y_ref[...] * od_exp + xs * dskip_ref[...]

    pair = 2 * CHUNK
    li = lax.broadcasted_iota(jnp.int32, (pair, pair), 0)
    si = lax.broadcasted_iota(jnp.int32, (pair, pair), 1)
    mask2 = (li >= si) & ((si >= CHUNK) | (li < CHUNK))
    lanep = lax.broadcasted_iota(jnp.int32, (pair, LANES), 1)
    nt = (((1,), (1,)), ((), ()))
    for pp in range(L // pair):
        r0 = pp * pair
        acs_p = acs[r0:r0 + pair, :]
        for g in range(SSM_GROUPS):
            cb2 = lax.dot_general(cmb[r0:r0 + pair, g * SSM_STATE:(g + 1) * SSM_STATE],
                                  bmb[r0:r0 + pair, g * SSM_STATE:(g + 1) * SSM_STATE],
                                  nt, preferred_element_type=F32)
            for hh in range(hpg // 2):
                hp = g * (hpg // 2) + hh
                mats = []
                for u in range(2):
                    hd = 2 * hp + u
                    seg = _lane_col(acs_p, hd) - acs_t[hd:hd + 1, r0:r0 + pair]
                    decay = jnp.exp(jnp.where(mask2, seg, -jnp.inf))
                    mats.append((cb2 * decay * dt_t[hd:hd + 1, r0:r0 + pair]).astype(BF16))
                lhs = jnp.concatenate(mats, axis=1)
                xp = xsb[r0:r0 + pair, hp * LANES:(hp + 1) * LANES]
                zero = jnp.zeros_like(xp)
                rhs = jnp.concatenate([jnp.where(lanep < SSM_HEAD_DIM, xp, zero),
                                       jnp.where(lanep >= SSM_HEAD_DIM, xp, zero)], axis=0)
                y_ref[r0:r0 + pair, hp * LANES:(hp + 1) * LANES] = jnp.dot(lhs, rhs, preferred_element_type=F32)
    y = y + y_ref[...]

    z = z_ref[...].astype(F32)
    y = y * (z * _sigmoid(z))
    outs = []
    for g in range(SSM_GROUPS):
        yg = y[:, g * gw:(g + 1) * gw]
        outs.append(yg * lax.rsqrt(jnp.mean(yg * yg, axis=-1, keepdims=True) + RMS_EPS))
    o_ref[...] = (jnp.concatenate(outs, axis=1) * nw_ref[...]).astype(o_ref.dtype)


def _ssd_mixer(proj, dt_raw, cw8, cb, dtb, alog, dskip_exp, nw, e_mat, B, S, side=()):
    T = B * S
    L = SSD_L
    nb = S // L
    row = lambda b, i: b * nb + i
    col0 = (N_Q + N_K + N_V + N_Z)
    const = lambda shape: pl.BlockSpec(shape, lambda b, i: (0, 0))
    steps = B * nb
    assert all(w.shape[0] % (8 * steps) == 0 for w in side)
    side_specs = [pl.BlockSpec((w.shape[0] // steps, w.shape[1]), lambda b, i: (row(b, i), 0)) for w in side]
    return pl.pallas_call(
        functools.partial(_ssd_kernel, n_side=len(side)),
        grid=(B, nb),
        in_specs=[
            pl.BlockSpec((L, N_Z), lambda b, i: (row(b, i), (N_Q + N_K + N_V) // N_Z)),
            pl.BlockSpec((L, SSM_WIDTH), lambda b, i: (row(b, i), col0 // SSM_WIDTH)),
            pl.BlockSpec((L, 256), lambda b, i: (row(b, i), (col0 + SSM_WIDTH) // 256)),
            pl.BlockSpec((L, 256), lambda b, i: (row(b, i), (col0 + SSM_WIDTH + 256) // 256)),
            pl.BlockSpec((L, LANES), lambda b, i: (row(b, i), 0)),
            const((8, N_XBC)),
            const((1, N_XBC)),
            const((1, LANES)),
            const((1, LANES)),
            const((1, SSM_WIDTH)),
            const((1, SSM_WIDTH)),
            const((LANES, SSM_WIDTH)),
        ] + side_specs,
        out_specs=[pl.BlockSpec((L, SSM_WIDTH), lambda b, i: (row(b, i), 0))] + side_specs,
        out_shape=[jax.ShapeDtypeStruct((T, SSM_WIDTH), BF16)]
        + [jax.ShapeDtypeStruct(w.shape, BF16) for w in side],
        scratch_shapes=[
            pltpu.VMEM((L + 8, N_XBC), F32),
            pltpu.VMEM((SSM_GROUPS, SSM_STATE, SSM_WIDTH // SSM_GROUPS), F32),
            pltpu.VMEM((L, SSM_WIDTH), F32),
        ],
        compiler_params=_cparams(("arbitrary", "arbitrary")),
        name="ssd_mixer",
    )(proj, proj, proj, proj, dt_raw, cw8, cb, dtb, alog, dskip_exp, nw, e_mat, *side)


def _layer_norm_rows(r, g, b):
    mu = jnp.mean(r, axis=-1, keepdims=True)
    d = r - mu
    var = jnp.mean(d * d, axis=-1, keepdims=True)
    return d * lax.rsqrt(var + LN_EPS) * g + b


def _outproj_kernel(att_ref, ssm_ref, x_ref, wa_ref, ws_ref, g_ref, b_ref, wr_ref, br_ref,
                    h_ref, eid_ref, ew_ref, cnt_ref):
    @pl.when(pl.program_id(0) == 0)
    def _():
        cnt_ref[...] = jnp.zeros_like(cnt_ref)

    chunk = min(OUTPROJ_CHUNK, h_ref.shape[0])
    for c in range(h_ref.shape[0] // chunk):
        rows = slice(c * chunk, (c + 1) * chunk)
        mix = (jnp.dot(att_ref[rows, :], wa_ref[...], preferred_element_type=F32)
               + jnp.dot(ssm_ref[rows, :], ws_ref[...], preferred_element_type=F32))
        h = _layer_norm_rows(DN_ALPHA * x_ref[rows, :] + mix, g_ref[...], b_ref[...])
        h_ref[rows, :] = h
        eid, ew, counts = _route(h, wr_ref[...], br_ref[...])
        eid_ref[rows, :] = eid
        ew_ref[rows, :] = ew
        cnt_ref[...] += counts


def _route(h, wr, br):
    logits = jnp.dot(h.astype(BF16), wr, preferred_element_type=F32) + br
    lane = lax.broadcasted_iota(jnp.int32, logits.shape, 1)
    lanef = lane.astype(F32)
    big = float(LANES)
    gl = jnp.where(lane < N_EXPERT_GROUPS, logits, -jnp.inf)
    gmax = jnp.max(gl, axis=-1, keepdims=True)
    g_prob = 1.0 / jnp.sum(jnp.exp(gl - gmax), axis=-1, keepdims=True)
    gidx = jnp.min(jnp.where(gl == gmax, lanef, big), axis=-1, keepdims=True)
    lo = N_EXPERT_GROUPS + EXPERTS_PER_GROUP * gidx
    el = jnp.where((lanef >= lo) & (lanef < lo + EXPERTS_PER_GROUP), logits, -jnp.inf)
    t1 = jnp.max(el, axis=-1, keepdims=True)
    i1 = jnp.min(jnp.where(el == t1, lanef, big), axis=-1, keepdims=True)
    el2 = jnp.where(lanef == i1, -jnp.inf, el)
    t2 = jnp.max(el2, axis=-1, keepdims=True)
    i2 = jnp.min(jnp.where(el2 == t2, lanef, big), axis=-1, keepdims=True)
    ex = jnp.exp(t2 - t1)
    w1 = g_prob / (1.0 + ex)
    w2 = g_prob * ex / (1.0 + ex)
    eid = jnp.where(lane == 0, i1 - N_EXPERT_GROUPS, jnp.where(lane == 1, i2 - N_EXPERT_GROUPS, 0.0))
    chosen = (lanef == i1 - N_EXPERT_GROUPS) | (lanef == i2 - N_EXPERT_GROUPS)
    counts = jnp.sum(chosen.astype(F32), axis=0, keepdims=True)
    return eid.astype(jnp.int32), jnp.where(lane == 0, w1, jnp.where(lane == 1, w2, 0.0)), counts


def _out_proj(att, ssm, x2d, wa, ws, g, b, wr, br, tm):
    T, D = x2d.shape
    const = lambda shape: pl.BlockSpec(shape, lambda i: (0, 0))
    rows = lambda w: pl.BlockSpec((tm, w), lambda i: (i, 0))
    return pl.pallas_call(
        _outproj_kernel,
        grid=(T // tm,),
        in_specs=[rows(ATT_WIDTH), rows(SSM_WIDTH), rows(D), const((ATT_WIDTH, D)), const((SSM_WIDTH, D)),
                  const((1, D)), const((1, D)), const((D, LANES)), const((1, LANES))],
        out_specs=[rows(D), rows(LANES), rows(LANES), const((8, LANES))],
        out_shape=[jax.ShapeDtypeStruct((T, D), F32),
                   jax.ShapeDtypeStruct((T, LANES), jnp.int32),
                   jax.ShapeDtypeStruct((T, LANES), F32),
                   jax.ShapeDtypeStruct((8, LANES), F32)],
        compiler_params=_cparams(("arbitrary",)),
        name="out_proj_ln1_router",
    )(att, ssm, x2d, wa, ws, g, b, wr, br)


def _pos_kernel(eid_ref, stril_ref, tot_ref, dest_ref, pend_ref, run_ref, pstart_ref):
    i = pl.program_id(0)
    tb = eid_ref.shape[0]
    lane = lax.broadcasted_iota(jnp.int32, (tb, LANES), 1)
    lanef = lane.astype(F32)
    ef = eid_ref[...].astype(F32)
    oh1 = (lanef == _lane_col(ef, 0)).astype(F32)
    oh2 = (lanef == _lane_col(ef, 1)).astype(F32)
    cnt = oh1 + oh2

    @pl.when(i == 0)
    def _():
        tot = tot_ref[...]
        padded = jnp.floor((tot + (MOE_BM - 1)) * (1.0 / MOE_BM)) * MOE_BM
        lane8 = lax.broadcasted_iota(jnp.int32, (8, LANES), 1)
        ends = padded
        k = 1
        while k < LANES:
            ends = ends + jnp.where(lane8 >= k, pltpu.roll(ends, k, axis=1), 0.0)
            k *= 2
        row8 = lax.broadcasted_iota(jnp.int32, (8, LANES), 0)
        pend_ref[...] = jnp.where(row8 == 0, ends, jnp.where(row8 == 1, tot, 0.0))
        pstart_ref[...] = (ends - padded)[0:1]
        run_ref[...] = jnp.zeros_like(run_ref)

    pre = jnp.dot(stril_ref[...], cnt.astype(BF16), preferred_element_type=F32)
    slot = pstart_ref[...] + run_ref[...] + pre
    d1 = jnp.sum(oh1 * slot, axis=-1, keepdims=True)
    d2 = jnp.sum(oh2 * slot, axis=-1, keepdims=True)
    dmat = jnp.where(lane == 0, d1, jnp.where(lane == 1, d2, 0.0))
    dest_ref[0] = dmat.T[:8, :].astype(jnp.int32)
    run_ref[...] += jnp.sum(cnt, axis=0, keepdims=True)


def _positions(eid, stril, tot):
    T = eid.shape[0]
    tb = POS_TB
    return pl.pallas_call(
        _pos_kernel,
        grid=(T // tb,),
        in_specs=[pl.BlockSpec((tb, LANES), lambda i: (i, 0)),
                  pl.BlockSpec((tb, tb), lambda i: (0, 0)),
                  pl.BlockSpec((8, LANES), lambda i: (0, 0))],
        out_specs=[pl.BlockSpec((1, 8, tb), lambda i: (i, 0, 0)),
                   pl.BlockSpec((8, LANES), lambda i: (0, 0))],
        out_shape=[jax.ShapeDtypeStruct((T // tb, 8, tb), jnp.int32),
                   jax.ShapeDtypeStruct((8, LANES), F32)],
        scratch_shapes=[pltpu.VMEM((1, LANES), F32), pltpu.VMEM((1, LANES), F32)],
        compiler_params=_cparams(("arbitrary",)),
        name="dispatch_positions",
    )(eid, stril, tot)


def _row_copy(src, s, dst, d, sem):
    return pltpu.make_async_copy(src.at[pl.ds(s, 1)], dst.at[pl.ds(d, 1)], sem)


def _scatter_kernel(zstart_ref, zcnt_ref, dest_ref, h_hbm, xs_hbm, zrow_ref, hbuf, in_sems, out_sems, zsem):
    i = pl.program_id(0)
    n = pl.num_programs(0)
    tb = hbuf.shape[1]

    @pl.when(i == 0)
    def _():
        zrow_ref[...] = jnp.zeros_like(zrow_ref)

        def fill_ops(e, act):
            cnt = zcnt_ref[e]
            start = zstart_ref[e]
            head = jnp.minimum((-start) & 7, cnt)
            lax.fori_loop(0, head, lambda r, c: (act(_row_copy(zrow_ref, 0, xs_hbm, start + r, zsem)), c)[1], 0)
            rest = cnt - head
            off = start + head
            size = 8
            while size < MOE_BM:
                @pl.when((rest & size) != 0)
                def _(off=off, size=size):
                    act(pltpu.make_async_copy(zrow_ref.at[pl.ds(0, size)],
                                              xs_hbm.at[pl.ds(pl.multiple_of(off, 8), size)], zsem))
                off = off + (rest & size)
                size *= 2

        def fill(e, c):
            fill_ops(e, lambda cp: cp.start())
            return c

        def fill_wait(e, c):
            fill_ops(e, lambda cp: cp.wait())
            return c

        nblk = xs_hbm.shape[0] // MOE_BM
        first_free = zstart_ref[N_EXPERTS]

        def tail_copy(b):
            return pltpu.make_async_copy(zrow_ref, xs_hbm.at[pl.ds(b * MOE_BM, MOE_BM)], zsem)

        lax.fori_loop(0, N_EXPERTS, fill, 0)
        lax.fori_loop(first_free, nblk, lambda b, c: (tail_copy(b).start(), c)[1], 0)
        lax.fori_loop(0, N_EXPERTS, fill_wait, 0)
        lax.fori_loop(first_free, nblk, lambda b, c: (tail_copy(b).wait(), c)[1], 0)

    slot = lax.rem(i, 3)
    nxt = lax.rem(i + 1, 3)
    prv = lax.rem(i + 2, 3)

    def load(b, s):
        return pltpu.make_async_copy(h_hbm.at[pl.ds(pl.multiple_of(b * tb, tb), tb)], hbuf.at[s], in_sems.at[s])

    def wait_rows(s):
        for _ in range(2):
            pltpu.make_async_copy(hbuf.at[s], xs_hbm.at[pl.ds(0, tb)], out_sems.at[s]).wait()

    @pl.when(i == 0)
    def _():
        load(0, 0).start()

    @pl.when(i + 1 < n)
    def _():
        load(i + 1, nxt).start()

    load(i, slot).wait()

    def issue(r, c):
        _row_copy(hbuf.at[slot], r, xs_hbm, dest_ref[0, 0, r], out_sems.at[slot]).start(priority=0)
        _row_copy(hbuf.at[slot], r, xs_hbm, dest_ref[0, 1, r], out_sems.at[slot]).start(priority=1)
        return c

    lax.fori_loop(0, tb, issue, 0, unroll=8)

    @pl.when(i > 0)
    def _():
        wait_rows(prv)

    @pl.when(i == n - 1)
    def _():
        wait_rows(slot)


def _dest_spec(dest, tb, step_of):
    per = dest.shape[2] // tb
    return pl.BlockSpec((1, 8, tb), lambda i, *_: (step_of(i) // per, 0, step_of(i) % per), memory_space=pltpu.SMEM)


def _dispatch(zstart, zcnt, dest, h, cap):
    T, D = h.shape
    tb = TOK_TB
    nb = T // tb
    grid_spec = pltpu.PrefetchScalarGridSpec(
        num_scalar_prefetch=2,
        grid=(nb,),
        in_specs=[_dest_spec(dest, tb, lambda i: i),
                  pl.BlockSpec(memory_space=pl.ANY)],
        out_specs=pl.BlockSpec(memory_space=pl.ANY),
        scratch_shapes=[pltpu.VMEM((MOE_BM, D), h.dtype), pltpu.VMEM((3, tb, D), h.dtype),
                        pltpu.SemaphoreType.DMA((3,)), pltpu.SemaphoreType.DMA((3,)), pltpu.SemaphoreType.DMA],
    )
    return pl.pallas_call(
        _scatter_kernel,
        grid_spec=grid_spec,
        out_shape=jax.ShapeDtypeStruct((cap, D), h.dtype),
        compiler_params=_cparams(("arbitrary",)),
        name="moe_dispatch",
    )(zstart, zcnt, dest, h)


def _expert_kernel(bexp_ref, nused_ref, x_ref, wg_ref, wu_ref, wd_ref, y_ref):
    i = pl.program_id(0)

    @pl.when(i < nused_ref[0])
    def _():
        x = x_ref[...].astype(BF16)
        gate = jnp.dot(x, wg_ref[0], preferred_element_type=F32)
        up = jnp.dot(x, wu_ref[0], preferred_element_type=F32)
        hid = (gate * _sigmoid(gate) * up).astype(BF16)
        y_ref[...] = jnp.dot(hid, wd_ref[0], preferred_element_type=F32)

    @pl.when(i >= nused_ref[0])
    def _():
        y_ref[...] = jnp.zeros_like(y_ref)


def _expert_mlp(bexp, nused, xs, wg, wu, wd):
    cap, D = xs.shape
    bm = MOE_BM
    H = wg.shape[2]
    grid_spec = pltpu.PrefetchScalarGridSpec(
        num_scalar_prefetch=2,
        grid=(cap // bm,),
        in_specs=[
            pl.BlockSpec((bm, D), lambda i, be, nu: (i, 0)),
            pl.BlockSpec((1, D, H), lambda i, be, nu: (be[i], 0, 0)),
            pl.BlockSpec((1, D, H), lambda i, be, nu: (be[i], 0, 0)),
            pl.BlockSpec((1, H, D), lambda i, be, nu: (be[i], 0, 0)),
        ],
        out_specs=pl.BlockSpec((bm, D), lambda i, be, nu: (i, 0)),
    )
    return pl.pallas_call(
        _expert_kernel,
        grid_spec=grid_spec,
        out_shape=jax.ShapeDtypeStruct((cap, D), F32),
        compiler_params=_cparams(("arbitrary",)),
        name="expert_mlp",
    )(bexp, nused, xs, wg, wu, wd)


def _combine_kernel(dest_ref, dnext_ref, h_ref, ew_ref, g_ref, b_ref, y_hbm, o_ref, ybuf, sems):
    i = pl.program_id(0)
    n = pl.num_programs(0)
    tb = h_ref.shape[0]
    slot = i & 1

    def gather(d_ref, s, unroll):
        def issue(r, c):
            _row_copy(y_hbm, d_ref[0, 0, r], ybuf.at[s, 0], r, sems.at[s]).start(priority=0)
            _row_copy(y_hbm, d_ref[0, 1, r], ybuf.at[s, 1], r, sems.at[s]).start(priority=1)
            return c
        lax.fori_loop(0, tb, issue, 0, unroll=unroll)

    def wait_slot(s):
        for k in range(2):
            pltpu.make_async_copy(y_hbm.at[pl.ds(0, tb)], ybuf.at[s, k], sems.at[s]).wait()

    @pl.when(i == 0)
    def _():
        gather(dest_ref, slot, 8)

    wait_slot(slot)
    gather(dnext_ref, 1 - slot, tb)
    ew = ew_ref[...]
    ffn = _lane_col(ew, 0) * ybuf[slot, 0] + _lane_col(ew, 1) * ybuf[slot, 1]
    o_ref[...] = _layer_norm_rows(DN_ALPHA * h_ref[...] + ffn, g_ref[...], b_ref[...])

    @pl.when(i == n - 1)
    def _():
        wait_slot(1 - slot)


def _combine(dest, h, ew, g, b, y):
    T, D = h.shape
    tb = TOK_TB
    nb = T // tb
    return pl.pallas_call(
        _combine_kernel,
        grid=(nb,),
        in_specs=[_dest_spec(dest, tb, lambda i: i),
                  _dest_spec(dest, tb, lambda i: jnp.minimum(i + 1, nb - 1)),
                  pl.BlockSpec((tb, D), lambda i: (i, 0)),
                  pl.BlockSpec((tb, LANES), lambda i: (i, 0)),
                  pl.BlockSpec((1, D), lambda i: (0, 0)),
                  pl.BlockSpec((1, D), lambda i: (0, 0)),
                  pl.BlockSpec(memory_space=pl.ANY)],
        out_specs=pl.BlockSpec((tb, D), lambda i: (i, 0)),
        out_shape=jax.ShapeDtypeStruct((T, D), F32),
        scratch_shapes=[pltpu.VMEM((2, 2, tb, D), F32), pltpu.SemaphoreType.DMA((2,))],
        compiler_params=_cparams(("arbitrary",)),
        name="moe_combine_ln2",
    )(dest, dest, h, ew, g, b, y)


def _pad_lanes(v, n=LANES):
    v = v.reshape(1, -1).astype(F32)
    return jnp.pad(v, ((0, 0), (0, n - v.shape[1])))


def kernel(x, w_in, lambda_q1, lambda_k1, lambda_q2, lambda_k2, attn_norm_w, conv_w, conv_b, dt_bias, a_log, d_skip, ssm_norm_w, w_out, ln1_g, ln1_b, w_router_group, b_router_group, w_router_expert, b_router_expert, w_gate, w_up, w_down, ln2_g, ln2_b):
    B, S, D = x.shape
    T = B * S
    assert w_in.shape[0] == DEPTH == 1
    assert S % ATT_TQ == 0 and S % SSD_L == 0 and T % POS_TB == 0 and POS_TB % TOK_TB == 0
    l = 0
    lambda_init = 0.8 - 0.6 * math.exp(-0.3 * l)
    x2d = x.reshape(T, D)

    w_main = w_in[l].astype(BF16)
    w_dt = jnp.pad(w_in[l][:, N_MAIN:], ((0, 0), (0, LANES - N_DT))).astype(BF16)
    slopes = jnp.exp2(-8.0 * jnp.arange(1, ATT_HEADS + 1, dtype=F32) / ATT_HEADS)
    lamp = jnp.concatenate([_pad_lanes(lambda_q1[l]), _pad_lanes(lambda_k1[l]),
                            _pad_lanes(lambda_q2[l]), _pad_lanes(lambda_k2[l]),
                            jnp.zeros((4, LANES), F32)], axis=0)
    nw_col = attn_norm_w[l].astype(F32).reshape(ATT_V_DIM, 1)
    cw8 = jnp.pad(conv_w[l].astype(F32), ((0, 8 - SSM_CONV), (0, 0)))
    cb = conv_b[l].astype(F32).reshape(1, N_XBC)
    dskip_exp = jnp.repeat(d_skip[l].astype(F32), SSM_HEAD_DIM).reshape(1, SSM_WIDTH)
    ssm_nw = ssm_norm_w[l].astype(F32).reshape(1, SSM_WIDTH)
    head_of_lane = jnp.arange(SSM_WIDTH, dtype=jnp.int32) // SSM_HEAD_DIM
    e_mat = (jnp.arange(LANES, dtype=jnp.int32)[:, None] == head_of_lane[None, :]).astype(BF16)
    wa = w_out[l][:ATT_WIDTH].astype(BF16)
    ws = w_out[l][ATT_WIDTH:].astype(BF16)
    wr = jnp.concatenate(
        [w_router_group[l], jnp.transpose(w_router_expert[l], (1, 0, 2)).reshape(D, N_EXPERTS)], axis=1)
    wr = jnp.pad(wr, ((0, 0), (0, LANES - wr.shape[1]))).astype(BF16)
    br = _pad_lanes(jnp.concatenate([b_router_group[l], b_router_expert[l].reshape(-1)]))
    row = lambda v: v.astype(F32).reshape(1, D)

    tm_in = INPROJ_TM if T % INPROJ_TM == 0 else 256
    proj, dt_raw = _in_proj(x2d, w_main, w_dt, tm_in, INPROJ_TN)
    att = _diff_attention(proj, slopes, lamp, nw_col, B, S, lambda_init)
    experts_f32 = [w[l].astype(F32).reshape(-1, w.shape[-1]) for w in (w_gate, w_up, w_down)]
    ssm, wg_b, wu_b, wd_b = _ssd_mixer(proj, dt_raw, cw8, cb, _pad_lanes(dt_bias[l]), _pad_lanes(a_log[l]),
                                       dskip_exp, ssm_nw, e_mat, B, S, side=experts_f32)
    wg_b, wu_b, wd_b = (w.reshape(src.shape[1:]) for w, src in zip((wg_b, wu_b, wd_b), (w_gate, w_up, w_down)))
    h1, eid, ew, expert_counts = _out_proj(att, ssm, x2d, wa, ws, row(ln1_g[l]), row(ln1_b[l]), wr, br,
                                           OUTPROJ_TM if T % OUTPROJ_TM == 0 else 256)

    stril = (jnp.arange(POS_TB)[:, None] > jnp.arange(POS_TB)[None, :]).astype(BF16)
    dest, pend = _positions(eid, stril, expert_counts)
    nblk = (T * 2) // MOE_BM + N_EXPERTS
    cap = nblk * MOE_BM
    pad_ends = pend[0, :N_EXPERTS].astype(jnp.int32)
    blk_start = jnp.arange(nblk, dtype=jnp.int32) * MOE_BM
    nused = (pad_ends[N_EXPERTS - 1] // MOE_BM).astype(jnp.int32)
    last_used = jnp.maximum(nused - 1, 0) * MOE_BM
    bexp = jnp.sum(pad_ends[None, :] <= jnp.minimum(blk_start, last_used)[:, None], axis=1).astype(jnp.int32)
    bexp = jnp.minimum(bexp, N_EXPERTS - 1)
    counts = pend[1, :N_EXPERTS].astype(jnp.int32)
    padded = pad_ends - jnp.concatenate([jnp.zeros((1,), jnp.int32), pad_ends[:-1]])
    zstart = jnp.concatenate([pad_ends - padded + counts, nused.reshape(1)])
    xs_sorted = _dispatch(zstart, padded - counts, dest, h1, cap)
    y_sorted = _expert_mlp(bexp, nused.reshape(1), xs_sorted, wg_b, wu_b, wd_b)
    out = _combine(dest, h1, ew, row(ln2_g[l]), row(ln2_b[l]), y_sorted)
    return out.reshape(B, S, D)
```
